```python
import jax, jax.numpy as jnp
from jax import lax
import numpy as np

D_MODEL = 2048
BATCH = 2
SEQ = 8192
DEPTH = 1

HEAD_DIM = 128
ATTN_GROUPS = ((128, 1), (512, 4), (2048, 16))
N_GROUPS = 3
ATTN_HEADS_PER_GROUP = 8
ATTN_QKV_WIDTH = N_GROUPS * ATTN_HEADS_PER_GROUP * HEAD_DIM
ATTN_OUT_WIDTH = ATTN_HEADS_PER_GROUP * HEAD_DIM
HGRN_HEADS = 8
HGRN_KEY_DIM = 128
HGRN_VAL_DIM = 128
HGRN_WIDTH = HGRN_HEADS * HGRN_KEY_DIM
HGRN_CHUNK = 64
ROPE_THETA = 10000.0
NORM_EPS = 1e-6
D_FF = -(-8 * D_MODEL // (3 * 256)) * 256
IN_COLS = 3 * ATTN_QKV_WIDTH + 4 * HGRN_WIDTH + 2 * D_MODEL

kernel_name = "hybrid_dilated_attn_hgrn2_gated_block"


def rmsnorm(x, gain):
    xf = x.astype(jnp.float32)
    y = xf * lax.rsqrt(jnp.mean(xf * xf, axis=-1, keepdims=True) + NORM_EPS)
    return (y * gain.astype(jnp.float32)).astype(x.dtype)


def rotary(t, seq_len):
    inv_freq = ROPE_THETA ** (-jnp.arange(0, HEAD_DIM, 2, dtype=jnp.float32) / HEAD_DIM)
    ang = jnp.arange(seq_len, dtype=jnp.float32)[:, None] * inv_freq[None, :]
    cos = jnp.concatenate([jnp.cos(ang), jnp.cos(ang)], axis=-1)
    sin = jnp.concatenate([jnp.sin(ang), jnp.sin(ang)], axis=-1)
    tf = t.astype(jnp.float32)
    half = HEAD_DIM // 2
    rot = jnp.concatenate([-tf[..., half:], tf[..., :half]], axis=-1)
    return (tf * cos + rot * sin).astype(t.dtype)


def banded_causal_attention(q, k, v, back):
    *lead, L, hd = q.shape
    blk = back
    n = L // blk
    qb = q.reshape(*lead, n, blk, hd).astype(jnp.float32)
    kb = k.reshape(*lead, n, blk, hd).astype(jnp.float32)
    vb = v.reshape(*lead, n, blk, hd).astype(jnp.float32)
    zero = jnp.zeros_like(kb[..., :1, :, :])
    kk = jnp.concatenate([jnp.concatenate([zero, kb[..., :-1, :, :]], axis=-3), kb], axis=-2)
    vv = jnp.concatenate([jnp.concatenate([zero, vb[..., :-1, :, :]], axis=-3), vb], axis=-2)
    s = jnp.einsum('...nqd,...nkd->...nqk', qb, kk) * (hd ** -0.5)
    qi = jnp.arange(blk)[:, None]
    kj = jnp.arange(2 * blk)[None, :]
    dist = qi + blk - kj
    band = (dist >= 0) & (dist <= back)
    in_range = (jnp.arange(n)[:, None, None] > 0) | (kj >= blk)[None]
    s = jnp.where(band[None] & in_range, s, -jnp.inf)
    m = jnp.max(s, axis=-1, keepdims=True)
    p = jnp.exp(s - m)
    den = jnp.sum(p, axis=-1, keepdims=True)
    out = jnp.einsum('...nqk,...nkd->...nqd', p, vv) / den
    lse = (m + jnp.log(den))[..., 0]
    return out.reshape(*lead, L, hd), lse.reshape(*lead, L)


def dilated_window_attention(q, k, v, window, dilation):
    B, H, S, hd = q.shape
    back = window // dilation
    span = dilation * back
    s_pad = -(-S // span) * span

    def to_residue(t):
        t = jnp.pad(t, ((0, 0), (0, 0), (0, s_pad - S), (0, 0)))
        t = t.reshape(B, H, s_pad // dilation, dilation, hd)
        return jnp.swapaxes(t, 2, 3)

    out, lse = banded_causal_attention(to_residue(q), to_residue(k), to_residue(v), back)
    out = jnp.swapaxes(out, 2, 3).reshape(B, H, s_pad, hd)[:, :, :S]
    lse = jnp.swapaxes(lse, 2, 3).reshape(B, H, s_pad)[:, :, :S]
    return out, lse


def hgrn2_chunked(q, logf, k, v):
    B, H, S, dk = q.shape
    dv = v.shape[-1]
    n = S // HGRN_CHUNK

    def chunks(t):
        return jnp.moveaxis(t.reshape(B, H, n, HGRN_CHUNK, t.shape[-1]), 2, 0)

    causal = jnp.tril(jnp.ones((HGRN_CHUNK, HGRN_CHUNK), dtype=bool))

    def step(state, inp):
        qc, gc, kc, vc = inp
        b = jnp.cumsum(gc, axis=-2)
        o_inter = jnp.einsum('bhtk,bhkv->bhtv', qc * jnp.exp(b), state)
        rel = b[:, :, :, None, :] - b[:, :, None, :, :]
        decay = jnp.exp(jnp.where(causal[:, :, None], rel, -jnp.inf))
        a = jnp.einsum('bhtk,bhtsk,bhsk->bhts', qc, decay, kc)
        o = o_inter + jnp.einsum('bhts,bhsv->bhtv', a, vc)
        b_last = b[:, :, -1:, :]
        new_state = jnp.exp(b_last[:, :, 0, :])[..., None] * state + jnp.einsum(
            'bhsk,bhsv->bhkv', kc * jnp.exp(b_last - b), vc)
        return new_state, o

    state0 = jnp.zeros((B, H, dk, dv), jnp.float32)
    _, o = lax.scan(step, state0, (chunks(q), chunks(logf), chunks(k), chunks(v)))
    return jnp.moveaxis(o, 0, 2).reshape(B, H, S, dv)


def split_columns(proj):
    sizes = [ATTN_QKV_WIDTH] * 3 + [HGRN_WIDTH] * 4 + [D_MODEL] * 2
    points = [int(p) for p in np.cumsum(sizes)[:-1]]
    return jnp.split(proj, points, axis=-1)


def setup_inputs(seed: int = 0) -> dict:
    key = jax.random.key(seed)
    ks = jax.random.split(key, 16)
    f32 = jnp.float32

    def w(k, shape, fan_in):
        return jax.random.normal(k, shape, f32) * (fan_in ** -0.5)

    def gain(k, shape):
        return 1.0 + 0.02 * jax.random.normal(k, shape, f32)

    return {
        "x": jax.random.normal(ks[0], (BATCH, SEQ, D_MODEL), f32),
        "w_in": w(ks[1], (DEPTH, D_MODEL, IN_COLS), D_MODEL),
        "w_attn_branch": w(ks[2], (DEPTH, ATTN_OUT_WIDTH, D_MODEL), ATTN_OUT_WIDTH),
        "w_hgrn_branch": w(ks[3], (DEPTH, HGRN_HEADS * HGRN_VAL_DIM, D_MODEL), HGRN_HEADS * HGRN_VAL_DIM),
        "w_mix_out": w(ks[4], (DEPTH, D_MODEL, D_MODEL), D_MODEL),
        "hgrn_lower_bounds": 0.1 * jax.random.normal(ks[5], (DEPTH + 1, HGRN_WIDTH), f32),
        "hgrn_norm_gain": gain(ks[6], (DEPTH, HGRN_HEADS * HGRN_VAL_DIM)),
        "norm_mix_pre": gain(ks[7], (DEPTH, D_MODEL)),
        "norm_mix_post": gain(ks[8], (DEPTH, D_MODEL)),
        "w_ffn_gate_up": w(ks[9], (DEPTH, D_MODEL, 2 * D_FF), D_MODEL),
        "w_ffn_down": w(ks[10], (DEPTH, D_FF, D_MODEL), D_FF),
        "norm_ffn_pre": gain(ks[11], (DEPTH, D_MODEL)),
        "norm_ffn_post": gain(ks[12], (DEPTH, D_MODEL)),
    }


def reference(x, w_in, w_attn_branch, w_hgrn_branch, w_mix_out, hgrn_lower_bounds, hgrn_norm_gain,
              norm_mix_pre, norm_mix_post, w_ffn_gate_up, w_ffn_down, norm_ffn_pre, norm_ffn_post):
    B, S, _ = x.shape
    lower_bounds = jnp.cumsum(jax.nn.softmax(hgrn_lower_bounds.astype(jnp.float32), axis=0), axis=0)
    h = x
    for layer in range(DEPTH):
        a = rmsnorm(h, norm_mix_pre[layer])
        proj = a @ w_in[layer]
        q_a, k_a, v_a, q_r, f_r, i_r, g_r, gate_a, gate_r = split_columns(proj)

        def heads(t):
            return t.reshape(B, S, N_GROUPS, ATTN_HEADS_PER_GROUP, HEAD_DIM).transpose(2, 0, 3, 1, 4)
        qh = rotary(heads(q_a), S)
        kh = rotary(heads(k_a), S)
        vh = heads(v_a)
        outs, lses = [], []
        for g, (window, dilation) in enumerate(ATTN_GROUPS):
            o_g, lse_g = dilated_window_attention(qh[g], kh[g], vh[g], window, dilation)
            outs.append(o_g)
            lses.append(lse_g)
        mix_w = jax.nn.softmax(jnp.stack(lses, axis=0), axis=0)
        attn = jnp.einsum('gbhs,gbhsd->bshd', mix_w, jnp.stack(outs, axis=0))
        attn = attn.reshape(B, S, ATTN_OUT_WIDTH).astype(x.dtype)

        def rheads(t):
            return t.reshape(B, S, HGRN_HEADS, -1).transpose(0, 2, 1, 3).astype(jnp.float32)
        lb = lower_bounds[layer].reshape(HGRN_HEADS, HGRN_KEY_DIM)[None, :, None, :]
        q_h = jax.nn.silu(rheads(q_r))
        f = lb + (1.0 - lb) * jax.nn.sigmoid(rheads(f_r))
        r_out = hgrn2_chunked(q_h, jnp.log(f), 1.0 - f, rheads(i_r))
        r_out = r_out.transpose(0, 2, 1, 3)
        r_out = r_out * lax.rsqrt(jnp.mean(r_out * r_out, axis=-1, keepdims=True) + NORM_EPS)
        r_out = r_out * hgrn_norm_gain[layer].astype(jnp.float32).reshape(HGRN_HEADS, HGRN_VAL_DIM)
        r_out = (r_out.reshape(B, S, HGRN_HEADS * HGRN_VAL_DIM) * jax.nn.silu(g_r.astype(jnp.float32))).astype(x.dtype)

        y = jax.nn.sigmoid(gate_a) * (attn @ w_attn_branch[layer]) + jax.nn.sigmoid(gate_r) * (r_out @ w_hgrn_branch[layer])
        h = h + rmsnorm(y @ w_mix_out[layer], norm_mix_post[layer])

        a = rmsnorm(h, norm_ffn_pre[layer])
        gt, up = jnp.split(a @ w_ffn_gate_up[layer], [D_FF], axis=-1)
        ff = (jax.nn.silu(gt) * up) @ w_ffn_down[layer]
        h = h + rmsnorm(ff, norm_ffn_post[layer])
    return h
```

```python
import functools

import numpy as np
import jax
import jax.numpy as jnp
from jax import lax
from jax.experimental import pallas as pl
from jax.experimental.pallas import tpu as pltpu

F32 = jnp.float32
BF16 = jnp.bfloat16

HEAD_DIM = 128
ATTN_GROUPS = ((128, 1), (512, 4), (2048, 16))
N_GROUPS = len(ATTN_GROUPS)
HEADS = 8
GROUP_WIDTH = HEADS * HEAD_DIM
ATTN_BACK = 128
ROPE_THETA = 10000.0
NORM_EPS = 1e-6
HGRN_CHUNK = 128
HGRN_LEVELS = (64, 32, 16, 8, 4, 2, 1)
LSE_LANES = 16
NEG_BIG = -1e30

VMEM_LIMIT_BYTES = 56 * 1024 * 1024


def _params(semantics):
    return pltpu.CompilerParams(dimension_semantics=semantics, vmem_limit_bytes=VMEM_LIMIT_BYTES)


def _rms_scale(v):
    return lax.rsqrt(jnp.mean(v * v, axis=-1, keepdims=True) + NORM_EPS)


def _in_proj_kernel(x_ref, gain_ref, w_ref, o_ref, a_ref, *, row_chunk):
    @pl.when(pl.program_id(1) == 0)
    def _():
        gain = gain_ref[...]

        def body(c, carry):
            rows = pl.ds(pl.multiple_of(c * row_chunk, row_chunk), row_chunk)
            xv = x_ref[rows, :]
            a_ref[rows, :] = (xv * _rms_scale(xv) * gain).astype(BF16)
            return carry

        lax.fori_loop(0, x_ref.shape[0] // row_chunk, body, 0)

    o_ref[...] = jnp.dot(a_ref[...], w_ref[...], preferred_element_type=F32).astype(o_ref.dtype)


def _in_proj(x2, gain, w, *, tm=1024, tn=1024):
    t, d = x2.shape
    n = w.shape[1]
    return pl.pallas_call(
        functools.partial(_in_proj_kernel, row_chunk=128),
        grid=(t // tm, n // tn),
        in_specs=[
            pl.BlockSpec((tm, d), lambda i, j: (i, 0)),
            pl.BlockSpec((1, d), lambda i, j: (0, 0)),
            pl.BlockSpec((d, tn), lambda i, j: (0, j)),
        ],
        out_specs=pl.BlockSpec((tm, tn), lambda i, j: (i, j)),
        out_shape=jax.ShapeDtypeStruct((t, n), BF16),
        scratch_shapes=[pltpu.VMEM((tm, d), BF16)],
        compiler_params=_params(("parallel", "arbitrary")),
        name="in_proj",
    )(x2, gain, w)


def _attn_kernel(q_ref, k_ref, v_ref, cos_ref, sin_ref, o_ref, lse_ref, kk_ref, vv_ref):
    n = pl.program_id(2)
    blk = ATTN_BACK

    @pl.when(n == 0)
    def _():
        kk_ref[0:blk, :] = jnp.zeros((blk, GROUP_WIDTH), BF16)
        vv_ref[0:blk, :] = jnp.zeros((blk, GROUP_WIDTH), BF16)

    @pl.when(n > 0)
    def _():
        kk_ref[0:blk, :] = kk_ref[blk:2 * blk, :]
        vv_ref[0:blk, :] = vv_ref[blk:2 * blk, :]

    vv_ref[blk:2 * blk, :] = v_ref[...]
    cos = cos_ref[...]
    sin = sin_ref[...]
    scale = HEAD_DIM ** -0.5

    qi = lax.broadcasted_iota(jnp.int32, (blk, 2 * blk), 0)
    kj = lax.broadcasted_iota(jnp.int32, (blk, 2 * blk), 1)
    first_key = jnp.where(n > 0, 0, blk)
    mask = (kj >= jnp.maximum(qi, first_key)) & (kj <= qi + blk)

    for h in range(HEADS):
        sl = slice(h * HEAD_DIM, (h + 1) * HEAD_DIM)
        kf = k_ref[:, sl].astype(F32)
        kk_ref[blk:2 * blk, sl] = (kf * cos + pltpu.roll(kf, HEAD_DIM // 2, 1) * sin).astype(BF16)
        qf = q_ref[:, sl].astype(F32)
        qr = ((qf * cos + pltpu.roll(qf, HEAD_DIM // 2, 1) * sin) * scale).astype(BF16)
        s = lax.dot_general(qr, kk_ref[:, sl], (((1,), (1,)), ((), ())), preferred_element_type=F32)
        s = jnp.where(mask, s, NEG_BIG)
        m = jnp.max(s, axis=-1, keepdims=True)
        p = jnp.exp(s - m)
        den = jnp.sum(p, axis=-1, keepdims=True)
        o = jnp.dot(p.astype(BF16), vv_ref[:, sl], preferred_element_type=F32)
        o_ref[:, sl] = (o / den).astype(o_ref.dtype)
        lse_ref[:, h * LSE_LANES:(h + 1) * LSE_LANES] = jnp.broadcast_to(m + jnp.log(den), (blk, LSE_LANES))


def _attn_group(proj, cos, sin_signed, g, dilation, batch, seq):
    cols = proj.shape[1]
    ncb = cols // GROUP_WIDTH
    length = seq // dilation
    nblk = length // ATTN_BACK
    proj_r = proj.reshape(batch, length, dilation * cols)
    cos_r = cos.reshape(length, dilation * HEAD_DIM)
    sin_r = sin_signed.reshape(length, dilation * HEAD_DIM)

    def qkv_spec(which):
        return pl.BlockSpec((None, ATTN_BACK, GROUP_WIDTH),
                            lambda b, r, n: (b, n, r * ncb + which * N_GROUPS + g))

    tab_spec = pl.BlockSpec((ATTN_BACK, HEAD_DIM), lambda b, r, n: (n, r))
    out, lse = pl.pallas_call(
        _attn_kernel,
        grid=(batch, dilation, nblk),
        in_specs=[qkv_spec(0), qkv_spec(1), qkv_spec(2), tab_spec, tab_spec],
        out_specs=[
            pl.BlockSpec((None, ATTN_BACK, GROUP_WIDTH), lambda b, r, n: (b, n, r)),
            pl.BlockSpec((None, ATTN_BACK, HEADS * LSE_LANES), lambda b, r, n: (b, n, r)),
        ],
        out_shape=[
            jax.ShapeDtypeStruct((batch, length, dilation * GROUP_WIDTH), BF16),
            jax.ShapeDtypeStruct((batch, length, dilation * HEADS * LSE_LANES), F32),
        ],
        scratch_shapes=[pltpu.VMEM((2 * ATTN_BACK, GROUP_WIDTH), BF16),
                        pltpu.VMEM((2 * ATTN_BACK, GROUP_WIDTH), BF16)],
        compiler_params=_params(("parallel", "parallel", "arbitrary")),
        name=f"attn_d{dilation}",
    )(proj_r, proj_r, proj_r, cos_r, sin_r)
    t = batch * seq
    return out.reshape(t, GROUP_WIDTH), lse.reshape(t, HEADS * LSE_LANES)


def _hgrn_cumsum_matrix():
    c = HGRN_CHUNK
    t = np.arange(c)
    tri = (t[None, :] <= t[:, None]).astype(np.float32)
    mats = [tri]
    for half in HGRN_LEVELS:
        ref_row = (t // (2 * half)) * (2 * half) + half - 1
        mats.append(tri[ref_row])
    return np.concatenate(mats, axis=0)


def _hgrn_kernel(qr_ref, fr_ref, ir_ref, gr_ref, hlb_ref, gain_ref, ct_ref, o_ref,
                 st_ref, q_s, k_s, cb_s):
    c = HGRN_CHUNK
    dk = HEAD_DIM

    @pl.when(pl.program_id(1) == 0)
    def _():
        st_ref[...] = jnp.zeros(st_ref.shape, F32)

    h0 = hlb_ref[0:1, :]
    h1 = hlb_ref[1:2, :]
    hm = jnp.maximum(h0, h1)
    e0 = jnp.exp(h0 - hm)
    e1 = jnp.exp(h1 - hm)
    lb = e0 / (e0 + e1)

    ti = lax.broadcasted_iota(jnp.int32, (c, c), 0)
    si = lax.broadcasted_iota(jnp.int32, (c, c), 1)
    diag_mask = ti == si
    level_masks = []
    for half in HGRN_LEVELS:
        shift = half.bit_length()
        level_masks.append((lax.shift_right_logical(ti, shift) == lax.shift_right_logical(si, shift))
                           & ((ti & half) != 0) & ((si & half) == 0))
    ones_b = jnp.ones((dk, c), BF16)
    ct = ct_ref[...]

    def chunk_body(ci, carry):
        rows = pl.ds(pl.multiple_of(ci * c, c), c)
        qx = qr_ref[rows, :].astype(F32)
        fx = fr_ref[rows, :].astype(F32)
        f = lb + (1.0 - lb) * jax.nn.sigmoid(fx)
        g = jnp.log(f)
        q_s[...] = qx * jax.nn.sigmoid(qx)
        k_s[...] = 1.0 - f
        g_hi = g.astype(BF16)
        r1 = g - g_hi.astype(F32)
        g_mid = r1.astype(BF16)
        g_lo = (r1 - g_mid.astype(F32)).astype(BF16)
        cb_s[...] = (jnp.dot(ct, g_hi, preferred_element_type=F32)
                     + jnp.dot(ct, g_mid, preferred_element_type=F32)
                     + jnp.dot(ct, g_lo, preferred_element_type=F32))

        for h in range(HEADS):
            sl = slice(h * dk, (h + 1) * dk)
            q = q_s[:, sl]
            k = k_s[:, sl]
            v = ir_ref[rows, sl]
            b = cb_s[0:c, sl]
            b_last = cb_s[c - 1:c, sl]
            st = st_ref[h]

            q_dec = (q * jnp.exp(b)).astype(BF16)
            o = lax.dot_general(q_dec, st.astype(BF16), (((1,), (1,)), ((), ())),
                                preferred_element_type=F32)

            qk = (q * k).astype(BF16)
            a = jnp.where(diag_mask, jnp.dot(qk, ones_b, preferred_element_type=F32), 0.0)
            for li in range(len(HGRN_LEVELS)):
                ref = cb_s[(li + 1) * c:(li + 2) * c, sl]
                x = b - ref
                q_l = (q * jnp.exp(jnp.minimum(x, 0.0))).astype(BF16)
                k_l = (k * jnp.exp(jnp.minimum(-x, 0.0))).astype(BF16)
                p_l = lax.dot_general(q_l, k_l, (((1,), (1,)), ((), ())), preferred_element_type=F32)
                a = jnp.where(level_masks[li], p_l, a)
            o = o + jnp.dot(a.astype(BF16), v, preferred_element_type=F32)

            k_dec = (k * jnp.exp(b_last - b)).astype(BF16)
            st_ref[h] = st * jnp.exp(b_last) + lax.dot_general(
                v, k_dec, (((0,), (0,)), ((), ())), preferred_element_type=F32)

            gx = gr_ref[rows, sl].astype(F32)
            r = o * _rms_scale(o) * gain_ref[:, sl]
            o_ref[rows, sl] = (r * (gx * jax.nn.sigmoid(gx))).astype(o_ref.dtype)
        return carry

    lax.fori_loop(0, o_ref.shape[0] // c, chunk_body, 0)


def _hgrn(proj, lower_bounds, gain, batch, seq, *, first_col_block, tc=512):
    assert lower_bounds.shape[0] == 2, "forget-gate bound rows: DEPTH + 1 with DEPTH == 1"
    t = batch * seq
    steps = seq // tc
    ct = jnp.asarray(_hgrn_cumsum_matrix(), BF16)

    def col_spec(k):
        return pl.BlockSpec((tc, GROUP_WIDTH), lambda b, s: (b * steps + s, first_col_block + k))

    const = lambda b, s: (0, 0)
    return pl.pallas_call(
        _hgrn_kernel,
        grid=(batch, steps),
        in_specs=[col_spec(0), col_spec(1), col_spec(2), col_spec(3),
                  pl.BlockSpec(lower_bounds.shape, const),
                  pl.BlockSpec(gain.shape, const),
                  pl.BlockSpec(ct.shape, const)],
        out_specs=pl.BlockSpec((tc, GROUP_WIDTH), lambda b, s: (b * steps + s, 0)),
        out_shape=jax.ShapeDtypeStruct((t, GROUP_WIDTH), BF16),
        scratch_shapes=[pltpu.VMEM((HEADS, HEAD_DIM, HEAD_DIM), F32),
                        pltpu.VMEM((HGRN_CHUNK, GROUP_WIDTH), F32),
                        pltpu.VMEM((HGRN_CHUNK, GROUP_WIDTH), F32),
                        pltpu.VMEM(((len(HGRN_LEVELS) + 1) * HGRN_CHUNK, GROUP_WIDTH), F32)],
        compiler_params=_params(("parallel", "arbitrary")),
        name="hgrn2",
    )(proj, proj, proj, proj, lower_bounds, gain, ct)


def _merge_kernel(o1_ref, o2_ref, o3_ref, l1_ref, l2_ref, l3_ref, r_ref,
                  ga0_ref, ga1_ref, gr0_ref, gr1_ref, x_ref,
                  wpa_ref, wpr_ref, wo_ref, gpost_ref, gffn_ref,
                  h_ref, a2_ref, attn_s, y_s):
    la, lb, lc = l1_ref[...], l2_ref[...], l3_ref[...]
    lm = jnp.maximum(la, jnp.maximum(lb, lc))
    ea, eb, ec = jnp.exp(la - lm), jnp.exp(lb - lm), jnp.exp(lc - lm)
    den = ea + eb + ec
    wa, wb, wc = ea / den, eb / den, ec / den
    for h in range(HEADS):
        sl = slice(h * HEAD_DIM, (h + 1) * HEAD_DIM)
        c0 = h * LSE_LANES
        mixed = (wa[:, c0:c0 + 1] * o1_ref[:, sl].astype(F32)
                 + wb[:, c0:c0 + 1] * o2_ref[:, sl].astype(F32)
                 + wc[:, c0:c0 + 1] * o3_ref[:, sl].astype(F32))
        attn_s[:, sl] = mixed.astype(BF16)

    attn = attn_s[...]
    r = r_ref[...]
    half_w = GROUP_WIDTH
    for half, (ga_ref, gr_ref) in enumerate(((ga0_ref, gr0_ref), (ga1_ref, gr1_ref))):
        cs = slice(half * half_w, (half + 1) * half_w)
        ya = jnp.dot(attn, wpa_ref[:, cs], preferred_element_type=F32)
        yr = jnp.dot(r, wpr_ref[:, cs], preferred_element_type=F32)
        y = (jax.nn.sigmoid(ga_ref[...].astype(F32)) * ya
             + jax.nn.sigmoid(gr_ref[...].astype(F32)) * yr)
        y_s[:, cs] = y.astype(BF16)

    z = jnp.dot(y_s[...], wo_ref[...], preferred_element_type=F32)
    h1 = x_ref[...] + z * _rms_scale(z) * gpost_ref[...]
    h_ref[...] = h1
    a2_ref[...] = (h1 * _rms_scale(h1) * gffn_ref[...]).astype(BF16)


def _merge(outs, lses, r_out, proj, x2, wpa, wpr, wo, gpost, gffn, *, gate_col_block, tm=256):
    t, d = x2.shape
    row = lambda i: (i, 0)
    const = lambda i: (0, 0)

    def resident(shape):
        return pl.BlockSpec(shape, const, pipeline_mode=pl.Buffered(1))

    def gate_spec(k):
        return pl.BlockSpec((tm, GROUP_WIDTH), lambda i: (i, gate_col_block + k))

    lse_w = HEADS * LSE_LANES
    return pl.pallas_call(
        _merge_kernel,
        grid=(t // tm,),
        in_specs=[pl.BlockSpec((tm, GROUP_WIDTH), row)] * 3
        + [pl.BlockSpec((tm, lse_w), row)] * 3
        + [pl.BlockSpec((tm, GROUP_WIDTH), row)]
        + [gate_spec(0), gate_spec(1), gate_spec(2), gate_spec(3)]
        + [pl.BlockSpec((tm, d), row),
           resident(wpa.shape), resident(wpr.shape), resident(wo.shape),
           resident(gpost.shape), resident(gffn.shape)],
        out_specs=[pl.BlockSpec((tm, d), row), pl.BlockSpec((tm, d), row)],
        out_shape=[jax.ShapeDtypeStruct((t, d), F32), jax.ShapeDtypeStruct((t, d), BF16)],
        scratch_shapes=[pltpu.VMEM((tm, GROUP_WIDTH), BF16), pltpu.VMEM((tm, d), BF16)],
        compiler_params=_params(("parallel",)),
        name="merge_out_proj",
    )(*outs, *lses, r_out, proj, proj, proj, proj, x2, wpa, wpr, wo, gpost, gffn)


def _ffn_kernel(a_ref, wg_ref, wu_ref, wd_ref, h_ref, gain_ref, o_ref, acc_ref):
    j = pl.program_id(1)
    a = a_ref[...]
    gt = jnp.dot(a, wg_ref[...], preferred_element_type=F32)
    up = jnp.dot(a, wu_ref[...], preferred_element_type=F32)
    hidden = (gt * jax.nn.sigmoid(gt) * up).astype(BF16)
    part = jnp.dot(hidden, wd_ref[...], preferred_element_type=F32)

    @pl.when(j == 0)
    def _():
        acc_ref[...] = part

    @pl.when(j > 0)
    def _():
        acc_ref[...] += part

    @pl.when(j == pl.num_programs(1) - 1)
    def _():
        ff = acc_ref[...]
        o_ref[...] = h_ref[...] + ff * _rms_scale(ff) * gain_ref[...]


def _ffn(a2, w_gate_up, w_down, h1, gain, *, tm=512, tf=512):
    t, d = a2.shape
    d_ff = w_down.shape[0]
    nf = d_ff // tf
    return pl.pallas_call(
        _ffn_kernel,
        grid=(t // tm, nf),
        in_specs=[
            pl.BlockSpec((tm, d), lambda i, j: (i, 0)),
            pl.BlockSpec((d, tf), lambda i, j: (0, j)),
            pl.BlockSpec((d, tf), lambda i, j: (0, nf + j)),
            pl.BlockSpec((tf, d), lambda i, j: (j, 0)),
            pl.BlockSpec((tm, d), lambda i, j: (i, 0)),
            pl.BlockSpec((1, d), lambda i, j: (0, 0)),
        ],
        out_specs=pl.BlockSpec((tm, d), lambda i, j: (i, 0)),
        out_shape=jax.ShapeDtypeStruct((t, d), F32),
        scratch_shapes=[pltpu.VMEM((tm, d), F32)],
        compiler_params=_params(("parallel", "arbitrary")),
        name="swiglu_ffn",
    )(a2, w_gate_up, w_gate_up, w_down, h1, gain)


def _rotary_tables(seq):
    inv_freq = ROPE_THETA ** (-jnp.arange(0, HEAD_DIM, 2, dtype=F32) / HEAD_DIM)
    ang = jnp.arange(seq, dtype=F32)[:, None] * inv_freq[None, :]
    cos = jnp.concatenate([jnp.cos(ang), jnp.cos(ang)], axis=-1)
    sin_signed = jnp.concatenate([-jnp.sin(ang), jnp.sin(ang)], axis=-1)
    return cos, sin_signed


def kernel(x, w_in, w_attn_branch, w_hgrn_branch, w_mix_out, hgrn_lower_bounds, hgrn_norm_gain,
           norm_mix_pre, norm_mix_post, w_ffn_gate_up, w_ffn_down, norm_ffn_pre, norm_ffn_post):
    batch, seq, d = x.shape
    assert w_in.shape[0] == 1, "single-layer block"
    x2 = x.reshape(batch * seq, d)
    row = lambda v: v[0].reshape(1, -1).astype(F32)

    proj = _in_proj(x2, row(norm_mix_pre), w_in[0].astype(BF16))

    cos, sin_signed = _rotary_tables(seq)
    outs, lses = [], []
    for g, (window, dilation) in enumerate(ATTN_GROUPS):
        assert window // dilation == ATTN_BACK and seq % window == 0
        o_g, lse_g = _attn_group(proj, cos, sin_signed, g, dilation, batch, seq)
        outs.append(o_g)
        lses.append(lse_g)

    qkv_blocks = 3 * N_GROUPS
    r_out = _hgrn(proj, hgrn_lower_bounds.astype(F32), row(hgrn_norm_gain), batch, seq,
                  first_col_block=qkv_blocks)

    h1, a2 = _merge(outs, lses, r_out, proj, x2,
                    w_attn_branch[0].astype(BF16), w_hgrn_branch[0].astype(BF16),
                    w_mix_out[0].astype(BF16), row(norm_mix_post), row(norm_ffn_pre),
                    gate_col_block=qkv_blocks + 4)

    out = _ffn(a2, w_ffn_gate_up[0].astype(BF16), w_ffn_down[0].astype(BF16), h1, row(norm_ffn_post))
    return out.reshape(batch, seq, d)
```

```python
import functools

import numpy as np
import jax
import jax.numpy as jnp
from jax import lax
from jax.experimental import pallas as pl
from jax.experimental.pallas import tpu as pltpu

F32 = jnp.float32
BF16 = jnp.bfloat16

HEAD_DIM = 128
ATTN_GROUPS = ((128, 1), (512, 4), (2048, 16))
N_GROUPS = len(ATTN_GROUPS)
HEADS = 8
GROUP_WIDTH = HEADS * HEAD_DIM
ATTN_BACK = 128
RES = 16
ROPE_THETA = 10000.0
NORM_EPS = 1e-6
HGRN_CHUNK = 128
HGRN_LEVELS = (64, 32, 16, 8, 4, 2, 1)
LSE_LANES = 16
NEG_BIG = -1e30
F32_SUBLANES = 8
LANES = 128

VMEM_LIMIT_BYTES = 56 * 1024 * 1024


def _params(semantics):
    return pltpu.CompilerParams(dimension_semantics=semantics, vmem_limit_bytes=VMEM_LIMIT_BYTES)


def _rms_scale(v):
    return lax.rsqrt(jnp.mean(v * v, axis=-1, keepdims=True) + NORM_EPS)


def _silu(v):
    return v * jax.nn.sigmoid(v)


def _block_position(i, rows_per_residue, block_rows):
    shift = rows_per_residue.bit_length() - 1
    n_res_shift = (block_rows // rows_per_residue).bit_length() - 1
    return ((i & (rows_per_residue - 1)) << n_res_shift) | lax.shift_right_logical(i, shift)


def _in_proj_kernel(x_ref, gain_ref, w_ref, o_ref, a_ref, tmp_ref):
    tm = x_ref.shape[0]
    per_res = tm // RES

    @pl.when(pl.program_id(1) == 0)
    def _():
        scale = _rms_scale(x_ref[...])
        for c in range(x_ref.shape[1] // LANES):
            cs = slice(c * LANES, (c + 1) * LANES)
            tmp_ref[...] = x_ref[:, cs] * scale * gain_ref[:, cs]
            for r in range(RES):
                a_ref[r * per_res:(r + 1) * per_res, cs] = (
                    tmp_ref[pl.ds(r, per_res, stride=RES), :].astype(BF16))

    res = jnp.dot(a_ref[...], w_ref[...], preferred_element_type=F32)
    o_ref[...] = res.reshape(RES, per_res, res.shape[-1]).astype(o_ref.dtype)


def _in_proj(x2, gain, w, batch, seq, *, tm=1024, tn=1024):
    t, d = x2.shape
    n = w.shape[1]
    tiles_per_seq = seq // tm
    return pl.pallas_call(
        _in_proj_kernel,
        grid=(t // tm, n // tn),
        in_specs=[
            pl.BlockSpec((tm, d), lambda i, j: (i, 0)),
            pl.BlockSpec((1, d), lambda i, j: (0, 0)),
            pl.BlockSpec((d, tn), lambda i, j: (0, j)),
        ],
        out_specs=pl.BlockSpec((None, RES, tm // RES, tn),
                               lambda i, j: (i // tiles_per_seq, 0, i % tiles_per_seq, j)),
        out_shape=jax.ShapeDtypeStruct((batch, RES, seq // RES, n), BF16),
        scratch_shapes=[pltpu.VMEM((tm, d), BF16), pltpu.VMEM((tm, LANES), F32)],
        compiler_params=_params(("parallel", "arbitrary")),
        name="in_proj",
    )(x2, gain, w)


def _sub_blocks(ref, sl, nsub):
    v = ref[..., sl].astype(F32)
    if v.ndim == 2:
        return [v]
    a, u, w = v.shape
    step = u // nsub
    return [v[:, s * step:(s + 1) * step, :].reshape(a * step, w) for s in range(nsub)]


def _store_sub_blocks(ref, sl, blocks):
    if len(ref.shape) == 2:
        ref[:, sl] = blocks[0].astype(ref.dtype)
        return
    a = ref.shape[0]
    parts = [b.reshape(a, b.shape[0] // a, b.shape[1]) for b in blocks]
    full = parts[0] if len(parts) == 1 else jnp.concatenate(parts, axis=1)
    ref[:, :, sl] = full.astype(ref.dtype)


def _attn_kernel(q_ref, k_ref, v_ref, cos_ref, sin_ref, o_ref, lse_ref, kk_ref, vv_ref, *, nsub, rows_per_residue):
    n = pl.program_id(2)
    blk = ATTN_BACK

    @pl.when(n == 0)
    def _():
        kk_ref[0:blk, :] = jnp.zeros((blk, GROUP_WIDTH), BF16)
        vv_ref[0:blk, :] = jnp.zeros((blk, GROUP_WIDTH), BF16)

    @pl.when(n > 0)
    def _():
        kk_ref[0:blk, :] = kk_ref[nsub * blk:(nsub + 1) * blk, :]
        vv_ref[0:blk, :] = vv_ref[nsub * blk:(nsub + 1) * blk, :]

    full = slice(None)
    cos = _sub_blocks(cos_ref, full, nsub)
    sin = _sub_blocks(sin_ref, full, nsub)
    scale = HEAD_DIM ** -0.5

    qi = lax.broadcasted_iota(jnp.int32, (blk, 2 * blk), 0)
    kj = lax.broadcasted_iota(jnp.int32, (blk, 2 * blk), 1)
    q_pos = _block_position(qi, rows_per_residue, blk)
    k_pos = _block_position(kj & (blk - 1), rows_per_residue, blk) - jnp.where(kj < blk, blk, 0)
    dist = q_pos - k_pos
    band = (dist >= 0) & (dist <= ATTN_BACK)
    first_mask = band & (kj >= jnp.where(n > 0, 0, blk))
    lane_head = lax.shift_right_logical(lax.broadcasted_iota(jnp.int32, (blk, HEADS * LSE_LANES), 1),
                                        LSE_LANES.bit_length() - 1)
    lse_tiles = [jnp.zeros((blk, HEADS * LSE_LANES), F32) for _ in range(nsub)]

    for h in range(HEADS):
        sl = slice(h * HEAD_DIM, (h + 1) * HEAD_DIM)
        for s, (kf, vf) in enumerate(zip(_sub_blocks(k_ref, sl, nsub), _sub_blocks(v_ref, sl, nsub))):
            rows = slice((s + 1) * blk, (s + 2) * blk)
            kk_ref[rows, sl] = (kf * cos[s] + pltpu.roll(kf, HEAD_DIM // 2, 1) * sin[s]).astype(BF16)
            vv_ref[rows, sl] = vf.astype(BF16)
        outs = []
        for s, qf in enumerate(_sub_blocks(q_ref, sl, nsub)):
            keys = slice(s * blk, (s + 2) * blk)
            qr = ((qf * cos[s] + pltpu.roll(qf, HEAD_DIM // 2, 1) * sin[s]) * scale).astype(BF16)
            sc = lax.dot_general(qr, kk_ref[keys, sl], (((1,), (1,)), ((), ())), preferred_element_type=F32)
            sc = jnp.where(first_mask if s == 0 else band, sc, NEG_BIG)
            m = jnp.max(sc, axis=-1, keepdims=True)
            p = jnp.exp(sc - m)
            den = jnp.sum(p, axis=-1, keepdims=True)
            o = jnp.dot(p.astype(BF16), vv_ref[keys, sl], preferred_element_type=F32)
            outs.append(o / den)
            lse_tiles[s] = jnp.where(lane_head == h, m + jnp.log(den), lse_tiles[s])
        _store_sub_blocks(o_ref, sl, outs)
    _store_sub_blocks(lse_ref, slice(None), lse_tiles)


def _attn_group(proj, cos_r, sin_r, g, dilation, batch, seq):
    cols = proj.shape[-1]
    per = seq // RES
    a = RES // dilation
    rows = ATTN_BACK // a
    nsub = 2 if rows < 2 * F32_SUBLANES else 1
    rows_blk = rows * nsub
    nblk = per // rows_blk
    view = lambda arr: arr.reshape(arr.shape[:-3] + (a, dilation) + arr.shape[-2:])

    def spec(width, col_block, lead):
        shape = lead + (a, None, rows_blk, width)
        if lead:
            return pl.BlockSpec(shape, lambda b, r, n: (b, 0, r, n, col_block))
        return pl.BlockSpec(shape, lambda b, r, n: (0, r, n, col_block))

    qkv = [spec(GROUP_WIDTH, which * N_GROUPS + g, (None,)) for which in range(3)]
    tab = spec(HEAD_DIM, 0, ())
    lse_w = HEADS * LSE_LANES
    proj_v = view(proj)
    out, lse = pl.pallas_call(
        functools.partial(_attn_kernel, nsub=nsub, rows_per_residue=rows),
        grid=(batch, dilation, nblk),
        in_specs=qkv + [tab, tab],
        out_specs=[spec(GROUP_WIDTH, 0, (None,)), spec(lse_w, 0, (None,))],
        out_shape=[jax.ShapeDtypeStruct((batch, a, dilation, per, GROUP_WIDTH), BF16),
                   jax.ShapeDtypeStruct((batch, a, dilation, per, lse_w), F32)],
        scratch_shapes=[pltpu.VMEM(((nsub + 1) * ATTN_BACK, GROUP_WIDTH), BF16),
                        pltpu.VMEM(((nsub + 1) * ATTN_BACK, GROUP_WIDTH), BF16)],
        compiler_params=_params(("parallel", "parallel", "arbitrary")),
        name=f"attn_d{dilation}",
    )(proj_v, proj_v, proj_v, view(cos_r), view(sin_r))
    return out.reshape(batch, RES, per, GROUP_WIDTH), lse.reshape(batch, RES, per, lse_w)


def _hgrn_cumsum_matrix():
    c = HGRN_CHUNK
    i = np.arange(c)
    per = c // RES
    pos = (i % per) * RES + i // per
    mats = [(pos[None, :] <= pos[:, None])]
    for half in HGRN_LEVELS:
        ref_pos = (pos // (2 * half)) * (2 * half) + half - 1
        mats.append(pos[None, :] <= ref_pos[:, None])
    return np.concatenate(mats, axis=0).astype(np.float32)


def _hgrn_kernel(qr_ref, fr_ref, ir_ref, gr_ref, hlb_ref, gain_ref, ct_ref, o_ref,
                 st_ref, q_s, k_s, g_s, v_s, gate_s, out_s, cb_s):
    c = HGRN_CHUNK
    dk = HEAD_DIM
    per = c // RES

    @pl.when(pl.program_id(1) == 0)
    def _():
        st_ref[...] = jnp.zeros(st_ref.shape, F32)

    h0 = hlb_ref[0:1, :]
    h1 = hlb_ref[1:2, :]
    hm = jnp.maximum(h0, h1)
    e0 = jnp.exp(h0 - hm)
    e1 = jnp.exp(h1 - hm)
    lb = e0 / (e0 + e1)

    f = lb + (1.0 - lb) * jax.nn.sigmoid(fr_ref[...].astype(F32))
    g_s[...] = jnp.log(f)
    k_s[...] = 1.0 - f
    q_s[...] = _silu(qr_ref[...].astype(F32))
    v_s[...] = ir_ref[...].astype(F32)
    gate_s[...] = _silu(gr_ref[...].astype(F32))

    ti = _block_position(lax.broadcasted_iota(jnp.int32, (c, c), 0), per, c)
    si = _block_position(lax.broadcasted_iota(jnp.int32, (c, c), 1), per, c)
    diag_mask = ti == si
    level_masks = []
    for half in HGRN_LEVELS:
        shift = half.bit_length()
        level_masks.append((lax.shift_right_logical(ti, shift) == lax.shift_right_logical(si, shift))
                           & ((ti & half) != 0) & ((si & half) == 0))
    ones_b = jnp.ones((dk, c), BF16)
    ct = ct_ref[...]

    def chunk_body(ci, carry):
        rows = pl.ds(pl.multiple_of(ci * per, per), per)
        take = lambda ref, sl: ref[:, rows, sl].reshape(c, -1)
        g = take(g_s, slice(None))
        g_hi = g.astype(BF16)
        r1 = g - g_hi.astype(F32)
        g_mid = r1.astype(BF16)
        g_lo = (r1 - g_mid.astype(F32)).astype(BF16)
        cb_s[...] = (jnp.dot(ct, g_hi, preferred_element_type=F32)
                     + jnp.dot(ct, g_mid, preferred_element_type=F32)
                     + jnp.dot(ct, g_lo, preferred_element_type=F32))

        for h in range(HEADS):
            sl = slice(h * dk, (h + 1) * dk)
            q = take(q_s, sl)
            k = take(k_s, sl)
            v = take(v_s, sl).astype(BF16)
            b = cb_s[0:c, sl]
            b_last = cb_s[c - 1:c, sl]
            st = st_ref[h]

            q_dec = (q * jnp.exp(b)).astype(BF16)
            o = lax.dot_general(q_dec, st.astype(BF16), (((1,), (1,)), ((), ())),
                                preferred_element_type=F32)

            qk = (q * k).astype(BF16)
            a = jnp.where(diag_mask, jnp.dot(qk, ones_b, preferred_element_type=F32), 0.0)
            for li in range(len(HGRN_LEVELS)):
                ref = cb_s[(li + 1) * c:(li + 2) * c, sl]
                x = b - ref
                q_l = (q * jnp.exp(jnp.minimum(x, 0.0))).astype(BF16)
                k_l = (k * jnp.exp(jnp.minimum(-x, 0.0))).astype(BF16)
                p_l = lax.dot_general(q_l, k_l, (((1,), (1,)), ((), ())), preferred_element_type=F32)
                a = jnp.where(level_masks[li], p_l, a)
            o = o + jnp.dot(a.astype(BF16), v, preferred_element_type=F32)

            k_dec = (k * jnp.exp(b_last - b)).astype(BF16)
            st_ref[h] = st * jnp.exp(b_last) + lax.dot_general(
                v, k_dec, (((0,), (0,)), ((), ())), preferred_element_type=F32)

            r = o * _rms_scale(o) * gain_ref[:, sl] * take(gate_s, sl)
            out_s[:, rows, sl] = r.reshape(RES, per, dk)
        return carry

    lax.fori_loop(0, o_ref.shape[1] // per, chunk_body, 0)
    o_ref[...] = out_s[...].astype(o_ref.dtype)


def _hgrn(proj, lower_bounds, gain, *, first_col_block, tc=512):
    assert lower_bounds.shape[0] == 2, "forget-gate bound rows: DEPTH + 1 with DEPTH == 1"
    batch, _, per, _ = proj.shape
    rows = tc // RES
    ct = jnp.asarray(_hgrn_cumsum_matrix(), BF16)

    def col_spec(k):
        return pl.BlockSpec((None, RES, rows, GROUP_WIDTH), lambda b, s: (b, 0, s, first_col_block + k))

    const = lambda b, s: (0, 0)
    tile = pltpu.VMEM((RES, rows, GROUP_WIDTH), F32)
    return pl.pallas_call(
        _hgrn_kernel,
        grid=(batch, per // rows),
        in_specs=[col_spec(0), col_spec(1), col_spec(2), col_spec(3),
                  pl.BlockSpec(lower_bounds.shape, const),
                  pl.BlockSpec(gain.shape, const),
                  pl.BlockSpec(ct.shape, const)],
        out_specs=pl.BlockSpec((None, RES, rows, GROUP_WIDTH), lambda b, s: (b, 0, s, 0)),
        out_shape=jax.ShapeDtypeStruct((batch, RES, per, GROUP_WIDTH), BF16),
        scratch_shapes=[pltpu.VMEM((HEADS, HEAD_DIM, HEAD_DIM), F32),
                        tile, tile, tile, tile, tile, tile,
                        pltpu.VMEM(((len(HGRN_LEVELS) + 1) * HGRN_CHUNK, GROUP_WIDTH), F32)],
        compiler_params=_params(("parallel", "arbitrary")),
        name="hgrn2",
    )(proj, proj, proj, proj, lower_bounds, gain, ct)


def _merge_kernel(o1_ref, o2_ref, o3_ref, l1_ref, l2_ref, l3_ref, r_ref,
                  ga0_ref, ga1_ref, gr0_ref, gr1_ref, x_ref,
                  wpa_ref, wpr_ref, wo_ref, gpost_ref, gffn_ref,
                  h_ref, a2_ref, attn_s, y_s, xp_s, tmp_ref):
    tm = x_ref.shape[0]
    per = tm // RES
    flat = lambda ref: ref[...].reshape(tm, ref.shape[-1])

    la, lb, lc = flat(l1_ref), flat(l2_ref), flat(l3_ref)
    lm = jnp.maximum(la, jnp.maximum(lb, lc))
    ea, eb, ec = jnp.exp(la - lm), jnp.exp(lb - lm), jnp.exp(lc - lm)
    den = ea + eb + ec
    wa, wb, wc = ea / den, eb / den, ec / den
    o1, o2, o3 = flat(o1_ref), flat(o2_ref), flat(o3_ref)
    for h in range(HEADS):
        sl = slice(h * HEAD_DIM, (h + 1) * HEAD_DIM)
        c0 = h * LSE_LANES
        mixed = (wa[:, c0:c0 + 1] * o1[:, sl].astype(F32)
                 + wb[:, c0:c0 + 1] * o2[:, sl].astype(F32)
                 + wc[:, c0:c0 + 1] * o3[:, sl].astype(F32))
        attn_s[:, sl] = mixed.astype(BF16)

    attn = attn_s[...]
    r = flat(r_ref)
    half_w = GROUP_WIDTH
    for half, (ga_ref, gr_ref) in enumerate(((ga0_ref, gr0_ref), (ga1_ref, gr1_ref))):
        cs = slice(half * half_w, (half + 1) * half_w)
        ya = jnp.dot(attn, wpa_ref[:, cs], preferred_element_type=F32)
        yr = jnp.dot(r, wpr_ref[:, cs], preferred_element_type=F32)
        y = (jax.nn.sigmoid(flat(ga_ref).astype(F32)) * ya
             + jax.nn.sigmoid(flat(gr_ref).astype(F32)) * yr)
        y_s[:, cs] = y.astype(BF16)

    z = jnp.dot(y_s[...], wo_ref[...], preferred_element_type=F32)
    for c in range(x_ref.shape[1] // LANES):
        cs = slice(c * LANES, (c + 1) * LANES)
        tmp_ref[...] = x_ref[:, cs]
        for r_ in range(RES):
            xp_s[r_ * per:(r_ + 1) * per, cs] = tmp_ref[pl.ds(r_, per, stride=RES), :]
    h1 = xp_s[...] + z * _rms_scale(z) * gpost_ref[...]
    h_ref[...] = h1.reshape(RES, per, h1.shape[-1])
    a2_ref[...] = (h1 * _rms_scale(h1) * gffn_ref[...]).astype(BF16).reshape(RES, per, h1.shape[-1])


def _merge(outs, lses, r_out, proj, x2, wpa, wpr, wo, gpost, gffn, *, gate_col_block, tm=256):
    t, d = x2.shape
    batch, _, per, _ = proj.shape
    rows = tm // RES
    tiles_per_seq = per // rows
    const = lambda i: (0, 0)

    def lay(width, col_block=0):
        return pl.BlockSpec((None, RES, rows, width),
                            lambda i: (i // tiles_per_seq, 0, i % tiles_per_seq, col_block))

    def resident(shape):
        return pl.BlockSpec(shape, const, pipeline_mode=pl.Buffered(1))

    lse_w = HEADS * LSE_LANES
    return pl.pallas_call(
        _merge_kernel,
        grid=(t // tm,),
        in_specs=[lay(GROUP_WIDTH)] * 3 + [lay(lse_w)] * 3 + [lay(GROUP_WIDTH)]
        + [lay(GROUP_WIDTH, gate_col_block + k) for k in range(4)]
        + [pl.BlockSpec((tm, d), lambda i: (i, 0)),
           resident(wpa.shape), resident(wpr.shape), resident(wo.shape),
           resident(gpost.shape), resident(gffn.shape)],
        out_specs=[lay(d), lay(d)],
        out_shape=[jax.ShapeDtypeStruct((batch, RES, per, d), F32),
                   jax.ShapeDtypeStruct((batch, RES, per, d), BF16)],
        scratch_shapes=[pltpu.VMEM((tm, GROUP_WIDTH), BF16), pltpu.VMEM((tm, d), BF16),
                        pltpu.VMEM((tm, d), F32), pltpu.VMEM((tm, LANES), F32)],
        compiler_params=_params(("parallel",)),
        name="merge_out_proj",
    )(*outs, *lses, r_out, proj, proj, proj, proj, x2, wpa, wpr, wo, gpost, gffn)


def _ffn_kernel(a_ref, wg_ref, wu_ref, wd_ref, h_ref, gain_ref, o_ref, acc_ref, tmp_ref):
    j = pl.program_id(1)
    tm = o_ref.shape[0]
    per = tm // RES
    a = a_ref[...].reshape(tm, a_ref.shape[-1])
    gt = jnp.dot(a, wg_ref[...], preferred_element_type=F32)
    up = jnp.dot(a, wu_ref[...], preferred_element_type=F32)
    hidden = (_silu(gt) * up).astype(BF16)
    part = jnp.dot(hidden, wd_ref[...], preferred_element_type=F32)

    @pl.when(j == 0)
    def _():
        acc_ref[...] = part

    @pl.when(j > 0)
    def _():
        acc_ref[...] += part

    @pl.when(j == pl.num_programs(1) - 1)
    def _():
        ff = acc_ref[...]
        acc_ref[...] = h_ref[...].reshape(tm, ff.shape[-1]) + ff * _rms_scale(ff) * gain_ref[...]
        for c in range(o_ref.shape[1] // LANES):
            cs = slice(c * LANES, (c + 1) * LANES)
            for r in range(RES):
                tmp_ref[pl.ds(r, per, stride=RES), :] = acc_ref[r * per:(r + 1) * per, cs]
            o_ref[:, cs] = tmp_ref[...]


def _ffn(a2, w_gate_up, w_down, h1, gain, *, tm=512, tf=512):
    batch, _, per, d = a2.shape
    t = batch * per * RES
    rows = tm // RES
    tiles_per_seq = per // rows
    d_ff = w_down.shape[0]
    nf = d_ff // tf
    lay = pl.BlockSpec((None, RES, rows, d), lambda i, j: (i // tiles_per_seq, 0, i % tiles_per_seq, 0))
    return pl.pallas_call(
        _ffn_kernel,
        grid=(t // tm, nf),
        in_specs=[
            lay,
            pl.BlockSpec((d, tf), lambda i, j: (0, j)),
            pl.BlockSpec((d, tf), lambda i, j: (0, nf + j)),
            pl.BlockSpec((tf, d), lambda i, j: (j, 0)),
            lay,
            pl.BlockSpec((1, d), lambda i, j: (0, 0)),
        ],
        out_specs=pl.BlockSpec((tm, d), lambda i, j: (i, 0)),
        out_shape=jax.ShapeDtypeStruct((t, d), F32),
        scratch_shapes=[pltpu.VMEM((tm, d), F32), pltpu.VMEM((tm, LANES), F32)],
        compiler_params=_params(("parallel", "arbitrary")),
        name="swiglu_ffn",
    )(a2, w_gate_up, w_gate_up, w_down, h1, gain)


def _rotary_tables(seq):
    inv_freq = ROPE_THETA ** (-jnp.arange(0, HEAD_DIM, 2, dtype=F32) / HEAD_DIM)
    ang = jnp.arange(seq, dtype=F32)[:, None] * inv_freq[None, :]
    cos = jnp.concatenate([jnp.cos(ang), jnp.cos(ang)], axis=-1)
    sin_signed = jnp.concatenate([-jnp.sin(ang), jnp.sin(ang)], axis=-1)
    lay = lambda tab: tab.reshape(seq // RES, RES, HEAD_DIM).transpose(1, 0, 2)
    return lay(cos), lay(sin_signed)


def kernel(x, w_in, w_attn_branch, w_hgrn_branch, w_mix_out, hgrn_lower_bounds, hgrn_norm_gain,
           norm_mix_pre, norm_mix_post, w_ffn_gate_up, w_ffn_down, norm_ffn_pre, norm_ffn_post):
    batch, seq, d = x.shape
    assert w_in.shape[0] == 1, "single-layer block"
    x2 = x.reshape(batch * seq, d)
    row = lambda v: v[0].reshape(1, -1).astype(F32)

    proj = _in_proj(x2, row(norm_mix_pre), w_in[0].astype(BF16), batch, seq)

    cos_r, sin_r = _rotary_tables(seq)
    outs, lses = [], []
    for g, (window, dilation) in enumerate(ATTN_GROUPS):
        assert window // dilation == ATTN_BACK and seq % window == 0 and RES % dilation == 0
        o_g, lse_g = _attn_group(proj, cos_r, sin_r, g, dilation, batch, seq)
        outs.append(o_g)
        lses.append(lse_g)

    qkv_blocks = 3 * N_GROUPS
    r_out = _hgrn(proj, hgrn_lower_bounds.astype(F32), row(hgrn_norm_gain), first_col_block=qkv_blocks)

    h1, a2 = _merge(outs, lses, r_out, proj, x2,
                    w_attn_branch[0].astype(BF16), w_hgrn_branch[0].astype(BF16),
                    w_mix_out[0].astype(BF16), row(norm_mix_post), row(norm_ffn_pre),
                    gate_col_block=qkv_blocks + 4)

    out = _ffn(a2, w_ffn_gate_up[0].astype(BF16), w_ffn_down[0].astype(BF16), h1, row(norm_ffn_post))
    return out.reshape(batch, seq, d)
```

```python
import functools
import math

import numpy as np
import jax
import jax.numpy as jnp
from jax import lax
from jax.experimental import pallas as pl
from jax.experimental.pallas import tpu as pltpu

F32 = jnp.float32
BF16 = jnp.bfloat16

HEAD_DIM = 128
ATTN_GROUPS = ((128, 1), (512, 4), (2048, 16))
N_GROUPS = len(ATTN_GROUPS)
HEADS = 8
GROUP_WIDTH = HEADS * HEAD_DIM
ATTN_BACK = 128
ATTN_SUB_BLOCKS = 2
RES = 16
ROPE_THETA = 10000.0
NORM_EPS = 1e-6
HGRN_CHUNK = 128
HGRN_LEVELS = (64, 32, 16, 8, 4, 2, 1)
LSE_LANES = 16
NEG_BIG = -1e30
PERMUTE_GROUP = 256
LOG2E = math.log2(math.e)

VMEM_LIMIT_BYTES = 56 * 1024 * 1024


def _params(semantics):
    return pltpu.CompilerParams(dimension_semantics=semantics, vmem_limit_bytes=VMEM_LIMIT_BYTES)


def _rms_scale(v):
    return lax.rsqrt(jnp.mean(v * v, axis=-1, keepdims=True) + NORM_EPS)


def _sigmoid(v):
    return 0.5 * jnp.tanh(0.5 * v) + 0.5


def _silu(v):
    return v * _sigmoid(v)


def _block_position(i, rows_per_residue, block_rows):
    shift = rows_per_residue.bit_length() - 1
    n_res_shift = (block_rows // rows_per_residue).bit_length() - 1
    return ((i & (rows_per_residue - 1)) << n_res_shift) | lax.shift_right_logical(i, shift)


def _in_proj_kernel(x_ref, gain_ref, w_ref, o_ref, a_ref):
    tm = x_ref.shape[0]
    per_res = tm // RES
    grp = PERMUTE_GROUP
    per_grp = grp // RES

    @pl.when(pl.program_id(1) == 0)
    def _():
        row_token = _block_position(lax.broadcasted_iota(jnp.int32, (grp, grp), 0), per_grp, grp)
        permute = jnp.where(row_token == lax.broadcasted_iota(jnp.int32, (grp, grp), 1), 1.0, 0.0).astype(BF16)
        gain = gain_ref[...]
        for gi in range(tm // grp):
            xg = x_ref[gi * grp:(gi + 1) * grp, :]
            ag = (xg * _rms_scale(xg) * gain).astype(BF16)
            pg = jnp.dot(permute, ag, preferred_element_type=F32).astype(BF16)
            for r in range(RES):
                dst = r * per_res + gi * per_grp
                a_ref[dst:dst + per_grp, :] = pg[r * per_grp:(r + 1) * per_grp, :]

    res = jnp.dot(a_ref[...], w_ref[...], preferred_element_type=F32)
    o_ref[...] = res.reshape(RES, per_res, res.shape[-1]).astype(o_ref.dtype)


def _in_proj(x2, gain, w, batch, seq, *, tm=1024, tn=1024):
    t, d = x2.shape
    n = w.shape[1]
    tiles_per_seq = seq // tm
    return pl.pallas_call(
        _in_proj_kernel,
        grid=(t // tm, n // tn),
        in_specs=[
            pl.BlockSpec((tm, d), lambda i, j: (i, 0)),
            pl.BlockSpec((1, d), lambda i, j: (0, 0)),
            pl.BlockSpec((d, tn), lambda i, j: (0, j)),
        ],
        out_specs=pl.BlockSpec((None, RES, tm // RES, tn),
                               lambda i, j: (i // tiles_per_seq, 0, i % tiles_per_seq, j)),
        out_shape=jax.ShapeDtypeStruct((batch, RES, seq // RES, n), BF16),
        scratch_shapes=[pltpu.VMEM((tm, d), BF16)],
        compiler_params=_params(("parallel", "arbitrary")),
        name="in_proj",
    )(x2, gain, w)


def _sub_blocks(ref, sl, nsub):
    v = ref[:, :, sl].astype(F32)
    a, u, w = v.shape
    step = u // nsub
    return [v[:, s * step:(s + 1) * step, :].reshape(a * step, w) for s in range(nsub)]


def _store_sub_blocks(ref, sl, blocks):
    a = ref.shape[0]
    parts = [b.reshape(a, b.shape[0] // a, b.shape[1]) for b in blocks]
    ref[:, :, sl] = jnp.concatenate(parts, axis=1).astype(ref.dtype)


def _attn_kernel(q_ref, k_ref, v_ref, cos_ref, sin_ref, o_ref, lse_ref, kk_ref, vv_ref, *, rows_per_residue):
    n = pl.program_id(2)
    blk = ATTN_BACK
    nsub = ATTN_SUB_BLOCKS

    @pl.when(n == 0)
    def _():
        kk_ref[0:blk, :] = jnp.zeros((blk, GROUP_WIDTH), BF16)
        vv_ref[0:blk, :] = jnp.zeros((blk, GROUP_WIDTH), BF16)

    @pl.when(n > 0)
    def _():
        kk_ref[0:blk, :] = kk_ref[nsub * blk:(nsub + 1) * blk, :]
        vv_ref[0:blk, :] = vv_ref[nsub * blk:(nsub + 1) * blk, :]

    full = slice(None)
    cos = _sub_blocks(cos_ref, full, nsub)
    sin = _sub_blocks(sin_ref, full, nsub)
    scale = HEAD_DIM ** -0.5 * math.log2(math.e)

    qi = lax.broadcasted_iota(jnp.int32, (blk, 2 * blk), 0)
    kj = lax.broadcasted_iota(jnp.int32, (blk, 2 * blk), 1)
    q_pos = _block_position(qi, rows_per_residue, blk)
    k_pos = _block_position(kj & (blk - 1), rows_per_residue, blk) - jnp.where(kj < blk, blk, 0)
    dist = q_pos - k_pos
    band = (dist >= 0) & (dist <= ATTN_BACK)
    first_mask = band & (kj >= jnp.where(n > 0, 0, blk))
    lane_head = lax.shift_right_logical(lax.broadcasted_iota(jnp.int32, (blk, HEADS * LSE_LANES), 1),
                                        LSE_LANES.bit_length() - 1)
    lse_tiles = [jnp.zeros((blk, HEADS * LSE_LANES), F32) for _ in range(nsub)]

    for h in range(HEADS):
        sl = slice(h * HEAD_DIM, (h + 1) * HEAD_DIM)
        for s, (kf, vf) in enumerate(zip(_sub_blocks(k_ref, sl, nsub), _sub_blocks(v_ref, sl, nsub))):
            rows = slice((s + 1) * blk, (s + 2) * blk)
            kk_ref[rows, sl] = (kf * cos[s] + pltpu.roll(kf, HEAD_DIM // 2, 1) * sin[s]).astype(BF16)
            vv_ref[rows, sl] = vf.astype(BF16)
        outs = []
        for s, qf in enumerate(_sub_blocks(q_ref, sl, nsub)):
            keys = slice(s * blk, (s + 2) * blk)
            qr = ((qf * cos[s] + pltpu.roll(qf, HEAD_DIM // 2, 1) * sin[s]) * scale).astype(BF16)
            sc = lax.dot_general(qr, kk_ref[keys, sl], (((1,), (1,)), ((), ())), preferred_element_type=F32)
            sc = jnp.where(first_mask if s == 0 else band, sc, NEG_BIG)
            m = jnp.max(sc, axis=-1, keepdims=True)
            p = jnp.exp2(sc - m)
            den = jnp.sum(p, axis=-1, keepdims=True)
            o = jnp.dot(p.astype(BF16), vv_ref[keys, sl], preferred_element_type=F32)
            outs.append(o / den)
            lse = m * math.log(2.0) + jnp.log(den)
            lse_tiles[s] = jnp.where(lane_head == h, lse, lse_tiles[s])
        _store_sub_blocks(o_ref, sl, outs)
    _store_sub_blocks(lse_ref, full, lse_tiles)


def _attn_group(proj, cos_r, sin_r, g, dilation, batch, seq):
    per = seq // RES
    a = RES // dilation
    rows = ATTN_BACK // a
    rows_blk = rows * ATTN_SUB_BLOCKS
    nblk = per // rows_blk
    view = lambda arr: arr.reshape(arr.shape[:-3] + (a, dilation) + arr.shape[-2:])

    def spec(width, col_block, lead):
        shape = lead + (a, None, rows_blk, width)
        if lead:
            return pl.BlockSpec(shape, lambda b, r, n: (b, 0, r, n, col_block))
        return pl.BlockSpec(shape, lambda b, r, n: (0, r, n, col_block))

    qkv = [spec(GROUP_WIDTH, which * N_GROUPS + g, (None,)) for which in range(3)]
    tab = spec(HEAD_DIM, 0, ())
    lse_w = HEADS * LSE_LANES
    proj_v = view(proj)
    key_rows = (ATTN_SUB_BLOCKS + 1) * ATTN_BACK
    out, lse = pl.pallas_call(
        functools.partial(_attn_kernel, rows_per_residue=rows),
        grid=(batch, dilation, nblk),
        in_specs=qkv + [tab, tab],
        out_specs=[spec(GROUP_WIDTH, 0, (None,)), spec(lse_w, 0, (None,))],
        out_shape=[jax.ShapeDtypeStruct((batch, a, dilation, per, GROUP_WIDTH), BF16),
                   jax.ShapeDtypeStruct((batch, a, dilation, per, lse_w), F32)],
        scratch_shapes=[pltpu.VMEM((key_rows, GROUP_WIDTH), BF16),
                        pltpu.VMEM((key_rows, GROUP_WIDTH), BF16)],
        compiler_params=_params(("parallel", "parallel", "arbitrary")),
        name=f"attn_d{dilation}",
    )(proj_v, proj_v, proj_v, view(cos_r), view(sin_r))
    return out.reshape(batch, RES, per, GROUP_WIDTH), lse.reshape(batch, RES, per, lse_w)


def _hgrn_cumsum_matrix():
    c = HGRN_CHUNK
    i = np.arange(c)
    per = c // RES
    pos = (i % per) * RES + i // per
    return (pos[None, :] <= pos[:, None]).astype(np.float32)


def _level_reference(b3, half):
    res, per, dk = b3.shape
    if half < res:
        two = 2 * half
        parts = [jnp.broadcast_to(b3[blk * two + half - 1][None], (two, per, dk)) for blk in range(res // two)]
        return parts[0] if len(parts) == 1 else jnp.concatenate(parts, axis=0)
    last = b3[res - 1]
    step = half // res
    u = lax.broadcasted_iota(jnp.int32, (per, dk), 0)
    ref = None
    for blk in reversed(range(per // (2 * step))):
        row = jnp.broadcast_to(last[blk * 2 * step + step - 1][None], (per, dk))
        ref = row if ref is None else jnp.where(u < (blk + 1) * 2 * step, row, ref)
    return jnp.broadcast_to(ref[None], (res, per, dk))


def _scale_valid_rows(t3, e, half, lower):
    res = t3.shape[0]
    if half >= res:
        return t3 * e
    parts = []
    for lo in range(0, res, 2 * half):
        up, dn = slice(lo, lo + half), slice(lo + half, lo + 2 * half)
        parts += [t3[up], t3[dn] * e[dn]] if lower else [t3[up] * e[up], t3[dn]]
    return jnp.concatenate(parts, axis=0)


def _hgrn_kernel(qr_ref, fr_ref, ir_ref, gr_ref, hlb_ref, gain_ref, tri_ref, o_ref,
                 st_ref, q_s, k_s, g_s, v_s, gate_s, out_s, cb_s):
    c = HGRN_CHUNK
    dk = HEAD_DIM
    per = c // RES

    @pl.when(pl.program_id(1) == 0)
    def _():
        st_ref[...] = jnp.zeros(st_ref.shape, F32)

    h0 = hlb_ref[0:1, :]
    h1 = hlb_ref[1:2, :]
    hm = jnp.maximum(h0, h1)
    e0 = jnp.exp(h0 - hm)
    e1 = jnp.exp(h1 - hm)
    lb = e0 / (e0 + e1)

    c1 = 0.5 * (1.0 - lb)
    c0 = 1.0 - c1
    c1_tanh = c1 * jnp.tanh(0.5 * fr_ref[...].astype(F32))
    g_s[...] = jnp.log(c0 + c1_tanh)
    k_s[...] = c1 - c1_tanh
    q_s[...] = _silu(qr_ref[...].astype(F32))
    v_s[...] = ir_ref[...].astype(F32)
    gate_s[...] = _silu(gr_ref[...].astype(F32))

    ti = _block_position(lax.broadcasted_iota(jnp.int32, (c, c), 0), per, c)
    si = _block_position(lax.broadcasted_iota(jnp.int32, (c, c), 1), per, c)
    diag_mask = ti == si
    level_masks = []
    for half in HGRN_LEVELS:
        shift = half.bit_length()
        level_masks.append((lax.shift_right_logical(ti, shift) == lax.shift_right_logical(si, shift))
                           & ((ti & half) != 0) & ((si & half) == 0))
    ones_b = jnp.ones((dk, c), BF16)
    tri = tri_ref[...]

    def chunk_body(ci, carry):
        rows = pl.ds(pl.multiple_of(ci * per, per), per)
        g = g_s[:, rows, :].reshape(c, GROUP_WIDTH)
        g_hi = g.astype(BF16)
        r1 = g - g_hi.astype(F32)
        g_mid = r1.astype(BF16)
        g_lo = (r1 - g_mid.astype(F32)).astype(BF16)
        cb_s[...] = (jnp.dot(tri, g_hi, preferred_element_type=F32)
                     + jnp.dot(tri, g_mid, preferred_element_type=F32)
                     + jnp.dot(tri, g_lo, preferred_element_type=F32))

        for h in range(HEADS):
            sl = slice(h * dk, (h + 1) * dk)
            q3 = q_s[:, rows, sl]
            k3 = k_s[:, rows, sl]
            q = q3.reshape(c, dk)
            k = k3.reshape(c, dk)
            v = v_s[:, rows, sl].reshape(c, dk).astype(BF16)
            b = cb_s[:, sl]
            b3 = b.reshape(RES, per, dk)
            b_last = b[c - 1:c, :]
            st = st_ref[h]

            q_dec = (q * jnp.exp(b)).astype(BF16)
            o = lax.dot_general(q_dec, st.astype(BF16), (((1,), (1,)), ((), ())),
                                preferred_element_type=F32)

            qk = (q * k).astype(BF16)
            a = jnp.where(diag_mask, jnp.dot(qk, ones_b, preferred_element_type=F32), 0.0)
            for li, half in enumerate(HGRN_LEVELS):
                e = jnp.exp2(jnp.abs(b3 - _level_reference(b3, half)) * -LOG2E)
                q_l = _scale_valid_rows(q3, e, half, lower=True).reshape(c, dk).astype(BF16)
                k_l = _scale_valid_rows(k3, e, half, lower=False).reshape(c, dk).astype(BF16)
                p_l = lax.dot_general(q_l, k_l, (((1,), (1,)), ((), ())), preferred_element_type=F32)
                a = jnp.where(level_masks[li], p_l, a)
            o = o + jnp.dot(a.astype(BF16), v, preferred_element_type=F32)

            k_dec = (k * jnp.exp(b_last - b)).astype(BF16)
            st_ref[h] = st * jnp.exp(b_last) + lax.dot_general(
                v, k_dec, (((0,), (0,)), ((), ())), preferred_element_type=F32)

            r = o * _rms_scale(o) * gain_ref[:, sl] * gate_s[:, rows, sl].reshape(c, dk)
            out_s[:, rows, sl] = r.reshape(RES, per, dk)
        return carry

    lax.fori_loop(0, o_ref.shape[1] // per, chunk_body, 0)
    o_ref[...] = out_s[...].astype(o_ref.dtype)


def _hgrn(proj, lower_bounds, gain, *, first_col_block, tc=512):
    assert lower_bounds.shape[0] == 2, "forget-gate bound rows: DEPTH + 1 with DEPTH == 1"
    batch, _, per, _ = proj.shape
    rows = tc // RES
    tri = jnp.asarray(_hgrn_cumsum_matrix(), BF16)

    def col_spec(k):
        return pl.BlockSpec((None, RES, rows, GROUP_WIDTH), lambda b, s: (b, 0, s, first_col_block + k))

    const = lambda b, s: (0, 0)
    tile = pltpu.VMEM((RES, rows, GROUP_WIDTH), F32)
    return pl.pallas_call(
        _hgrn_kernel,
        grid=(batch, per // rows),
        in_specs=[col_spec(0), col_spec(1), col_spec(2), col_spec(3),
                  pl.BlockSpec(lower_bounds.shape, const),
                  pl.BlockSpec(gain.shape, const),
                  pl.BlockSpec(tri.shape, const)],
        out_specs=pl.BlockSpec((None, RES, rows, GROUP_WIDTH), lambda b, s: (b, 0, s, 0)),
        out_shape=jax.ShapeDtypeStruct((batch, RES, per, GROUP_WIDTH), BF16),
        scratch_shapes=[pltpu.VMEM((HEADS, HEAD_DIM, HEAD_DIM), F32),
                        tile, tile, tile, tile, tile, tile,
                        pltpu.VMEM((HGRN_CHUNK, GROUP_WIDTH), F32)],
        compiler_params=_params(("parallel", "arbitrary")),
        name="hgrn2",
    )(proj, proj, proj, proj, lower_bounds, gain, tri)


def _merge_kernel(o1_ref, o2_ref, o3_ref, l1_ref, l2_ref, l3_ref, r_ref,
                  ga0_ref, ga1_ref, gr0_ref, gr1_ref, x_ref,
                  wpa_ref, wpr_ref, wo_ref, gpost_ref, gffn_ref,
                  h_ref, a2_ref, attn_s, y_s):
    tm = x_ref.shape[0]
    per = tm // RES
    flat = lambda ref: ref[...].reshape(tm, ref.shape[-1])

    la, lb, lc = flat(l1_ref), flat(l2_ref), flat(l3_ref)
    lm = jnp.maximum(la, jnp.maximum(lb, lc))
    ea, eb, ec = jnp.exp(la - lm), jnp.exp(lb - lm), jnp.exp(lc - lm)
    den = ea + eb + ec
    wa, wb, wc = ea / den, eb / den, ec / den
    o1, o2, o3 = flat(o1_ref), flat(o2_ref), flat(o3_ref)
    for h in range(HEADS):
        sl = slice(h * HEAD_DIM, (h + 1) * HEAD_DIM)
        c0 = h * LSE_LANES
        mixed = (wa[:, c0:c0 + 1] * o1[:, sl].astype(F32)
                 + wb[:, c0:c0 + 1] * o2[:, sl].astype(F32)
                 + wc[:, c0:c0 + 1] * o3[:, sl].astype(F32))
        attn_s[:, sl] = mixed.astype(BF16)

    attn = attn_s[...]
    r = flat(r_ref)
    half_w = GROUP_WIDTH
    for half, (ga_ref, gr_ref) in enumerate(((ga0_ref, gr0_ref), (ga1_ref, gr1_ref))):
        cs = slice(half * half_w, (half + 1) * half_w)
        ya = jnp.dot(attn, wpa_ref[:, cs], preferred_element_type=F32)
        yr = jnp.dot(r, wpr_ref[:, cs], preferred_element_type=F32)
        y = _sigmoid(flat(ga_ref).astype(F32)) * ya + _sigmoid(flat(gr_ref).astype(F32)) * yr
        y_s[:, cs] = y.astype(BF16)

    tok = lax.broadcasted_iota(jnp.int32, (tm, tm), 0)
    src = _block_position(lax.broadcasted_iota(jnp.int32, (tm, tm), 1), per, tm)
    unpermute = jnp.where(src == tok, 1.0, 0.0).astype(BF16)
    y_nat = jnp.dot(unpermute, y_s[...], preferred_element_type=F32).astype(BF16)

    z = jnp.dot(y_nat, wo_ref[...], preferred_element_type=F32)
    h1 = x_ref[...] + z * _rms_scale(z) * gpost_ref[...]
    h_ref[...] = h1
    a2_ref[...] = (h1 * _rms_scale(h1) * gffn_ref[...]).astype(BF16)


def _merge(outs, lses, r_out, proj, x2, wpa, wpr, wo, gpost, gffn, *, gate_col_block, tm=256):
    t, d = x2.shape
    batch, _, per, _ = proj.shape
    rows = tm // RES
    tiles_per_seq = per // rows
    const = lambda i: (0, 0)
    nat = pl.BlockSpec((tm, d), lambda i: (i, 0))

    def lay(width, col_block=0):
        return pl.BlockSpec((None, RES, rows, width),
                            lambda i: (i // tiles_per_seq, 0, i % tiles_per_seq, col_block))

    def resident(shape):
        return pl.BlockSpec(shape, const, pipeline_mode=pl.Buffered(1))

    lse_w = HEADS * LSE_LANES
    return pl.pallas_call(
        _merge_kernel,
        grid=(t // tm,),
        in_specs=[lay(GROUP_WIDTH)] * 3 + [lay(lse_w)] * 3 + [lay(GROUP_WIDTH)]
        + [lay(GROUP_WIDTH, gate_col_block + k) for k in range(4)]
        + [nat, resident(wpa.shape), resident(wpr.shape), resident(wo.shape),
           resident(gpost.shape), resident(gffn.shape)],
        out_specs=[nat, nat],
        out_shape=[jax.ShapeDtypeStruct((t, d), F32), jax.ShapeDtypeStruct((t, d), BF16)],
        scratch_shapes=[pltpu.VMEM((tm, GROUP_WIDTH), BF16), pltpu.VMEM((tm, d), BF16)],
        compiler_params=_params(("parallel",)),
        name="merge_out_proj",
    )(*outs, *lses, r_out, proj, proj, proj, proj, x2, wpa, wpr, wo, gpost, gffn)


def _ffn_kernel(a_ref, wg_ref, wu_ref, wd_ref, h_ref, gain_ref, o_ref, acc_ref):
    j = pl.program_id(1)

    @pl.when(j == 0)
    def _():
        acc_ref[...] = jnp.zeros(acc_ref.shape, F32)

    a = a_ref[...]
    gt = jnp.dot(a, wg_ref[...], preferred_element_type=F32)
    up = jnp.dot(a, wu_ref[...], preferred_element_type=F32)
    hidden = (_silu(gt) * up).astype(BF16)
    acc_ref[...] += jnp.dot(hidden, wd_ref[...], preferred_element_type=F32)

    @pl.when(j == pl.num_programs(1) - 1)
    def _():
        ff = acc_ref[...]
        o_ref[...] = h_ref[...] + ff * _rms_scale(ff) * gain_ref[...]


def _ffn(a2, w_gate_up, w_down, h1, gain, *, tm=512, tf=512):
    t, d = a2.shape
    d_ff = w_down.shape[0]
    nf = d_ff // tf
    rows = pl.BlockSpec((tm, d), lambda i, j: (i, 0))
    return pl.pallas_call(
        _ffn_kernel,
        grid=(t // tm, nf),
        in_specs=[
            rows,
            pl.BlockSpec((d, tf), lambda i, j: (0, j)),
            pl.BlockSpec((d, tf), lambda i, j: (0, nf + j)),
            pl.BlockSpec((tf, d), lambda i, j: (j, 0)),
            rows,
            pl.BlockSpec((1, d), lambda i, j: (0, 0)),
        ],
        out_specs=rows,
        out_shape=jax.ShapeDtypeStruct((t, d), F32),
        scratch_shapes=[pltpu.VMEM((tm, d), F32)],
        compiler_params=_params(("parallel", "arbitrary")),
        name="swiglu_ffn",
    )(a2, w_gate_up, w_gate_up, w_down, h1, gain)


def _rotary_tables(seq):
    inv_freq = ROPE_THETA ** (-jnp.arange(0, HEAD_DIM, 2, dtype=F32) / HEAD_DIM)
    ang = jnp.arange(seq, dtype=F32)[:, None] * inv_freq[None, :]
    cos = jnp.concatenate([jnp.cos(ang), jnp.cos(ang)], axis=-1)
    sin_signed = jnp.concatenate([-jnp.sin(ang), jnp.sin(ang)], axis=-1)
    lay = lambda tab: tab.reshape(seq // RES, RES, HEAD_DIM).transpose(1, 0, 2)
    return lay(cos), lay(sin_signed)


def kernel(x, w_in, w_attn_branch, w_hgrn_branch, w_mix_out, hgrn_lower_bounds, hgrn_norm_gain,
           norm_mix_pre, norm_mix_post, w_ffn_gate_up, w_ffn_down, norm_ffn_pre, norm_ffn_post):
    batch, seq, d = x.shape
    assert w_in.shape[0] == 1, "single-layer block"
    x2 = x.reshape(batch * seq, d)
    row = lambda v: v[0].reshape(1, -1).astype(F32)

    proj = _in_proj(x2, row(norm_mix_pre), w_in[0].astype(BF16), batch, seq)

    cos_r, sin_r = _rotary_tables(seq)
    outs, lses = [], []
    for g, (window, dilation) in enumerate(ATTN_GROUPS):
        assert window // dilation == ATTN_BACK and seq % window == 0 and RES % dilation == 0
        o_g, lse_g = _attn_group(proj, cos_r, sin_r, g, dilation, batch, seq)
        outs.append(o_g)
        lses.append(lse_g)

    qkv_blocks = 3 * N_GROUPS
    r_out = _hgrn(proj, hgrn_lower_bounds.astype(F32), row(hgrn_norm_gain), first_col_block=qkv_blocks)

    h1, a2 = _merge(outs, lses, r_out, proj, x2,
                    w_attn_branch[0].astype(BF16), w_hgrn_branch[0].astype(BF16),
                    w_mix_out[0].astype(BF16), row(norm_mix_post), row(norm_ffn_pre),
                    gate_col_block=qkv_blocks + 4)

    out = _ffn(a2, w_ffn_gate_up[0].astype(BF16), w_ffn_down[0].astype(BF16), h1, row(norm_ffn_post))
    return out.reshape(batch, seq, d)
```

```python
import functools
import math

import numpy as np
import jax
import jax.numpy as jnp
from jax import lax
from jax.experimental import pallas as pl
from jax.experimental.pallas import tpu as pltpu

F32 = jnp.float32
BF16 = jnp.bfloat16

HEAD_DIM = 128
ATTN_GROUPS = ((128, 1), (512, 4), (2048, 16))
N_GROUPS = len(ATTN_GROUPS)
HEADS = 8
GROUP_WIDTH = HEADS * HEAD_DIM
ATTN_BACK = 128
ATTN_SUB_BLOCKS = 2
RES = 16
ROPE_THETA = 10000.0
NORM_EPS = 1e-6
HGRN_CHUNK = 128
HGRN_LEVELS = (64, 32, 16, 8, 4, 2, 1)
LSE_LANES = 16
NEG_BIG = -1e30
PERMUTE_GROUP = 256
LOG2E = math.log2(math.e)

VMEM_LIMIT_BYTES = 56 * 1024 * 1024


def _params(semantics):
    return pltpu.CompilerParams(dimension_semantics=semantics, vmem_limit_bytes=VMEM_LIMIT_BYTES)


def _rms_scale(v):
    return lax.rsqrt(jnp.mean(v * v, axis=-1, keepdims=True) + NORM_EPS)


def _sigmoid(v):
    return 0.5 * jnp.tanh(0.5 * v) + 0.5


def _silu(v):
    return v * _sigmoid(v)


def _block_position(i, rows_per_residue, block_rows):
    shift = rows_per_residue.bit_length() - 1
    n_res_shift = (block_rows // rows_per_residue).bit_length() - 1
    return ((i & (rows_per_residue - 1)) << n_res_shift) | lax.shift_right_logical(i, shift)


def _in_proj_kernel(x_ref, gain_ref, w_ref, o_ref, a_ref):
    tm = x_ref.shape[0]
    per_res = tm // RES
    grp = PERMUTE_GROUP
    per_grp = grp // RES

    @pl.when(pl.program_id(1) == 0)
    def _():
        row_token = _block_position(lax.broadcasted_iota(jnp.int32, (grp, grp), 0), per_grp, grp)
        permute = jnp.where(row_token == lax.broadcasted_iota(jnp.int32, (grp, grp), 1), 1.0, 0.0).astype(BF16)
        gain = gain_ref[...]
        for gi in range(tm // grp):
            xg = x_ref[gi * grp:(gi + 1) * grp, :]
            ag = (xg * _rms_scale(xg) * gain).astype(BF16)
            pg = jnp.dot(permute, ag, preferred_element_type=F32).astype(BF16)
            for r in range(RES):
                dst = r * per_res + gi * per_grp
                a_ref[dst:dst + per_grp, :] = pg[r * per_grp:(r + 1) * per_grp, :]

    res = jnp.dot(a_ref[...], w_ref[...], preferred_element_type=F32)
    o_ref[...] = res.reshape(RES, per_res, res.shape[-1]).astype(o_ref.dtype)


def _in_proj(x2, gain, w, batch, seq, *, tm=1024, tn=1024):
    t, d = x2.shape
    n = w.shape[1]
    tiles_per_seq = seq // tm
    return pl.pallas_call(
        _in_proj_kernel,
        grid=(t // tm, n // tn),
        in_specs=[
            pl.BlockSpec((tm, d), lambda i, j: (i, 0)),
            pl.BlockSpec((1, d), lambda i, j: (0, 0)),
            pl.BlockSpec((d, tn), lambda i, j: (0, j)),
        ],
        out_specs=pl.BlockSpec((None, RES, tm // RES, tn),
                               lambda i, j: (i // tiles_per_seq, 0, i % tiles_per_seq, j)),
        out_shape=jax.ShapeDtypeStruct((batch, RES, seq // RES, n), BF16),
        scratch_shapes=[pltpu.VMEM((tm, d), BF16)],
        compiler_params=_params(("parallel", "arbitrary")),
        name="in_proj",
    )(x2, gain, w)


def _sub_blocks(ref, sl, nsub):
    v = ref[:, :, sl].astype(F32)
    a, u, w = v.shape
    step = u // nsub
    return [v[:, s * step:(s + 1) * step, :].reshape(a * step, w) for s in range(nsub)]


def _store_sub_blocks(ref, sl, blocks):
    a = ref.shape[0]
    parts = [b.reshape(a, b.shape[0] // a, b.shape[1]) for b in blocks]
    ref[:, :, sl] = jnp.concatenate(parts, axis=1).astype(ref.dtype)


def _attn_body(n, q_ref, k_ref, v_ref, cos_ref, sin_ref, o_ref, lse_ref, kk_ref, vv_ref, *, rows_per_residue):
    blk = ATTN_BACK
    nsub = ATTN_SUB_BLOCKS

    @pl.when(n == 0)
    def _():
        kk_ref[0:blk, :] = jnp.zeros((blk, GROUP_WIDTH), BF16)
        vv_ref[0:blk, :] = jnp.zeros((blk, GROUP_WIDTH), BF16)

    @pl.when(n > 0)
    def _():
        kk_ref[0:blk, :] = kk_ref[nsub * blk:(nsub + 1) * blk, :]
        vv_ref[0:blk, :] = vv_ref[nsub * blk:(nsub + 1) * blk, :]

    full = slice(None)
    cos = _sub_blocks(cos_ref, full, nsub)
    sin = _sub_blocks(sin_ref, full, nsub)
    scale = HEAD_DIM ** -0.5 * math.log2(math.e)

    qi = lax.broadcasted_iota(jnp.int32, (blk, 2 * blk), 0)
    kj = lax.broadcasted_iota(jnp.int32, (blk, 2 * blk), 1)
    q_pos = _block_position(qi, rows_per_residue, blk)
    k_pos = _block_position(kj & (blk - 1), rows_per_residue, blk) - jnp.where(kj < blk, blk, 0)
    dist = q_pos - k_pos
    band = (dist >= 0) & (dist <= ATTN_BACK)
    first_mask = band & (kj >= jnp.where(n > 0, 0, blk))
    lane_head = lax.shift_right_logical(lax.broadcasted_iota(jnp.int32, (blk, HEADS * LSE_LANES), 1),
                                        LSE_LANES.bit_length() - 1)
    lse_tiles = [jnp.zeros((blk, HEADS * LSE_LANES), F32) for _ in range(nsub)]

    for h in range(HEADS):
        sl = slice(h * HEAD_DIM, (h + 1) * HEAD_DIM)
        for s, (kf, vf) in enumerate(zip(_sub_blocks(k_ref, sl, nsub), _sub_blocks(v_ref, sl, nsub))):
            rows = slice((s + 1) * blk, (s + 2) * blk)
            kk_ref[rows, sl] = (kf * cos[s] + pltpu.roll(kf, HEAD_DIM // 2, 1) * sin[s]).astype(BF16)
            vv_ref[rows, sl] = vf.astype(BF16)
        outs = []
        for s, qf in enumerate(_sub_blocks(q_ref, sl, nsub)):
            keys = slice(s * blk, (s + 2) * blk)
            qr = ((qf * cos[s] + pltpu.roll(qf, HEAD_DIM // 2, 1) * sin[s]) * scale).astype(BF16)
            sc = lax.dot_general(qr, kk_ref[keys, sl], (((1,), (1,)), ((), ())), preferred_element_type=F32)
            sc = jnp.where(first_mask if s == 0 else band, sc, NEG_BIG)
            m = jnp.max(sc, axis=-1, keepdims=True)
            p = jnp.exp2(sc - m)
            den = jnp.sum(p, axis=-1, keepdims=True)
            o = jnp.dot(p.astype(BF16), vv_ref[keys, sl], preferred_element_type=F32)
            outs.append(o / den)
            lse = m * math.log(2.0) + jnp.log(den)
            lse_tiles[s] = jnp.where(lane_head == h, lse, lse_tiles[s])
        _store_sub_blocks(o_ref, sl, outs)
    _store_sub_blocks(lse_ref, full, lse_tiles)


def _attn_operands(proj, cos_r, sin_r, g, dilation):
    batch, _, per, _ = proj.shape
    a = RES // dilation
    rows = ATTN_BACK // a
    rows_blk = rows * ATTN_SUB_BLOCKS
    nblk = per // rows_blk
    view = lambda arr: arr.reshape(arr.shape[:-3] + (a, dilation) + arr.shape[-2:])

    def spec(width, col_block, lead):
        shape = lead + (a, None, rows_blk, width)
        if lead:
            return pl.BlockSpec(shape, lambda i: (i // (dilation * nblk), 0, (i // nblk) % dilation,
                                                  i % nblk, col_block))
        return pl.BlockSpec(shape, lambda i: (0, (i // nblk) % dilation, i % nblk, col_block))

    lse_w = HEADS * LSE_LANES
    proj_v = view(proj)
    key_rows = (ATTN_SUB_BLOCKS + 1) * ATTN_BACK
    return dict(
        inputs=[proj_v, proj_v, proj_v, view(cos_r), view(sin_r)],
        in_specs=[spec(GROUP_WIDTH, which * N_GROUPS + g, (None,)) for which in range(3)]
        + [spec(HEAD_DIM, 0, ())] * 2,
        out_specs=[spec(GROUP_WIDTH, 0, (None,)), spec(lse_w, 0, (None,))],
        out_shape=[jax.ShapeDtypeStruct((batch, a, dilation, per, GROUP_WIDTH), BF16),
                   jax.ShapeDtypeStruct((batch, a, dilation, per, lse_w), F32)],
        scratch=[pltpu.VMEM((key_rows, GROUP_WIDTH), BF16), pltpu.VMEM((key_rows, GROUP_WIDTH), BF16)],
        rows_per_residue=rows, nblk=nblk)


def _hgrn_cumsum_matrix():
    c = HGRN_CHUNK
    i = np.arange(c)
    per = c // RES
    pos = (i % per) * RES + i // per
    return (pos[None, :] <= pos[:, None]).astype(np.float32)


def _level_reference(b3, half):
    res, per, dk = b3.shape
    if half < res:
        two = 2 * half
        parts = [jnp.broadcast_to(b3[blk * two + half - 1][None], (two, per, dk)) for blk in range(res // two)]
        return parts[0] if len(parts) == 1 else jnp.concatenate(parts, axis=0)
    last = b3[res - 1]
    step = half // res
    u = lax.broadcasted_iota(jnp.int32, (per, dk), 0)
    ref = None
    for blk in reversed(range(per // (2 * step))):
        row = jnp.broadcast_to(last[blk * 2 * step + step - 1][None], (per, dk))
        ref = row if ref is None else jnp.where(u < (blk + 1) * 2 * step, row, ref)
    return jnp.broadcast_to(ref[None], (res, per, dk))


def _scale_valid_rows(t3, e, half, lower):
    res = t3.shape[0]
    if half >= res:
        return t3 * e
    parts = []
    for lo in range(0, res, 2 * half):
        up, dn = slice(lo, lo + half), slice(lo + half, lo + 2 * half)
        parts += [t3[up], t3[dn] * e[dn]] if lower else [t3[up] * e[up], t3[dn]]
    return jnp.concatenate(parts, axis=0)


def _hgrn_body(first, qr_ref, fr_ref, ir_ref, gr_ref, hlb_ref, gain_ref, tri_ref, o_ref, st_ref):
    c = HGRN_CHUNK
    dk = HEAD_DIM
    per = c // RES

    @pl.when(first)
    def _():
        st_ref[...] = jnp.zeros(st_ref.shape, F32)

    h0 = hlb_ref[0:1, :]
    h1 = hlb_ref[1:2, :]
    hm = jnp.maximum(h0, h1)
    e0 = jnp.exp(h0 - hm)
    e1 = jnp.exp(h1 - hm)
    lb = e0 / (e0 + e1)

    c1 = 0.5 * (1.0 - lb)
    c0 = 1.0 - c1
    c1_tanh = c1 * jnp.tanh(0.5 * fr_ref[...].astype(F32))
    g_all = jnp.log(c0 + c1_tanh)
    k_all = c1 - c1_tanh
    q_all = _silu(qr_ref[...].astype(F32))
    v_all = ir_ref[...].astype(F32)
    gate_all = _silu(gr_ref[...].astype(F32))

    ti = _block_position(lax.broadcasted_iota(jnp.int32, (c, c), 0), per, c)
    si = _block_position(lax.broadcasted_iota(jnp.int32, (c, c), 1), per, c)
    diag_mask = ti == si
    level_masks = []
    for half in HGRN_LEVELS:
        shift = half.bit_length()
        level_masks.append((lax.shift_right_logical(ti, shift) == lax.shift_right_logical(si, shift))
                           & ((ti & half) != 0) & ((si & half) == 0))
    ones_b = jnp.ones((dk, c), BF16)
    tri = tri_ref[...]

    n_chunks = o_ref.shape[1] // per
    outs = [[None] * n_chunks for _ in range(HEADS)]
    for ci in range(n_chunks):
        rows = slice(ci * per, (ci + 1) * per)
        g = g_all[:, rows, :].reshape(c, GROUP_WIDTH)
        g_hi = g.astype(BF16)
        r1 = g - g_hi.astype(F32)
        g_mid = r1.astype(BF16)
        g_lo = (r1 - g_mid.astype(F32)).astype(BF16)
        cum = (jnp.dot(tri, g_hi, preferred_element_type=F32)
               + jnp.dot(tri, g_mid, preferred_element_type=F32)
               + jnp.dot(tri, g_lo, preferred_element_type=F32))

        for h in range(HEADS):
            sl = slice(h * dk, (h + 1) * dk)
            q3 = q_all[:, rows, sl]
            k3 = k_all[:, rows, sl]
            q = q3.reshape(c, dk)
            k = k3.reshape(c, dk)
            v = v_all[:, rows, sl].reshape(c, dk).astype(BF16)
            b = cum[:, sl]
            b3 = b.reshape(RES, per, dk)
            b_last = b[c - 1:c, :]
            st = st_ref[h]

            q_dec = (q * jnp.exp(b)).astype(BF16)
            o = lax.dot_general(q_dec, st.astype(BF16), (((1,), (1,)), ((), ())),
                                preferred_element_type=F32)

            qk = (q * k).astype(BF16)
            a = jnp.where(diag_mask, jnp.dot(qk, ones_b, preferred_element_type=F32), 0.0)
            for li, half in enumerate(HGRN_LEVELS):
                e = jnp.exp2(jnp.abs(b3 - _level_reference(b3, half)) * -LOG2E)
                q_l = _scale_valid_rows(q3, e, half, lower=True).reshape(c, dk).astype(BF16)
                k_l = _scale_valid_rows(k3, e, half, lower=False).reshape(c, dk).astype(BF16)
                p_l = lax.dot_general(q_l, k_l, (((1,), (1,)), ((), ())), preferred_element_type=F32)
                a = jnp.where(level_masks[li], p_l, a)
            o = o + jnp.dot(a.astype(BF16), v, preferred_element_type=F32)

            k_dec = (k * jnp.exp(b_last - b)).astype(BF16)
            st_ref[h] = st * jnp.exp(b_last) + lax.dot_general(
                v, k_dec, (((0,), (0,)), ((), ())), preferred_element_type=F32)

            r = o * _rms_scale(o) * gain_ref[:, sl] * gate_all[:, rows, sl].reshape(c, dk)
            outs[h][ci] = r.reshape(RES, per, dk)

    for h in range(HEADS):
        o_ref[:, :, h * dk:(h + 1) * dk] = jnp.concatenate(outs[h], axis=1).astype(o_ref.dtype)


def _hgrn_operands(proj, lower_bounds, gain, *, first_col_block, rows):
    assert lower_bounds.shape[0] == 2, "forget-gate bound rows: DEPTH + 1 with DEPTH == 1"
    batch, _, per, _ = proj.shape
    tiles = per // rows
    tri = jnp.asarray(_hgrn_cumsum_matrix(), BF16)

    def col_spec(col_block):
        return pl.BlockSpec((None, RES, rows, GROUP_WIDTH), lambda i: (i // tiles, 0, i % tiles, col_block))

    const = lambda i: (0, 0)
    return dict(
        inputs=[proj, proj, proj, proj, lower_bounds, gain, tri],
        in_specs=[col_spec(first_col_block + k) for k in range(4)]
        + [pl.BlockSpec(lower_bounds.shape, const), pl.BlockSpec(gain.shape, const),
           pl.BlockSpec(tri.shape, const)],
        out_specs=[col_spec(0)],
        out_shape=[jax.ShapeDtypeStruct((batch, RES, per, GROUP_WIDTH), BF16)],
        scratch=[pltpu.VMEM((HEADS, HEAD_DIM, HEAD_DIM), F32)],
        tiles=tiles)


def _mixers(proj, cos_r, sin_r, lower_bounds, gain, *, hgrn_col_block):
    batch, _, per, _ = proj.shape
    rows = ATTN_SUB_BLOCKS * ATTN_BACK // RES
    steps = batch * per // rows
    parts = [_attn_operands(proj, cos_r, sin_r, g, dilation) for g, (_, dilation) in enumerate(ATTN_GROUPS)]
    parts.append(_hgrn_operands(proj, lower_bounds, gain, first_col_block=hgrn_col_block, rows=rows))
    n_in = [len(p["inputs"]) for p in parts]
    n_out = [len(p["out_shape"]) for p in parts]
    n_scr = [len(p["scratch"]) for p in parts]

    def kernel_body(*refs):
        ins, rest = refs[:sum(n_in)], refs[sum(n_in):]
        outs, scr = rest[:sum(n_out)], rest[sum(n_out):]
        i = pl.program_id(0)
        take = lambda seq, counts, k: seq[sum(counts[:k]):sum(counts[:k + 1])]
        for k, part in enumerate(parts[:-1]):
            _attn_body(i % part["nblk"], *take(ins, n_in, k), *take(outs, n_out, k), *take(scr, n_scr, k),
                       rows_per_residue=part["rows_per_residue"])
        k = len(parts) - 1
        _hgrn_body(i % parts[k]["tiles"] == 0, *take(ins, n_in, k), *take(outs, n_out, k), *take(scr, n_scr, k))

    flat = lambda key: [v for p in parts for v in p[key]]
    res = pl.pallas_call(
        kernel_body,
        grid=(steps,),
        in_specs=flat("in_specs"),
        out_specs=flat("out_specs"),
        out_shape=flat("out_shape"),
        scratch_shapes=flat("scratch"),
        compiler_params=_params(("arbitrary",)),
        name="token_mixers",
    )(*flat("inputs"))
    lse_w = HEADS * LSE_LANES
    outs = [res[2 * g].reshape(batch, RES, per, GROUP_WIDTH) for g in range(N_GROUPS)]
    lses = [res[2 * g + 1].reshape(batch, RES, per, lse_w) for g in range(N_GROUPS)]
    return outs, lses, res[2 * N_GROUPS]


def _merge_kernel(o1_ref, o2_ref, o3_ref, l1_ref, l2_ref, l3_ref, r_ref,
                  ga0_ref, ga1_ref, gr0_ref, gr1_ref, x_ref,
                  wpa_ref, wpr_ref, wo_ref, gpost_ref, gffn_ref,
                  h_ref, a2_ref, attn_s):
    tm = x_ref.shape[0]
    per = tm // RES
    flat = lambda ref: ref[...].reshape(tm, ref.shape[-1])

    la, lb, lc = flat(l1_ref), flat(l2_ref), flat(l3_ref)
    lm = jnp.maximum(la, jnp.maximum(lb, lc))
    ea, eb, ec = jnp.exp(la - lm), jnp.exp(lb - lm), jnp.exp(lc - lm)
    den = ea + eb + ec
    wa, wb, wc = ea / den, eb / den, ec / den
    o1, o2, o3 = flat(o1_ref), flat(o2_ref), flat(o3_ref)
    for h in range(HEADS):
        sl = slice(h * HEAD_DIM, (h + 1) * HEAD_DIM)
        c0 = h * LSE_LANES
        mixed = (wa[:, c0:c0 + 1] * o1[:, sl].astype(F32)
                 + wb[:, c0:c0 + 1] * o2[:, sl].astype(F32)
                 + wc[:, c0:c0 + 1] * o3[:, sl].astype(F32))
        attn_s[:, sl] = mixed.astype(BF16)

    tok = lax.broadcasted_iota(jnp.int32, (tm, tm), 0)
    src = _block_position(lax.broadcasted_iota(jnp.int32, (tm, tm), 1), per, tm)
    unpermute = jnp.where(src == tok, 1.0, 0.0).astype(BF16)

    attn = attn_s[...]
    r = flat(r_ref)
    half_w = GROUP_WIDTH
    halves = [slice(half * half_w, (half + 1) * half_w) for half in range(2)]
    branch = [(jnp.dot(attn, wpa_ref[:, cs], preferred_element_type=F32),
               jnp.dot(r, wpr_ref[:, cs], preferred_element_type=F32)) for cs in halves]
    z = None
    for cs, (ya, yr), (ga_ref, gr_ref) in zip(halves, branch, ((ga0_ref, gr0_ref), (ga1_ref, gr1_ref))):
        y = _sigmoid(flat(ga_ref).astype(F32)) * ya + _sigmoid(flat(gr_ref).astype(F32)) * yr
        y_nat = jnp.dot(unpermute, y.astype(BF16), preferred_element_type=F32).astype(BF16)
        part = jnp.dot(y_nat, wo_ref[cs, :], preferred_element_type=F32)
        z = part if z is None else z + part
    h1 = x_ref[...] + z * _rms_scale(z) * gpost_ref[...]
    h_ref[...] = h1
    a2_ref[...] = (h1 * _rms_scale(h1) * gffn_ref[...]).astype(BF16)


def _merge(outs, lses, r_out, proj, x2, wpa, wpr, wo, gpost, gffn, *, gate_col_block, tm=256):
    t, d = x2.shape
    batch, _, per, _ = proj.shape
    rows = tm // RES
    tiles_per_seq = per // rows
    const = lambda i: (0, 0)
    nat = pl.BlockSpec((tm, d), lambda i: (i, 0))

    def lay(width, col_block=0):
        return pl.BlockSpec((None, RES, rows, width),
                            lambda i: (i // tiles_per_seq, 0, i % tiles_per_seq, col_block))

    def resident(shape):
        return pl.BlockSpec(shape, const, pipeline_mode=pl.Buffered(1))

    lse_w = HEADS * LSE_LANES
    return pl.pallas_call(
        _merge_kernel,
        grid=(t // tm,),
        in_specs=[lay(GROUP_WIDTH)] * 3 + [lay(lse_w)] * 3 + [lay(GROUP_WIDTH)]
        + [lay(GROUP_WIDTH, gate_col_block + k) for k in range(4)]
        + [nat, resident(wpa.shape), resident(wpr.shape), resident(wo.shape),
           resident(gpost.shape), resident(gffn.shape)],
        out_specs=[nat, nat],
        out_shape=[jax.ShapeDtypeStruct((t, d), F32), jax.ShapeDtypeStruct((t, d), BF16)],
        scratch_shapes=[pltpu.VMEM((tm, GROUP_WIDTH), BF16)],
        compiler_params=_params(("parallel",)),
        name="merge_out_proj",
    )(*outs, *lses, r_out, proj, proj, proj, proj, x2, wpa, wpr, wo, gpost, gffn)


def _ffn_kernel(a_ref, wg_ref, wu_ref, wd_ref, h_ref, gain_ref, o_ref, acc_ref):
    j = pl.program_id(1)

    @pl.when(j == 0)
    def _():
        acc_ref[...] = jnp.zeros(acc_ref.shape, F32)

    a = a_ref[...]
    gt = jnp.dot(a, wg_ref[...], preferred_element_type=F32)
    up = jnp.dot(a, wu_ref[...], preferred_element_type=F32)
    hidden = (_silu(gt) * up).astype(BF16)
    acc_ref[...] += jnp.dot(hidden, wd_ref[...], preferred_element_type=F32)

    @pl.when(j == pl.num_programs(1) - 1)
    def _():
        ff = acc_ref[...]
        o_ref[...] = h_ref[...] + ff * _rms_scale(ff) * gain_ref[...]


def _ffn(a2, w_gate_up, w_down, h1, gain, *, tm=512, tf=512):
    t, d = a2.shape
    d_ff = w_down.shape[0]
    nf = d_ff // tf
    rows = pl.BlockSpec((tm, d), lambda i, j: (i, 0))
    return pl.pallas_call(
        _ffn_kernel,
        grid=(t // tm, nf),
        in_specs=[
            rows,
            pl.BlockSpec((d, tf), lambda i, j: (0, j)),
            pl.BlockSpec((d, tf), lambda i, j: (0, nf + j)),
            pl.BlockSpec((tf, d), lambda i, j: (j, 0)),
            rows,
            pl.BlockSpec((1, d), lambda i, j: (0, 0)),
        ],
        out_specs=rows,
        out_shape=jax.ShapeDtypeStruct((t, d), F32),
        scratch_shapes=[pltpu.VMEM((tm, d), F32)],
        compiler_params=_params(("parallel", "arbitrary")),
        name="swiglu_ffn",
    )(a2, w_gate_up, w_gate_up, w_down, h1, gain)


def _rotary_tables(seq):
    inv_freq = ROPE_THETA ** (-jnp.arange(0, HEAD_DIM, 2, dtype=F32) / HEAD_DIM)
    ang = jnp.arange(seq, dtype=F32)[:, None] * inv_freq[None, :]
    cos = jnp.concatenate([jnp.cos(ang), jnp.cos(ang)], axis=-1)
    sin_signed = jnp.concatenate([-jnp.sin(ang), jnp.sin(ang)], axis=-1)
    lay = lambda tab: tab.reshape(seq // RES, RES, HEAD_DIM).transpose(1, 0, 2)
    return lay(cos), lay(sin_signed)


def kernel(x, w_in, w_attn_branch, w_hgrn_branch, w_mix_out, hgrn_lower_bounds, hgrn_norm_gain,
           norm_mix_pre, norm_mix_post, w_ffn_gate_up, w_ffn_down, norm_ffn_pre, norm_ffn_post):
    batch, seq, d = x.shape
    assert w_in.shape[0] == 1, "single-layer block"
    x2 = x.reshape(batch * seq, d)
    row = lambda v: v[0].reshape(1, -1).astype(F32)

    proj = _in_proj(x2, row(norm_mix_pre), w_in[0].astype(BF16), batch, seq)

    for window, dilation in ATTN_GROUPS:
        assert window // dilation == ATTN_BACK and seq % window == 0 and RES % dilation == 0
    cos_r, sin_r = _rotary_tables(seq)
    qkv_blocks = 3 * N_GROUPS
    outs, lses, r_out = _mixers(proj, cos_r, sin_r, hgrn_lower_bounds.astype(F32), row(hgrn_norm_gain),
                                hgrn_col_block=qkv_blocks)

    h1, a2 = _merge(outs, lses, r_out, proj, x2,
                    w_attn_branch[0].astype(BF16), w_hgrn_branch[0].astype(BF16),
                    w_mix_out[0].astype(BF16), row(norm_mix_post), row(norm_ffn_pre),
                    gate_col_block=qkv_blocks + 4)

    out = _ffn(a2, w_ffn_gate_up[0].astype(BF16), w_ffn_down[0].astype(BF16), h1, row(norm_ffn_post))
    return out.reshape(batch, seq, d)
```

```python
import functools
import math

import numpy as np
import jax
import jax.numpy as jnp
from jax import lax
from jax.experimental import pallas as pl
from jax.experimental.pallas import tpu as pltpu

F32 = jnp.float32
BF16 = jnp.bfloat16

HEAD_DIM = 128
ATTN_GROUPS = ((128, 1), (512, 4), (2048, 16))
N_GROUPS = len(ATTN_GROUPS)
HEADS = 8
GROUP_WIDTH = HEADS * HEAD_DIM
ROTARY_BLOCKS = 2 * N_GROUPS
QKV_BLOCKS = 3 * N_GROUPS
FORGET_BLOCK = QKV_BLOCKS + 1
ATTN_BACK = 128
ATTN_SUB_BLOCKS = 2
RES = 16
ROPE_THETA = 10000.0
NORM_EPS = 1e-6
HGRN_CHUNK = 128
HGRN_LEVELS = (64, 32, 16, 8, 4, 2, 1)
LSE_LANES = 16
BF16_SUBLANES = 16
NEG_BIG = -1e30
PERMUTE_GROUP = 256
LOG2E = math.log2(math.e)

VMEM_LIMIT_BYTES = 56 * 1024 * 1024


def _params(semantics):
    return pltpu.CompilerParams(dimension_semantics=semantics, vmem_limit_bytes=VMEM_LIMIT_BYTES)


def _rms_scale(v):
    return lax.rsqrt(jnp.mean(v * v, axis=-1, keepdims=True) + NORM_EPS)


def _sigmoid(v):
    return 0.5 * jnp.tanh(0.5 * v) + 0.5


def _silu(v):
    return v * _sigmoid(v)


def _block_position(i, rows_per_residue, block_rows):
    shift = rows_per_residue.bit_length() - 1
    n_res_shift = (block_rows // rows_per_residue).bit_length() - 1
    return ((i & (rows_per_residue - 1)) << n_res_shift) | lax.shift_right_logical(i, shift)


def _in_proj_kernel(x_ref, gain_ref, w_ref, cos_ref, sin_ref, hlb_ref, o_ref, keep_ref, a_ref):
    tm = x_ref.shape[0]
    per_res = tm // RES
    grp = PERMUTE_GROUP
    per_grp = grp // RES
    j = pl.program_id(1)

    @pl.when(j == 0)
    def _():
        row_token = _block_position(lax.broadcasted_iota(jnp.int32, (grp, grp), 0), per_grp, grp)
        permute = jnp.where(row_token == lax.broadcasted_iota(jnp.int32, (grp, grp), 1), 1.0, 0.0).astype(BF16)
        gain = gain_ref[...]
        for gi in range(tm // grp):
            xg = x_ref[gi * grp:(gi + 1) * grp, :]
            ag = (xg * _rms_scale(xg) * gain).astype(BF16)
            pg = jnp.dot(permute, ag, preferred_element_type=F32).astype(BF16)
            for r in range(RES):
                dst = r * per_res + gi * per_grp
                a_ref[dst:dst + per_grp, :] = pg[r * per_grp:(r + 1) * per_grp, :]

    def project():
        return jnp.dot(a_ref[...], w_ref[...], preferred_element_type=F32)

    def store(ref, val):
        ref[...] = val.reshape(RES, per_res, val.shape[-1]).astype(ref.dtype)

    @pl.when(j < ROTARY_BLOCKS)
    def _():
        res = project()
        scale = jnp.where(j < N_GROUPS, HEAD_DIM ** -0.5 * LOG2E, 1.0)
        cos = cos_ref[...].reshape(tm, HEAD_DIM) * scale
        sin = sin_ref[...].reshape(tm, HEAD_DIM) * scale
        heads = []
        for h in range(HEADS):
            t = res[:, h * HEAD_DIM:(h + 1) * HEAD_DIM]
            heads.append(t * cos + pltpu.roll(t, HEAD_DIM // 2, 1) * sin)
        store(o_ref, jnp.concatenate(heads, axis=1))

    is_swish = (j == FORGET_BLOCK - 1) | (j == FORGET_BLOCK + 2)

    @pl.when(is_swish)
    def _():
        store(o_ref, _silu(project()))

    @pl.when((j >= ROTARY_BLOCKS) & (j != FORGET_BLOCK) & jnp.logical_not(is_swish))
    def _():
        store(o_ref, project())

    @pl.when(j == FORGET_BLOCK)
    def _():
        res = project()
        h0 = hlb_ref[0:1, :]
        h1 = hlb_ref[1:2, :]
        hm = jnp.maximum(h0, h1)
        e0 = jnp.exp(h0 - hm)
        e1 = jnp.exp(h1 - hm)
        lb = e0 / (e0 + e1)
        c1 = 0.5 * (1.0 - lb)
        c0 = 1.0 - c1
        c1_tanh = c1 * jnp.tanh(0.5 * res)
        store(o_ref, jnp.log(c0 + c1_tanh))
        store(keep_ref, c1 - c1_tanh)


def _in_proj(x2, gain, w, cos_r, sin_r, lower_bounds, batch, seq, *, tm=1024, tn=GROUP_WIDTH):
    assert lower_bounds.shape[0] == 2, "forget-gate bound rows: DEPTH + 1 with DEPTH == 1"
    t, d = x2.shape
    n = w.shape[1]
    tiles_per_seq = seq // tm
    table = pl.BlockSpec((RES, tm // RES, HEAD_DIM), lambda i, j: (0, i % tiles_per_seq, 0))
    lay = lambda col: pl.BlockSpec((None, RES, tm // RES, tn),
                                   lambda i, j: (i // tiles_per_seq, 0, i % tiles_per_seq, col(j)))
    return pl.pallas_call(
        _in_proj_kernel,
        grid=(t // tm, n // tn),
        in_specs=[
            pl.BlockSpec((tm, d), lambda i, j: (i, 0)),
            pl.BlockSpec((1, d), lambda i, j: (0, 0)),
            pl.BlockSpec((d, tn), lambda i, j: (0, j)),
            table, table,
            pl.BlockSpec(lower_bounds.shape, lambda i, j: (0, 0)),
        ],
        out_specs=[lay(lambda j: j), lay(lambda j: 0)],
        out_shape=[jax.ShapeDtypeStruct((batch, RES, seq // RES, n), BF16),
                   jax.ShapeDtypeStruct((batch, RES, seq // RES, tn), BF16)],
        scratch_shapes=[pltpu.VMEM((tm, d), BF16)],
        compiler_params=_params(("parallel", "arbitrary")),
        name="in_proj",
    )(x2, gain, w, cos_r, sin_r, lower_bounds)


def _sub_blocks(ref, sl, nsub):
    v = ref[:, :, sl]
    a, u, w = v.shape
    step = u // nsub
    if step % BF16_SUBLANES:
        v = v.astype(F32)
    return [v[:, s * step:(s + 1) * step, :].reshape(a * step, w).astype(ref.dtype) for s in range(nsub)]


def _store_sub_blocks(ref, sl, blocks):
    a = ref.shape[0]
    parts = [b.reshape(a, b.shape[0] // a, b.shape[1]) for b in blocks]
    ref[:, :, sl] = jnp.concatenate(parts, axis=1).astype(ref.dtype)


def _attn_body(n, q_ref, k_ref, v_ref, o_ref, lse_ref, kk_ref, vv_ref, *, rows_per_residue):
    blk = ATTN_BACK
    nsub = ATTN_SUB_BLOCKS

    @pl.when(n == 0)
    def _():
        kk_ref[0:blk, :] = jnp.zeros((blk, GROUP_WIDTH), BF16)
        vv_ref[0:blk, :] = jnp.zeros((blk, GROUP_WIDTH), BF16)

    @pl.when(n > 0)
    def _():
        kk_ref[0:blk, :] = kk_ref[nsub * blk:(nsub + 1) * blk, :]
        vv_ref[0:blk, :] = vv_ref[nsub * blk:(nsub + 1) * blk, :]

    qi = lax.broadcasted_iota(jnp.int32, (blk, 2 * blk), 0)
    kj = lax.broadcasted_iota(jnp.int32, (blk, 2 * blk), 1)
    q_pos = _block_position(qi, rows_per_residue, blk)
    k_pos = _block_position(kj & (blk - 1), rows_per_residue, blk) - jnp.where(kj < blk, blk, 0)
    dist = q_pos - k_pos
    band = (dist >= 0) & (dist <= ATTN_BACK)
    first_mask = band & (kj >= jnp.where(n > 0, 0, blk))
    lane_head = lax.shift_right_logical(lax.broadcasted_iota(jnp.int32, (blk, HEADS * LSE_LANES), 1),
                                        LSE_LANES.bit_length() - 1)
    lse_tiles = [jnp.zeros((blk, HEADS * LSE_LANES), F32) for _ in range(nsub)]

    for h in range(HEADS):
        sl = slice(h * HEAD_DIM, (h + 1) * HEAD_DIM)
        for s, (kb, vb) in enumerate(zip(_sub_blocks(k_ref, sl, nsub), _sub_blocks(v_ref, sl, nsub))):
            rows = slice((s + 1) * blk, (s + 2) * blk)
            kk_ref[rows, sl] = kb
            vv_ref[rows, sl] = vb
        outs = []
        for s, qb in enumerate(_sub_blocks(q_ref, sl, nsub)):
            keys = slice(s * blk, (s + 2) * blk)
            sc = lax.dot_general(qb, kk_ref[keys, sl], (((1,), (1,)), ((), ())), preferred_element_type=F32)
            sc = jnp.where(first_mask if s == 0 else band, sc, NEG_BIG)
            m = jnp.max(sc, axis=-1, keepdims=True)
            p = jnp.exp2(sc - m)
            den = jnp.sum(p, axis=-1, keepdims=True)
            o = jnp.dot(p.astype(BF16), vv_ref[keys, sl], preferred_element_type=F32)
            outs.append(o / den)
            lse = m * math.log(2.0) + jnp.log(den)
            lse_tiles[s] = jnp.where(lane_head == h, lse, lse_tiles[s])
        _store_sub_blocks(o_ref, sl, outs)
    _store_sub_blocks(lse_ref, slice(None), lse_tiles)


def _attn_operands(proj, g, dilation):
    batch, _, per, _ = proj.shape
    a = RES // dilation
    rows = ATTN_BACK // a
    rows_blk = rows * ATTN_SUB_BLOCKS
    nblk = per // rows_blk
    view = lambda arr: arr.reshape(arr.shape[:-3] + (a, dilation) + arr.shape[-2:])

    def spec(width, col_block):
        return pl.BlockSpec((None, a, None, rows_blk, width),
                            lambda i: (i // (dilation * nblk), 0, (i // nblk) % dilation, i % nblk, col_block))

    lse_w = HEADS * LSE_LANES
    proj_v = view(proj)
    key_rows = (ATTN_SUB_BLOCKS + 1) * ATTN_BACK
    return dict(
        inputs=[proj_v, proj_v, proj_v],
        in_specs=[spec(GROUP_WIDTH, which * N_GROUPS + g) for which in range(3)],
        out_specs=[spec(GROUP_WIDTH, 0), spec(lse_w, 0)],
        out_shape=[jax.ShapeDtypeStruct((batch, a, dilation, per, GROUP_WIDTH), BF16),
                   jax.ShapeDtypeStruct((batch, a, dilation, per, lse_w), F32)],
        scratch=[pltpu.VMEM((key_rows, GROUP_WIDTH), BF16), pltpu.VMEM((key_rows, GROUP_WIDTH), BF16)],
        rows_per_residue=rows, nblk=nblk)


def _hgrn_cumsum_matrix():
    c = HGRN_CHUNK
    i = np.arange(c)
    per = c // RES
    pos = (i % per) * RES + i // per
    return (pos[None, :] <= pos[:, None]).astype(np.float32)


def _level_reference(b3, half):
    res, per, dk = b3.shape
    if half < res:
        two = 2 * half
        parts = [jnp.broadcast_to(b3[blk * two + half - 1][None], (two, per, dk)) for blk in range(res // two)]
        return parts[0] if len(parts) == 1 else jnp.concatenate(parts, axis=0)
    last = b3[res - 1]
    step = half // res
    u = lax.broadcasted_iota(jnp.int32, (per, dk), 0)
    ref = None
    for blk in reversed(range(per // (2 * step))):
        row = jnp.broadcast_to(last[blk * 2 * step + step - 1][None], (per, dk))
        ref = row if ref is None else jnp.where(u < (blk + 1) * 2 * step, row, ref)
    return jnp.broadcast_to(ref[None], (res, per, dk))


def _scale_valid_rows(t3, e, half, lower):
    res = t3.shape[0]
    if half >= res:
        return t3 * e
    parts = []
    for lo in range(0, res, 2 * half):
        up, dn = slice(lo, lo + half), slice(lo + half, lo + 2 * half)
        parts += [t3[up], t3[dn] * e[dn]] if lower else [t3[up] * e[up], t3[dn]]
    return jnp.concatenate(parts, axis=0)


def _hgrn_body(first, q_ref, logf_ref, v_ref, gate_ref, keep_ref, gain_ref, tri_ref, o_ref, st_ref):
    c = HGRN_CHUNK
    dk = HEAD_DIM
    per = c // RES

    @pl.when(first)
    def _():
        st_ref[...] = jnp.zeros(st_ref.shape, F32)

    g_all = logf_ref[...].astype(F32)
    k_all = keep_ref[...].astype(F32)
    q_all = q_ref[...].astype(F32)
    v_all = v_ref[...].astype(F32)
    gate_all = gate_ref[...].astype(F32)

    ti = _block_position(lax.broadcasted_iota(jnp.int32, (c, c), 0), per, c)
    si = _block_position(lax.broadcasted_iota(jnp.int32, (c, c), 1), per, c)
    diag_mask = ti == si
    level_masks = []
    for half in HGRN_LEVELS:
        shift = half.bit_length()
        level_masks.append((lax.shift_right_logical(ti, shift) == lax.shift_right_logical(si, shift))
                           & ((ti & half) != 0) & ((si & half) == 0))
    ones_b = jnp.ones((dk, c), BF16)
    tri = tri_ref[...]

    n_chunks = o_ref.shape[1] // per
    outs = [[None] * n_chunks for _ in range(HEADS)]
    for ci in range(n_chunks):
        rows = slice(ci * per, (ci + 1) * per)
        g = g_all[:, rows, :].reshape(c, GROUP_WIDTH)
        g_hi = g.astype(BF16)
        r1 = g - g_hi.astype(F32)
        g_mid = r1.astype(BF16)
        g_lo = (r1 - g_mid.astype(F32)).astype(BF16)
        cum = (jnp.dot(tri, g_hi, preferred_element_type=F32)
               + jnp.dot(tri, g_mid, preferred_element_type=F32)
               + jnp.dot(tri, g_lo, preferred_element_type=F32))

        for h in range(HEADS):
            sl = slice(h * dk, (h + 1) * dk)
            q3 = q_all[:, rows, sl]
            k3 = k_all[:, rows, sl]
            q = q3.reshape(c, dk)
            k = k3.reshape(c, dk)
            v = v_all[:, rows, sl].reshape(c, dk).astype(BF16)
            b = cum[:, sl]
            b3 = b.reshape(RES, per, dk)
            b_last = b[c - 1:c, :]
            st = st_ref[h]

            q_dec = (q * jnp.exp(b)).astype(BF16)
            o = lax.dot_general(q_dec, st.astype(BF16), (((1,), (1,)), ((), ())),
                                preferred_element_type=F32)

            qk = (q * k).astype(BF16)
            a = jnp.where(diag_mask, jnp.dot(qk, ones_b, preferred_element_type=F32), 0.0)
            for li, half in enumerate(HGRN_LEVELS):
                e = jnp.exp2(jnp.abs(b3 - _level_reference(b3, half)) * -LOG2E)
                q_l = _scale_valid_rows(q3, e, half, lower=True).reshape(c, dk).astype(BF16)
                k_l = _scale_valid_rows(k3, e, half, lower=False).reshape(c, dk).astype(BF16)
                p_l = lax.dot_general(q_l, k_l, (((1,), (1,)), ((), ())), preferred_element_type=F32)
                a = jnp.where(level_masks[li], p_l, a)
            o = o + jnp.dot(a.astype(BF16), v, preferred_element_type=F32)

            k_dec = (k * jnp.exp(b_last - b)).astype(BF16)
            st_ref[h] = st * jnp.exp(b_last) + lax.dot_general(
                v, k_dec, (((0,), (0,)), ((), ())), preferred_element_type=F32)

            r = o * _rms_scale(o) * gain_ref[:, sl] * gate_all[:, rows, sl].reshape(c, dk)
            outs[h][ci] = r.reshape(RES, per, dk)

    for h in range(HEADS):
        o_ref[:, :, h * dk:(h + 1) * dk] = jnp.concatenate(outs[h], axis=1).astype(o_ref.dtype)


def _hgrn_operands(proj, keep, gain, *, first_col_block, rows):
    batch, _, per, _ = proj.shape
    tiles = per // rows
    tri = jnp.asarray(_hgrn_cumsum_matrix(), BF16)

    def col_spec(col_block):
        return pl.BlockSpec((None, RES, rows, GROUP_WIDTH), lambda i: (i // tiles, 0, i % tiles, col_block))

    const = lambda i: (0, 0)
    return dict(
        inputs=[proj, proj, proj, proj, keep, gain, tri],
        in_specs=[col_spec(first_col_block + k) for k in range(4)]
        + [col_spec(0), pl.BlockSpec(gain.shape, const), pl.BlockSpec(tri.shape, const)],
        out_specs=[col_spec(0)],
        out_shape=[jax.ShapeDtypeStruct((batch, RES, per, GROUP_WIDTH), BF16)],
        scratch=[pltpu.VMEM((HEADS, HEAD_DIM, HEAD_DIM), F32)],
        tiles=tiles)


def _mixers(proj, keep, gain, *, hgrn_col_block):
    batch, _, per, _ = proj.shape
    rows = ATTN_SUB_BLOCKS * ATTN_BACK // RES
    steps = batch * per // rows
    parts = [_attn_operands(proj, g, dilation) for g, (_, dilation) in enumerate(ATTN_GROUPS)]
    parts.append(_hgrn_operands(proj, keep, gain, first_col_block=hgrn_col_block, rows=rows))
    n_in = [len(p["inputs"]) for p in parts]
    n_out = [len(p["out_shape"]) for p in parts]
    n_scr = [len(p["scratch"]) for p in parts]

    def kernel_body(*refs):
        ins, rest = refs[:sum(n_in)], refs[sum(n_in):]
        outs, scr = rest[:sum(n_out)], rest[sum(n_out):]
        i = pl.program_id(0)
        take = lambda seq, counts, k: seq[sum(counts[:k]):sum(counts[:k + 1])]
        for k, part in enumerate(parts[:-1]):
            _attn_body(i % part["nblk"], *take(ins, n_in, k), *take(outs, n_out, k), *take(scr, n_scr, k),
                       rows_per_residue=part["rows_per_residue"])
        k = len(parts) - 1
        _hgrn_body(i % parts[k]["tiles"] == 0, *take(ins, n_in, k), *take(outs, n_out, k), *take(scr, n_scr, k))

    flat = lambda key: [v for p in parts for v in p[key]]
    res = pl.pallas_call(
        kernel_body,
        grid=(steps,),
        in_specs=flat("in_specs"),
        out_specs=flat("out_specs"),
        out_shape=flat("out_shape"),
        scratch_shapes=flat("scratch"),
        compiler_params=_params(("arbitrary",)),
        name="token_mixers",
    )(*flat("inputs"))
    lse_w = HEADS * LSE_LANES
    outs = [res[2 * g].reshape(batch, RES, per, GROUP_WIDTH) for g in range(N_GROUPS)]
    lses = [res[2 * g + 1].reshape(batch, RES, per, lse_w) for g in range(N_GROUPS)]
    return outs, lses, res[2 * N_GROUPS]


def _merge_kernel(o1_ref, o2_ref, o3_ref, l1_ref, l2_ref, l3_ref, r_ref,
                  ga0_ref, ga1_ref, gr0_ref, gr1_ref, x_ref,
                  wpa_ref, wpr_ref, wo_ref, gpost_ref, gffn_ref,
                  h_ref, a2_ref, attn_s):
    tm = x_ref.shape[0]
    per = tm // RES
    flat = lambda ref: ref[...].reshape(tm, ref.shape[-1])

    la, lb, lc = flat(l1_ref), flat(l2_ref), flat(l3_ref)
    lm = jnp.maximum(la, jnp.maximum(lb, lc))
    ea, eb, ec = jnp.exp(la - lm), jnp.exp(lb - lm), jnp.exp(lc - lm)
    den = ea + eb + ec
    wa, wb, wc = ea / den, eb / den, ec / den
    o1, o2, o3 = flat(o1_ref), flat(o2_ref), flat(o3_ref)
    for h in range(HEADS):
        sl = slice(h * HEAD_DIM, (h + 1) * HEAD_DIM)
        c0 = h * LSE_LANES
        mixed = (wa[:, c0:c0 + 1] * o1[:, sl].astype(F32)
                 + wb[:, c0:c0 + 1] * o2[:, sl].astype(F32)
                 + wc[:, c0:c0 + 1] * o3[:, sl].astype(F32))
        attn_s[:, sl] = mixed.astype(BF16)

    tok = lax.broadcasted_iota(jnp.int32, (tm, tm), 0)
    src = _block_position(lax.broadcasted_iota(jnp.int32, (tm, tm), 1), per, tm)
    unpermute = jnp.where(src == tok, 1.0, 0.0).astype(BF16)

    attn = attn_s[...]
    r = flat(r_ref)
    half_w = GROUP_WIDTH
    halves = [slice(half * half_w, (half + 1) * half_w) for half in range(2)]
    branch = [(jnp.dot(attn, wpa_ref[:, cs], preferred_element_type=F32),
               jnp.dot(r, wpr_ref[:, cs], preferred_element_type=F32)) for cs in halves]
    z = None
    for cs, (ya, yr), (ga_ref, gr_ref) in zip(halves, branch, ((ga0_ref, gr0_ref), (ga1_ref, gr1_ref))):
        y = _sigmoid(flat(ga_ref).astype(F32)) * ya + _sigmoid(flat(gr_ref).astype(F32)) * yr
        y_nat = jnp.dot(unpermute, y.astype(BF16), preferred_element_type=F32).astype(BF16)
        part = jnp.dot(y_nat, wo_ref[cs, :], preferred_element_type=F32)
        z = part if z is None else z + part
    h1 = x_ref[...] + z * _rms_scale(z) * gpost_ref[...]
    h_ref[...] = h1
    a2_ref[...] = (h1 * _rms_scale(h1) * gffn_ref[...]).astype(BF16)


def _merge(outs, lses, r_out, proj, x2, wpa, wpr, wo, gpost, gffn, *, gate_col_block, tm=256):
    t, d = x2.shape
    batch, _, per, _ = proj.shape
    rows = tm // RES
    tiles_per_seq = per // rows
    const = lambda i: (0, 0)
    nat = pl.BlockSpec((tm, d), lambda i: (i, 0))

    def lay(width, col_block=0):
        return pl.BlockSpec((None, RES, rows, width),
                            lambda i: (i // tiles_per_seq, 0, i % tiles_per_seq, col_block))

    def resident(shape):
        return pl.BlockSpec(shape, const, pipeline_mode=pl.Buffered(1))

    lse_w = HEADS * LSE_LANES
    return pl.pallas_call(
        _merge_kernel,
        grid=(t // tm,),
        in_specs=[lay(GROUP_WIDTH)] * 3 + [lay(lse_w)] * 3 + [lay(GROUP_WIDTH)]
        + [lay(GROUP_WIDTH, gate_col_block + k) for k in range(4)]
        + [nat, resident(wpa.shape), resident(wpr.shape), resident(wo.shape),
           resident(gpost.shape), resident(gffn.shape)],
        out_specs=[nat, nat],
        out_shape=[jax.ShapeDtypeStruct((t, d), F32), jax.ShapeDtypeStruct((t, d), BF16)],
        scratch_shapes=[pltpu.VMEM((tm, GROUP_WIDTH), BF16)],
        compiler_params=_params(("parallel",)),
        name="merge_out_proj",
    )(*outs, *lses, r_out, proj, proj, proj, proj, x2, wpa, wpr, wo, gpost, gffn)


def _ffn_kernel(a_ref, wg_ref, wu_ref, wd_ref, h_ref, gain_ref, o_ref, acc_ref):
    j = pl.program_id(1)

    @pl.when(j == 0)
    def _():
        acc_ref[...] = jnp.zeros(acc_ref.shape, F32)

    a = a_ref[...]
    gt = jnp.dot(a, wg_ref[...], preferred_element_type=F32)
    up = jnp.dot(a, wu_ref[...], preferred_element_type=F32)
    hidden = (_silu(gt) * up).astype(BF16)
    acc_ref[...] += jnp.dot(hidden, wd_ref[...], preferred_element_type=F32)

    @pl.when(j == pl.num_programs(1) - 1)
    def _():
        ff = acc_ref[...]
        o_ref[...] = h_ref[...] + ff * _rms_scale(ff) * gain_ref[...]


def _ffn(a2, w_gate_up, w_down, h1, gain, *, tm=512, tf=512):
    t, d = a2.shape
    d_ff = w_down.shape[0]
    nf = d_ff // tf
    rows = pl.BlockSpec((tm, d), lambda i, j: (i, 0))
    return pl.pallas_call(
        _ffn_kernel,
        grid=(t // tm, nf),
        in_specs=[
            rows,
            pl.BlockSpec((d, tf), lambda i, j: (0, j)),
            pl.BlockSpec((d, tf), lambda i, j: (0, nf + j)),
            pl.BlockSpec((tf, d), lambda i, j: (j, 0)),
            rows,
            pl.BlockSpec((1, d), lambda i, j: (0, 0)),
        ],
        out_specs=rows,
        out_shape=jax.ShapeDtypeStruct((t, d), F32),
        scratch_shapes=[pltpu.VMEM((tm, d), F32)],
        compiler_params=_params(("parallel", "arbitrary")),
        name="swiglu_ffn",
    )(a2, w_gate_up, w_gate_up, w_down, h1, gain)


def _rotary_tables(seq):
    inv_freq = ROPE_THETA ** (-jnp.arange(0, HEAD_DIM, 2, dtype=F32) / HEAD_DIM)
    ang = jnp.arange(seq, dtype=F32)[:, None] * inv_freq[None, :]
    cos = jnp.concatenate([jnp.cos(ang), jnp.cos(ang)], axis=-1)
    sin_signed = jnp.concatenate([-jnp.sin(ang), jnp.sin(ang)], axis=-1)
    lay = lambda tab: tab.reshape(seq // RES, RES, HEAD_DIM).transpose(1, 0, 2)
    return lay(cos), lay(sin_signed)


def kernel(x, w_in, w_attn_branch, w_hgrn_branch, w_mix_out, hgrn_lower_bounds, hgrn_norm_gain,
           norm_mix_pre, norm_mix_post, w_ffn_gate_up, w_ffn_down, norm_ffn_pre, norm_ffn_post):
    batch, seq, d = x.shape
    assert w_in.shape[0] == 1, "single-layer block"
    x2 = x.reshape(batch * seq, d)
    row = lambda v: v[0].reshape(1, -1).astype(F32)

    for window, dilation in ATTN_GROUPS:
        assert window // dilation == ATTN_BACK and seq % window == 0 and RES % dilation == 0
    assert w_in.shape[2] == (QKV_BLOCKS + 8) * GROUP_WIDTH
    cos_r, sin_r = _rotary_tables(seq)
    proj, keep = _in_proj(x2, row(norm_mix_pre), w_in[0].astype(BF16), cos_r, sin_r,
                          hgrn_lower_bounds.astype(F32), batch, seq)

    outs, lses, r_out = _mixers(proj, keep, row(hgrn_norm_gain), hgrn_col_block=QKV_BLOCKS)

    h1, a2 = _merge(outs, lses, r_out, proj, x2,
                    w_attn_branch[0].astype(BF16), w_hgrn_branch[0].astype(BF16),
                    w_mix_out[0].astype(BF16), row(norm_mix_post), row(norm_ffn_pre),
                    gate_col_block=QKV_BLOCKS + 4)

    out = _ffn(a2, w_ffn_gate_up[0].astype(BF16), w_ffn_down[0].astype(BF16), h1, row(norm_ffn_post))
    return out.reshape(batch, seq, d)
```

```python
import functools
import math

import numpy as np
import jax
import jax.numpy as jnp
from jax import lax
from jax.experimental import pallas as pl
from jax.experimental.pallas import tpu as pltpu

F32 = jnp.float32
BF16 = jnp.bfloat16

HEAD_DIM = 128
ATTN_GROUPS = ((128, 1), (512, 4), (2048, 16))
N_GROUPS = len(ATTN_GROUPS)
HEADS = 8
GROUP_WIDTH = HEADS * HEAD_DIM
ROTARY_BLOCKS = 2 * N_GROUPS
QKV_BLOCKS = 3 * N_GROUPS
FORGET_BLOCK = QKV_BLOCKS + 1
ATTN_BACK = 128
ATTN_SUB_BLOCKS = 2
RES = 16
ROPE_THETA = 10000.0
NORM_EPS = 1e-6
HGRN_CHUNK = 128
HGRN_LEVELS = (64, 32, 16, 8, 4, 2, 1)
LSE_LANES = 16
BF16_SUBLANES = 16
NEG_BIG = -1e30
PERMUTE_GROUP = 256
LOG2E = math.log2(math.e)

VMEM_LIMIT_BYTES = 56 * 1024 * 1024


def _params(semantics):
    return pltpu.CompilerParams(dimension_semantics=semantics, vmem_limit_bytes=VMEM_LIMIT_BYTES)


def _rms_scale(v):
    return lax.rsqrt(jnp.mean(v * v, axis=-1, keepdims=True) + NORM_EPS)


def _sigmoid(v):
    return 0.5 * jnp.tanh(0.5 * v) + 0.5


def _silu(v):
    return v * _sigmoid(v)


def _block_position(i, rows_per_residue, block_rows):
    shift = rows_per_residue.bit_length() - 1
    n_res_shift = (block_rows // rows_per_residue).bit_length() - 1
    return ((i & (rows_per_residue - 1)) << n_res_shift) | lax.shift_right_logical(i, shift)


def _in_proj_kernel(x_ref, gain_ref, w_ref, cos_ref, sin_ref, hlb_ref, o_ref, keep_ref, a_ref):
    tm = x_ref.shape[0]
    per_res = tm // RES
    grp = PERMUTE_GROUP
    per_grp = grp // RES
    j = pl.program_id(1)

    @pl.when(j == 0)
    def _():
        row_token = _block_position(lax.broadcasted_iota(jnp.int32, (grp, grp), 0), per_grp, grp)
        permute = jnp.where(row_token == lax.broadcasted_iota(jnp.int32, (grp, grp), 1), 1.0, 0.0).astype(BF16)
        gain = gain_ref[...]
        for gi in range(tm // grp):
            xg = x_ref[gi * grp:(gi + 1) * grp, :]
            ag = (xg * _rms_scale(xg) * gain).astype(BF16)
            pg = jnp.dot(permute, ag, preferred_element_type=F32).astype(BF16)
            for r in range(RES):
                dst = r * per_res + gi * per_grp
                a_ref[dst:dst + per_grp, :] = pg[r * per_grp:(r + 1) * per_grp, :]

    def project():
        return jnp.dot(a_ref[...], w_ref[...], preferred_element_type=F32)

    def store(ref, val):
        ref[...] = val.reshape(RES, per_res, val.shape[-1]).astype(ref.dtype)

    @pl.when(j < ROTARY_BLOCKS)
    def _():
        res = project()
        scale = jnp.where(j < N_GROUPS, HEAD_DIM ** -0.5 * LOG2E, 1.0)
        cos = cos_ref[...].reshape(tm, HEAD_DIM) * scale
        sin = sin_ref[...].reshape(tm, HEAD_DIM) * scale
        heads = []
        for h in range(HEADS):
            t = res[:, h * HEAD_DIM:(h + 1) * HEAD_DIM]
            heads.append(t * cos + pltpu.roll(t, HEAD_DIM // 2, 1) * sin)
        store(o_ref, jnp.concatenate(heads, axis=1))

    is_swish = (j == FORGET_BLOCK - 1) | (j == FORGET_BLOCK + 2)

    @pl.when(is_swish)
    def _():
        store(o_ref, _silu(project()))

    @pl.when((j >= ROTARY_BLOCKS) & (j != FORGET_BLOCK) & jnp.logical_not(is_swish))
    def _():
        store(o_ref, project())

    @pl.when(j == FORGET_BLOCK)
    def _():
        res = project()
        h0 = hlb_ref[0:1, :]
        h1 = hlb_ref[1:2, :]
        hm = jnp.maximum(h0, h1)
        e0 = jnp.exp(h0 - hm)
        e1 = jnp.exp(h1 - hm)
        lb = e0 / (e0 + e1)
        c1 = 0.5 * (1.0 - lb)
        c0 = 1.0 - c1
        c1_tanh = c1 * jnp.tanh(0.5 * res)
        store(o_ref, jnp.log(c0 + c1_tanh))
        store(keep_ref, c1 - c1_tanh)


def _in_proj(x2, gain, w, cos_r, sin_r, lower_bounds, batch, seq, *, tm=1024, tn=GROUP_WIDTH):
    assert lower_bounds.shape[0] == 2, "forget-gate bound rows: DEPTH + 1 with DEPTH == 1"
    t, d = x2.shape
    n = w.shape[1]
    tiles_per_seq = seq // tm
    table = pl.BlockSpec((RES, tm // RES, HEAD_DIM), lambda i, j: (0, i % tiles_per_seq, 0))
    lay = lambda col: pl.BlockSpec((None, RES, tm // RES, tn),
                                   lambda i, j: (i // tiles_per_seq, 0, i % tiles_per_seq, col(j)))
    return pl.pallas_call(
        _in_proj_kernel,
        grid=(t // tm, n // tn),
        in_specs=[
            pl.BlockSpec((tm, d), lambda i, j: (i, 0)),
            pl.BlockSpec((1, d), lambda i, j: (0, 0)),
            pl.BlockSpec((d, tn), lambda i, j: (0, j)),
            table, table,
            pl.BlockSpec(lower_bounds.shape, lambda i, j: (0, 0)),
        ],
        out_specs=[lay(lambda j: j), lay(lambda j: 0)],
        out_shape=[jax.ShapeDtypeStruct((batch, RES, seq // RES, n), BF16),
                   jax.ShapeDtypeStruct((batch, RES, seq // RES, tn), BF16)],
        scratch_shapes=[pltpu.VMEM((tm, d), BF16)],
        compiler_params=_params(("parallel", "arbitrary")),
        name="in_proj",
    )(x2, gain, w, cos_r, sin_r, lower_bounds)


def _sub_blocks(ref, sl, nsub):
    v = ref[:, :, sl]
    a, u, w = v.shape
    step = u // nsub
    if step % BF16_SUBLANES:
        v = v.astype(F32)
    return [v[:, s * step:(s + 1) * step, :].reshape(a * step, w).astype(ref.dtype) for s in range(nsub)]


def _store_sub_blocks(ref, sl, blocks):
    a = ref.shape[0]
    parts = [b.reshape(a, b.shape[0] // a, b.shape[1]) for b in blocks]
    ref[:, :, sl] = jnp.concatenate(parts, axis=1).astype(ref.dtype)


def _attn_body(n, q_ref, k_ref, v_ref, o_ref, lse_ref, kk_ref, vv_ref, *, rows_per_residue):
    blk = ATTN_BACK
    nsub = ATTN_SUB_BLOCKS

    @pl.when(n == 0)
    def _():
        kk_ref[0:blk, :] = jnp.zeros((blk, GROUP_WIDTH), BF16)
        vv_ref[0:blk, :] = jnp.zeros((blk, GROUP_WIDTH), BF16)

    @pl.when(n > 0)
    def _():
        kk_ref[0:blk, :] = kk_ref[nsub * blk:(nsub + 1) * blk, :]
        vv_ref[0:blk, :] = vv_ref[nsub * blk:(nsub + 1) * blk, :]

    qi = lax.broadcasted_iota(jnp.int32, (blk, 2 * blk), 0)
    kj = lax.broadcasted_iota(jnp.int32, (blk, 2 * blk), 1)
    q_pos = _block_position(qi, rows_per_residue, blk)
    k_pos = _block_position(kj & (blk - 1), rows_per_residue, blk) - jnp.where(kj < blk, blk, 0)
    dist = q_pos - k_pos
    band = (dist >= 0) & (dist <= ATTN_BACK)
    first_mask = band & (kj >= jnp.where(n > 0, 0, blk))
    lane_head = lax.shift_right_logical(lax.broadcasted_iota(jnp.int32, (blk, HEADS * LSE_LANES), 1),
                                        LSE_LANES.bit_length() - 1)
    lse_tiles = [jnp.zeros((blk, HEADS * LSE_LANES), F32) for _ in range(nsub)]

    for h in range(HEADS):
        sl = slice(h * HEAD_DIM, (h + 1) * HEAD_DIM)
        for s, (kb, vb) in enumerate(zip(_sub_blocks(k_ref, sl, nsub), _sub_blocks(v_ref, sl, nsub))):
            rows = slice((s + 1) * blk, (s + 2) * blk)
            kk_ref[rows, sl] = kb
            vv_ref[rows, sl] = vb
        outs = []
        for s, qb in enumerate(_sub_blocks(q_ref, sl, nsub)):
            keys = slice(s * blk, (s + 2) * blk)
            sc = lax.dot_general(qb, kk_ref[keys, sl], (((1,), (1,)), ((), ())), preferred_element_type=F32)
            sc = jnp.where(first_mask if s == 0 else band, sc, NEG_BIG)
            m = jnp.max(sc, axis=-1, keepdims=True)
            p = jnp.exp2(sc - m)
            den = jnp.sum(p, axis=-1, keepdims=True)
            o = jnp.dot(p.astype(BF16), vv_ref[keys, sl], preferred_element_type=F32)
            outs.append(o / den)
            lse = m * math.log(2.0) + jnp.log(den)
            lse_tiles[s] = jnp.where(lane_head == h, lse, lse_tiles[s])
        _store_sub_blocks(o_ref, sl, outs)
    _store_sub_blocks(lse_ref, slice(None), lse_tiles)


def _attn_operands(proj, g, dilation):
    batch, _, per, _ = proj.shape
    a = RES // dilation
    rows = ATTN_BACK // a
    rows_blk = rows * ATTN_SUB_BLOCKS
    nblk = per // rows_blk
    view = lambda arr: arr.reshape(arr.shape[:-3] + (a, dilation) + arr.shape[-2:])

    def spec(width, col_block):
        return pl.BlockSpec((None, a, None, rows_blk, width),
                            lambda i: (i // (dilation * nblk), 0, (i // nblk) % dilation, i % nblk, col_block))

    lse_w = HEADS * LSE_LANES
    proj_v = view(proj)
    key_rows = (ATTN_SUB_BLOCKS + 1) * ATTN_BACK
    return dict(
        inputs=[proj_v, proj_v, proj_v],
        in_specs=[spec(GROUP_WIDTH, which * N_GROUPS + g) for which in range(3)],
        out_specs=[spec(GROUP_WIDTH, 0), spec(lse_w, 0)],
        out_shape=[jax.ShapeDtypeStruct((batch, a, dilation, per, GROUP_WIDTH), BF16),
                   jax.ShapeDtypeStruct((batch, a, dilation, per, lse_w), F32)],
        scratch=[pltpu.VMEM((key_rows, GROUP_WIDTH), BF16), pltpu.VMEM((key_rows, GROUP_WIDTH), BF16)],
        rows_per_residue=rows, nblk=nblk)


def _hgrn_cumsum_matrix():
    c = HGRN_CHUNK
    i = np.arange(c)
    per = c // RES
    pos = (i % per) * RES + i // per
    return (pos[None, :] <= pos[:, None]).astype(np.float32)


def _level_reference(b3, half):
    res, per, dk = b3.shape
    if half < res:
        two = 2 * half
        parts = [jnp.broadcast_to(b3[blk * two + half - 1][None], (two, per, dk)) for blk in range(res // two)]
        return parts[0] if len(parts) == 1 else jnp.concatenate(parts, axis=0)
    last = b3[res - 1]
    step = half // res
    u = lax.broadcasted_iota(jnp.int32, (per, dk), 0)
    ref = None
    for blk in reversed(range(per // (2 * step))):
        row = jnp.broadcast_to(last[blk * 2 * step + step - 1][None], (per, dk))
        ref = row if ref is None else jnp.where(u < (blk + 1) * 2 * step, row, ref)
    return jnp.broadcast_to(ref[None], (res, per, dk))


def _scale_valid_rows(t3, e, half, lower):
    res = t3.shape[0]
    if half >= res:
        return t3 * e
    parts = []
    for lo in range(0, res, 2 * half):
        up, dn = slice(lo, lo + half), slice(lo + half, lo + 2 * half)
        parts += [t3[up], t3[dn] * e[dn]] if lower else [t3[up] * e[up], t3[dn]]
    return jnp.concatenate(parts, axis=0)


def _hgrn_body(first, q_ref, logf_ref, v_ref, gate_ref, keep_ref, gain_ref, tri_ref, o_ref, st_ref):
    c = HGRN_CHUNK
    dk = HEAD_DIM
    per = c // RES

    @pl.when(first)
    def _():
        st_ref[...] = jnp.zeros(st_ref.shape, F32)

    g_all = logf_ref[...].astype(F32)
    k_all = keep_ref[...].astype(F32)
    q_all = q_ref[...].astype(F32)
    v_all = v_ref[...].astype(F32)
    gate_all = gate_ref[...].astype(F32)

    ti = _block_position(lax.broadcasted_iota(jnp.int32, (c, c), 0), per, c)
    si = _block_position(lax.broadcasted_iota(jnp.int32, (c, c), 1), per, c)
    diag_mask = ti == si
    level_masks = []
    for half in HGRN_LEVELS:
        shift = half.bit_length()
        level_masks.append((lax.shift_right_logical(ti, shift) == lax.shift_right_logical(si, shift))
                           & ((ti & half) != 0) & ((si & half) == 0))
    tri = tri_ref[...]

    n_chunks = o_ref.shape[1] // per
    outs = [[None] * n_chunks for _ in range(HEADS)]
    for ci in range(n_chunks):
        rows = slice(ci * per, (ci + 1) * per)
        g = g_all[:, rows, :].reshape(c, GROUP_WIDTH)
        g_hi = g.astype(BF16)
        r1 = g - g_hi.astype(F32)
        g_mid = r1.astype(BF16)
        g_lo = (r1 - g_mid.astype(F32)).astype(BF16)
        cum = (jnp.dot(tri, g_hi, preferred_element_type=F32)
               + jnp.dot(tri, g_mid, preferred_element_type=F32)
               + jnp.dot(tri, g_lo, preferred_element_type=F32))

        for h in range(HEADS):
            sl = slice(h * dk, (h + 1) * dk)
            q3 = q_all[:, rows, sl]
            k3 = k_all[:, rows, sl]
            q = q3.reshape(c, dk)
            k = k3.reshape(c, dk)
            v = v_all[:, rows, sl].reshape(c, dk).astype(BF16)
            b = cum[:, sl]
            b3 = b.reshape(RES, per, dk)
            b_last = b[c - 1:c, :]
            st = st_ref[h]

            q_dec = (q * jnp.exp(b)).astype(BF16)
            o = lax.dot_general(q_dec, st.astype(BF16), (((1,), (1,)), ((), ())),
                                preferred_element_type=F32)

            a = jnp.where(diag_mask, jnp.sum(q * k, axis=-1, keepdims=True), 0.0)
            for li, half in enumerate(HGRN_LEVELS):
                e = jnp.exp2(jnp.abs(b3 - _level_reference(b3, half)) * -LOG2E)
                q_l = _scale_valid_rows(q3, e, half, lower=True).reshape(c, dk).astype(BF16)
                k_l = _scale_valid_rows(k3, e, half, lower=False).reshape(c, dk).astype(BF16)
                p_l = lax.dot_general(q_l, k_l, (((1,), (1,)), ((), ())), preferred_element_type=F32)
                a = jnp.where(level_masks[li], p_l, a)
            o = o + jnp.dot(a.astype(BF16), v, preferred_element_type=F32)

            k_dec = (k * jnp.exp(b_last - b)).astype(BF16)
            st_ref[h] = st * jnp.exp(b_last) + lax.dot_general(
                v, k_dec, (((0,), (0,)), ((), ())), preferred_element_type=F32)

            r = o * _rms_scale(o) * gain_ref[:, sl] * gate_all[:, rows, sl].reshape(c, dk)
            outs[h][ci] = r.reshape(RES, per, dk)

    for h in range(HEADS):
        o_ref[:, :, h * dk:(h + 1) * dk] = jnp.concatenate(outs[h], axis=1).astype(o_ref.dtype)


def _hgrn_operands(proj, keep, gain, *, first_col_block, rows):
    batch, _, per, _ = proj.shape
    tiles = per // rows
    tri = jnp.asarray(_hgrn_cumsum_matrix(), BF16)

    def col_spec(col_block):
        return pl.BlockSpec((None, RES, rows, GROUP_WIDTH), lambda i: (i // tiles, 0, i % tiles, col_block))

    const = lambda i: (0, 0)
    return dict(
        inputs=[proj, proj, proj, proj, keep, gain, tri],
        in_specs=[col_spec(first_col_block + k) for k in range(4)]
        + [col_spec(0), pl.BlockSpec(gain.shape, const), pl.BlockSpec(tri.shape, const)],
        out_specs=[col_spec(0)],
        out_shape=[jax.ShapeDtypeStruct((batch, RES, per, GROUP_WIDTH), BF16)],
        scratch=[pltpu.VMEM((HEADS, HEAD_DIM, HEAD_DIM), F32)],
        tiles=tiles)


def _mixers(proj, keep, gain, *, hgrn_col_block):
    batch, _, per, _ = proj.shape
    rows = ATTN_SUB_BLOCKS * ATTN_BACK // RES
    steps = batch * per // rows
    parts = [_attn_operands(proj, g, dilation) for g, (_, dilation) in enumerate(ATTN_GROUPS)]
    parts.append(_hgrn_operands(proj, keep, gain, first_col_block=hgrn_col_block, rows=rows))
    n_in = [len(p["inputs"]) for p in parts]
    n_out = [len(p["out_shape"]) for p in parts]
    n_scr = [len(p["scratch"]) for p in parts]

    def kernel_body(*refs):
        ins, rest = refs[:sum(n_in)], refs[sum(n_in):]
        outs, scr = rest[:sum(n_out)], rest[sum(n_out):]
        i = pl.program_id(0)
        take = lambda seq, counts, k: seq[sum(counts[:k]):sum(counts[:k + 1])]
        for k, part in enumerate(parts[:-1]):
            _attn_body(i % part["nblk"], *take(ins, n_in, k), *take(outs, n_out, k), *take(scr, n_scr, k),
                       rows_per_residue=part["rows_per_residue"])
        k = len(parts) - 1
        _hgrn_body(i % parts[k]["tiles"] == 0, *take(ins, n_in, k), *take(outs, n_out, k), *take(scr, n_scr, k))

    flat = lambda key: [v for p in parts for v in p[key]]
    res = pl.pallas_call(
        kernel_body,
        grid=(steps,),
        in_specs=flat("in_specs"),
        out_specs=flat("out_specs"),
        out_shape=flat("out_shape"),
        scratch_shapes=flat("scratch"),
        compiler_params=_params(("arbitrary",)),
        name="token_mixers",
    )(*flat("inputs"))
    lse_w = HEADS * LSE_LANES
    outs = [res[2 * g].reshape(batch, RES, per, GROUP_WIDTH) for g in range(N_GROUPS)]
    lses = [res[2 * g + 1].reshape(batch, RES, per, lse_w) for g in range(N_GROUPS)]
    return outs, lses, res[2 * N_GROUPS]


def _merge_kernel(o1_ref, o2_ref, o3_ref, l1_ref, l2_ref, l3_ref, r_ref,
                  ga0_ref, ga1_ref, gr0_ref, gr1_ref, x_ref,
                  wpa_ref, wpr_ref, wo_ref, gpost_ref, gffn_ref,
                  h_ref, a2_ref, attn_s):
    tm = x_ref.shape[0]
    per = tm // RES
    flat = lambda ref: ref[...].reshape(tm, ref.shape[-1])

    la, lb, lc = flat(l1_ref), flat(l2_ref), flat(l3_ref)
    lm = jnp.maximum(la, jnp.maximum(lb, lc))
    ea, eb, ec = jnp.exp(la - lm), jnp.exp(lb - lm), jnp.exp(lc - lm)
    den = ea + eb + ec
    wa, wb, wc = ea / den, eb / den, ec / den
    o1, o2, o3 = flat(o1_ref), flat(o2_ref), flat(o3_ref)
    for h in range(HEADS):
        sl = slice(h * HEAD_DIM, (h + 1) * HEAD_DIM)
        c0 = h * LSE_LANES
        mixed = (wa[:, c0:c0 + 1] * o1[:, sl].astype(F32)
                 + wb[:, c0:c0 + 1] * o2[:, sl].astype(F32)
                 + wc[:, c0:c0 + 1] * o3[:, sl].astype(F32))
        attn_s[:, sl] = mixed.astype(BF16)

    tok = lax.broadcasted_iota(jnp.int32, (tm, tm), 0)
    src = _block_position(lax.broadcasted_iota(jnp.int32, (tm, tm), 1), per, tm)
    unpermute = jnp.where(src == tok, 1.0, 0.0).astype(BF16)

    attn = attn_s[...]
    r = flat(r_ref)
    half_w = GROUP_WIDTH
    halves = [slice(half * half_w, (half + 1) * half_w) for half in range(2)]
    branch = [(jnp.dot(attn, wpa_ref[:, cs], preferred_element_type=F32),
               jnp.dot(r, wpr_ref[:, cs], preferred_element_type=F32)) for cs in halves]
    z = None
    for cs, (ya, yr), (ga_ref, gr_ref) in zip(halves, branch, ((ga0_ref, gr0_ref), (ga1_ref, gr1_ref))):
        y = _sigmoid(flat(ga_ref).astype(F32)) * ya + _sigmoid(flat(gr_ref).astype(F32)) * yr
        y_nat = jnp.dot(unpermute, y.astype(BF16), preferred_element_type=F32).astype(BF16)
        part = jnp.dot(y_nat, wo_ref[cs, :], preferred_element_type=F32)
        z = part if z is None else z + part
    h1 = x_ref[...] + z * _rms_scale(z) * gpost_ref[...]
    h_ref[...] = h1
    a2_ref[...] = (h1 * _rms_scale(h1) * gffn_ref[...]).astype(BF16)


def _merge(outs, lses, r_out, proj, x2, wpa, wpr, wo, gpost, gffn, *, gate_col_block, tm=256):
    t, d = x2.shape
    batch, _, per, _ = proj.shape
    rows = tm // RES
    tiles_per_seq = per // rows
    const = lambda i: (0, 0)
    nat = pl.BlockSpec((tm, d), lambda i: (i, 0))

    def lay(width, col_block=0):
        return pl.BlockSpec((None, RES, rows, width),
                            lambda i: (i // tiles_per_seq, 0, i % tiles_per_seq, col_block))

    def resident(shape):
        return pl.BlockSpec(shape, const, pipeline_mode=pl.Buffered(1))

    lse_w = HEADS * LSE_LANES
    return pl.pallas_call(
        _merge_kernel,
        grid=(t // tm,),
        in_specs=[lay(GROUP_WIDTH)] * 3 + [lay(lse_w)] * 3 + [lay(GROUP_WIDTH)]
        + [lay(GROUP_WIDTH, gate_col_block + k) for k in range(4)]
        + [nat, resident(wpa.shape), resident(wpr.shape), resident(wo.shape),
           resident(gpost.shape), resident(gffn.shape)],
        out_specs=[nat, nat],
        out_shape=[jax.ShapeDtypeStruct((t, d), F32), jax.ShapeDtypeStruct((t, d), BF16)],
        scratch_shapes=[pltpu.VMEM((tm, GROUP_WIDTH), BF16)],
        compiler_params=_params(("parallel",)),
        name="merge_out_proj",
    )(*outs, *lses, r_out, proj, proj, proj, proj, x2, wpa, wpr, wo, gpost, gffn)


def _ffn_hidden_kernel(a_ref, wg_ref, wu_ref, o_ref, wg_b, wu_b):
    @pl.when(pl.program_id(1) == 0)
    def _():
        wg_b[...] = wg_ref[...].astype(BF16)
        wu_b[...] = wu_ref[...].astype(BF16)

    a = a_ref[...]
    gt = jnp.dot(a, wg_b[...], preferred_element_type=F32)
    up = jnp.dot(a, wu_b[...], preferred_element_type=F32)
    o_ref[...] = (_silu(gt) * up).astype(o_ref.dtype)


def _ffn_out_kernel(hid_ref, wd_ref, h_ref, gain_ref, o_ref):
    ff = jnp.dot(hid_ref[...], wd_ref[...], preferred_element_type=F32)
    o_ref[...] = h_ref[...] + ff * _rms_scale(ff) * gain_ref[...]


def _ffn(a2, w_gate_up, w_down, h1, gain, *, tm_hidden=1024, tf=512, tm_out=256):
    t, d = a2.shape
    d_ff = w_down.shape[0]
    nf = d_ff // tf
    hidden = pl.pallas_call(
        _ffn_hidden_kernel,
        grid=(nf, t // tm_hidden),
        in_specs=[
            pl.BlockSpec((tm_hidden, d), lambda j, i: (i, 0)),
            pl.BlockSpec((d, tf), lambda j, i: (0, j)),
            pl.BlockSpec((d, tf), lambda j, i: (0, nf + j)),
        ],
        out_specs=pl.BlockSpec((tm_hidden, tf), lambda j, i: (i, j)),
        out_shape=jax.ShapeDtypeStruct((t, d_ff), BF16),
        scratch_shapes=[pltpu.VMEM((d, tf), BF16), pltpu.VMEM((d, tf), BF16)],
        compiler_params=_params(("arbitrary", "arbitrary")),
        name="swiglu_hidden",
    )(a2, w_gate_up, w_gate_up)
    rows = pl.BlockSpec((tm_out, d), lambda i: (i, 0))
    return pl.pallas_call(
        _ffn_out_kernel,
        grid=(t // tm_out,),
        in_specs=[
            pl.BlockSpec((tm_out, d_ff), lambda i: (i, 0)),
            pl.BlockSpec((d_ff, d), lambda i: (0, 0), pipeline_mode=pl.Buffered(1)),
            rows,
            pl.BlockSpec((1, d), lambda i: (0, 0)),
        ],
        out_specs=rows,
        out_shape=jax.ShapeDtypeStruct((t, d), F32),
        compiler_params=_params(("parallel",)),
        name="swiglu_out",
    )(hidden, w_down, h1, gain)


def _rotary_tables(seq):
    inv_freq = ROPE_THETA ** (-jnp.arange(0, HEAD_DIM, 2, dtype=F32) / HEAD_DIM)
    ang = jnp.arange(seq, dtype=F32)[:, None] * inv_freq[None, :]
    cos = jnp.concatenate([jnp.cos(ang), jnp.cos(ang)], axis=-1)
    sin_signed = jnp.concatenate([-jnp.sin(ang), jnp.sin(ang)], axis=-1)
    lay = lambda tab: tab.reshape(seq // RES, RES, HEAD_DIM).transpose(1, 0, 2)
    return lay(cos), lay(sin_signed)


def kernel(x, w_in, w_attn_branch, w_hgrn_branch, w_mix_out, hgrn_lower_bounds, hgrn_norm_gain,
           norm_mix_pre, norm_mix_post, w_ffn_gate_up, w_ffn_down, norm_ffn_pre, norm_ffn_post):
    batch, seq, d = x.shape
    assert w_in.shape[0] == 1, "single-layer block"
    x2 = x.reshape(batch * seq, d)
    row = lambda v: v[0].reshape(1, -1).astype(F32)

    for window, dilation in ATTN_GROUPS:
        assert window // dilation == ATTN_BACK and seq % window == 0 and RES % dilation == 0
    assert w_in.shape[2] == (QKV_BLOCKS + 8) * GROUP_WIDTH
    cos_r, sin_r = _rotary_tables(seq)
    proj, keep = _in_proj(x2, row(norm_mix_pre), w_in[0].astype(BF16), cos_r, sin_r,
                          hgrn_lower_bounds.astype(F32), batch, seq)

    outs, lses, r_out = _mixers(proj, keep, row(hgrn_norm_gain), hgrn_col_block=QKV_BLOCKS)

    h1, a2 = _merge(outs, lses, r_out, proj, x2,
                    w_attn_branch[0].astype(BF16), w_hgrn_branch[0].astype(BF16),
                    w_mix_out[0].astype(BF16), row(norm_mix_post), row(norm_ffn_pre),
                    gate_col_block=QKV_BLOCKS + 4)

    out = _ffn(a2, w_ffn_gate_up[0].astype(F32), w_ffn_down[0].astype(BF16), h1, row(norm_ffn_post))
    return out.reshape(batch, seq, d)
```

```python
import functools
import math

import numpy as np
import jax
import jax.numpy as jnp
from jax import lax
from jax.experimental import pallas as pl
from jax.experimental.pallas import tpu as pltpu

F32 = jnp.float32
BF16 = jnp.bfloat16

HEAD_DIM = 128
ATTN_GROUPS = ((128, 1), (512, 4), (2048, 16))
N_GROUPS = len(ATTN_GROUPS)
HEADS = 8
GROUP_WIDTH = HEADS * HEAD_DIM
ROTARY_BLOCKS = 2 * N_GROUPS
QKV_BLOCKS = 3 * N_GROUPS
FORGET_BLOCK = QKV_BLOCKS + 1
ATTN_BACK = 128
ATTN_SUB_BLOCKS = 2
RES = 16
ROPE_THETA = 10000.0
NORM_EPS = 1e-6
HGRN_CHUNK = 128
HGRN_LEVELS = (64, 32, 16, 8, 4, 2, 1)
LSE_LANES = 16
BF16_SUBLANES = 16
NEG_BIG = -1e30
PERMUTE_GROUP = 256
LOG2E = math.log2(math.e)

VMEM_LIMIT_BYTES = 56 * 1024 * 1024


def _params(semantics):
    return pltpu.CompilerParams(dimension_semantics=semantics, vmem_limit_bytes=VMEM_LIMIT_BYTES)


def _rms_scale(v):
    return lax.rsqrt(jnp.mean(v * v, axis=-1, keepdims=True) + NORM_EPS)


def _sigmoid(v):
    return 0.5 * jnp.tanh(0.5 * v) + 0.5


def _silu(v):
    return v * _sigmoid(v)


def _block_position(i, rows_per_residue, block_rows):
    shift = rows_per_residue.bit_length() - 1
    n_res_shift = (block_rows // rows_per_residue).bit_length() - 1
    return ((i & (rows_per_residue - 1)) << n_res_shift) | lax.shift_right_logical(i, shift)


def _in_proj_kernel(x_ref, gain_ref, w_ref, cos_ref, sin_ref, hlb_ref, o_ref, keep_ref, a_ref):
    tm = x_ref.shape[0]
    per_res = tm // RES
    grp = PERMUTE_GROUP
    per_grp = grp // RES
    j = pl.program_id(1)

    @pl.when(j == 0)
    def _():
        row_token = _block_position(lax.broadcasted_iota(jnp.int32, (grp, grp), 0), per_grp, grp)
        permute = jnp.where(row_token == lax.broadcasted_iota(jnp.int32, (grp, grp), 1), 1.0, 0.0).astype(BF16)
        gain = gain_ref[...]
        for gi in range(tm // grp):
            xg = x_ref[gi * grp:(gi + 1) * grp, :]
            ag = (xg * _rms_scale(xg) * gain).astype(BF16)
            pg = jnp.dot(permute, ag, preferred_element_type=F32).astype(BF16)
            for r in range(RES):
                dst = r * per_res + gi * per_grp
                a_ref[dst:dst + per_grp, :] = pg[r * per_grp:(r + 1) * per_grp, :]

    def project():
        return jnp.dot(a_ref[...], w_ref[...], preferred_element_type=F32)

    def store(ref, val):
        ref[...] = val.reshape(RES, per_res, val.shape[-1]).astype(ref.dtype)

    @pl.when(j < ROTARY_BLOCKS)
    def _():
        res = project()
        scale = jnp.where(j < N_GROUPS, HEAD_DIM ** -0.5 * LOG2E, 1.0)
        cos = cos_ref[...].reshape(tm, HEAD_DIM) * scale
        sin = sin_ref[...].reshape(tm, HEAD_DIM) * scale
        heads = []
        for h in range(HEADS):
            t = res[:, h * HEAD_DIM:(h + 1) * HEAD_DIM]
            heads.append(t * cos + pltpu.roll(t, HEAD_DIM // 2, 1) * sin)
        store(o_ref, jnp.concatenate(heads, axis=1))

    is_swish = (j == FORGET_BLOCK - 1) | (j == FORGET_BLOCK + 2)

    @pl.when(is_swish)
    def _():
        store(o_ref, _silu(project()))

    @pl.when((j >= ROTARY_BLOCKS) & (j != FORGET_BLOCK) & jnp.logical_not(is_swish))
    def _():
        store(o_ref, project())

    @pl.when(j == FORGET_BLOCK)
    def _():
        res = project()
        h0 = hlb_ref[0:1, :]
        h1 = hlb_ref[1:2, :]
        hm = jnp.maximum(h0, h1)
        e0 = jnp.exp(h0 - hm)
        e1 = jnp.exp(h1 - hm)
        lb = e0 / (e0 + e1)
        c1 = 0.5 * (1.0 - lb)
        c0 = 1.0 - c1
        c1_tanh = c1 * jnp.tanh(0.5 * res)
        store(o_ref, jnp.log(c0 + c1_tanh))
        store(keep_ref, c1 - c1_tanh)


def _in_proj(x2, gain, w, cos_r, sin_r, lower_bounds, batch, seq, *, tm=1024, tn=GROUP_WIDTH):
    assert lower_bounds.shape[0] == 2, "forget-gate bound rows: DEPTH + 1 with DEPTH == 1"
    t, d = x2.shape
    n = w.shape[1]
    tiles_per_seq = seq // tm
    table = pl.BlockSpec((RES, tm // RES, HEAD_DIM), lambda i, j: (0, i % tiles_per_seq, 0))
    lay = lambda col: pl.BlockSpec((None, RES, tm // RES, tn),
                                   lambda i, j: (i // tiles_per_seq, 0, i % tiles_per_seq, col(j)))
    return pl.pallas_call(
        _in_proj_kernel,
        grid=(t // tm, n // tn),
        in_specs=[
            pl.BlockSpec((tm, d), lambda i, j: (i, 0)),
            pl.BlockSpec((1, d), lambda i, j: (0, 0)),
            pl.BlockSpec((d, tn), lambda i, j: (0, j)),
            table, table,
            pl.BlockSpec(lower_bounds.shape, lambda i, j: (0, 0)),
        ],
        out_specs=[lay(lambda j: j), lay(lambda j: 0)],
        out_shape=[jax.ShapeDtypeStruct((batch, RES, seq // RES, n), BF16),
                   jax.ShapeDtypeStruct((batch, RES, seq // RES, tn), BF16)],
        scratch_shapes=[pltpu.VMEM((tm, d), BF16)],
        compiler_params=_params(("parallel", "arbitrary")),
        name="in_proj",
    )(x2, gain, w, cos_r, sin_r, lower_bounds)


def _sub_blocks(ref, sl, nsub):
    v = ref[:, :, sl]
    a, u, w = v.shape
    step = u // nsub
    if step % BF16_SUBLANES:
        v = v.astype(F32)
    return [v[:, s * step:(s + 1) * step, :].reshape(a * step, w).astype(ref.dtype) for s in range(nsub)]


def _store_sub_blocks(ref, sl, blocks):
    a = ref.shape[0]
    parts = [b.reshape(a, b.shape[0] // a, b.shape[1]) for b in blocks]
    ref[:, :, sl] = jnp.concatenate(parts, axis=1).astype(ref.dtype)


def _attn_carry(n, kk_ref, vv_ref):
    blk = ATTN_BACK
    nsub = ATTN_SUB_BLOCKS

    @pl.when(n == 0)
    def _():
        kk_ref[0:blk, :] = jnp.zeros((blk, GROUP_WIDTH), BF16)
        vv_ref[0:blk, :] = jnp.zeros((blk, GROUP_WIDTH), BF16)

    @pl.when(n > 0)
    def _():
        kk_ref[0:blk, :] = kk_ref[nsub * blk:(nsub + 1) * blk, :]
        vv_ref[0:blk, :] = vv_ref[nsub * blk:(nsub + 1) * blk, :]


def _attn_body(n, q_ref, k_ref, v_ref, o_ref, lse_ref, kk_ref, vv_ref, *, rows_per_residue):
    blk = ATTN_BACK
    nsub = ATTN_SUB_BLOCKS

    qi = lax.broadcasted_iota(jnp.int32, (blk, 2 * blk), 0)
    kj = lax.broadcasted_iota(jnp.int32, (blk, 2 * blk), 1)
    q_pos = _block_position(qi, rows_per_residue, blk)
    k_pos = _block_position(kj & (blk - 1), rows_per_residue, blk) - jnp.where(kj < blk, blk, 0)
    dist = q_pos - k_pos
    band = (dist >= 0) & (dist <= ATTN_BACK)
    first_mask = band & (kj >= jnp.where(n > 0, 0, blk))
    lane_head = lax.shift_right_logical(lax.broadcasted_iota(jnp.int32, (blk, HEADS * LSE_LANES), 1),
                                        LSE_LANES.bit_length() - 1)
    lse_tiles = [jnp.zeros((blk, HEADS * LSE_LANES), F32) for _ in range(nsub)]

    for h in range(HEADS):
        sl = slice(h * HEAD_DIM, (h + 1) * HEAD_DIM)
        for s, (kb, vb) in enumerate(zip(_sub_blocks(k_ref, sl, nsub), _sub_blocks(v_ref, sl, nsub))):
            rows = slice((s + 1) * blk, (s + 2) * blk)
            kk_ref[rows, sl] = kb
            vv_ref[rows, sl] = vb
        outs = []
        for s, qb in enumerate(_sub_blocks(q_ref, sl, nsub)):
            keys = slice(s * blk, (s + 2) * blk)
            sc = lax.dot_general(qb, kk_ref[keys, sl], (((1,), (1,)), ((), ())), preferred_element_type=F32)
            sc = jnp.where(first_mask if s == 0 else band, sc, NEG_BIG)
            m = jnp.max(sc, axis=-1, keepdims=True)
            p = jnp.exp2(sc - m)
            den = jnp.sum(p, axis=-1, keepdims=True)
            o = jnp.dot(p.astype(BF16), vv_ref[keys, sl], preferred_element_type=F32)
            outs.append(o / den)
            lse = m * math.log(2.0) + jnp.log(den)
            lse_tiles[s] = jnp.where(lane_head == h, lse, lse_tiles[s])
        _store_sub_blocks(o_ref, sl, outs)
    _store_sub_blocks(lse_ref, slice(None), lse_tiles)


def _attn_operands(proj, g, dilation):
    batch, _, per, _ = proj.shape
    a = RES // dilation
    rows = ATTN_BACK // a
    rows_blk = rows * ATTN_SUB_BLOCKS
    nblk = per // rows_blk
    view = lambda arr: arr.reshape(arr.shape[:-3] + (a, dilation) + arr.shape[-2:])

    def spec(width, col_block):
        return pl.BlockSpec((None, a, None, rows_blk, width),
                            lambda i: (i // (dilation * nblk), 0, (i // nblk) % dilation, i % nblk, col_block))

    lse_w = HEADS * LSE_LANES
    proj_v = view(proj)
    key_rows = (ATTN_SUB_BLOCKS + 1) * ATTN_BACK
    return dict(
        inputs=[proj_v, proj_v, proj_v],
        in_specs=[spec(GROUP_WIDTH, which * N_GROUPS + g) for which in range(3)],
        out_specs=[spec(GROUP_WIDTH, 0), spec(lse_w, 0)],
        out_shape=[jax.ShapeDtypeStruct((batch, a, dilation, per, GROUP_WIDTH), BF16),
                   jax.ShapeDtypeStruct((batch, a, dilation, per, lse_w), F32)],
        scratch=[pltpu.VMEM((key_rows, GROUP_WIDTH), BF16), pltpu.VMEM((key_rows, GROUP_WIDTH), BF16)],
        rows_per_residue=rows, nblk=nblk)


def _hgrn_cumsum_matrix():
    c = HGRN_CHUNK
    i = np.arange(c)
    per = c // RES
    pos = (i % per) * RES + i // per
    return (pos[None, :] <= pos[:, None]).astype(np.float32)


def _level_reference(b3, half):
    res, per, dk = b3.shape
    if half < res:
        two = 2 * half
        parts = [jnp.broadcast_to(b3[blk * two + half - 1][None], (two, per, dk)) for blk in range(res // two)]
        return parts[0] if len(parts) == 1 else jnp.concatenate(parts, axis=0)
    last = b3[res - 1]
    step = half // res
    u = lax.broadcasted_iota(jnp.int32, (per, dk), 0)
    ref = None
    for blk in reversed(range(per // (2 * step))):
        row = jnp.broadcast_to(last[blk * 2 * step + step - 1][None], (per, dk))
        ref = row if ref is None else jnp.where(u < (blk + 1) * 2 * step, row, ref)
    return jnp.broadcast_to(ref[None], (res, per, dk))


def _scale_valid_rows(t3, e, half, lower):
    res = t3.shape[0]
    if half >= res:
        return t3 * e
    parts = []
    for lo in range(0, res, 2 * half):
        up, dn = slice(lo, lo + half), slice(lo + half, lo + 2 * half)
        parts += [t3[up], t3[dn] * e[dn]] if lower else [t3[up] * e[up], t3[dn]]
    return jnp.concatenate(parts, axis=0)


def _hgrn_body(first, q_ref, logf_ref, v_ref, gate_ref, keep_ref, gain_ref, tri_ref, o_ref, st_ref):
    c = HGRN_CHUNK
    dk = HEAD_DIM
    per = c // RES
    del first

    g_all = logf_ref[...].astype(F32)
    k_all = keep_ref[...].astype(F32)
    q_all = q_ref[...].astype(F32)
    v_all = v_ref[...].astype(F32)
    gate_all = gate_ref[...].astype(F32)

    ti = _block_position(lax.broadcasted_iota(jnp.int32, (c, c), 0), per, c)
    si = _block_position(lax.broadcasted_iota(jnp.int32, (c, c), 1), per, c)
    diag_mask = ti == si
    level_masks = []
    for half in HGRN_LEVELS:
        shift = half.bit_length()
        level_masks.append((lax.shift_right_logical(ti, shift) == lax.shift_right_logical(si, shift))
                           & ((ti & half) != 0) & ((si & half) == 0))
    tri = tri_ref[...]

    n_chunks = o_ref.shape[1] // per
    outs = [[None] * n_chunks for _ in range(HEADS)]
    for ci in range(n_chunks):
        rows = slice(ci * per, (ci + 1) * per)
        g = g_all[:, rows, :].reshape(c, GROUP_WIDTH)
        g_hi = g.astype(BF16)
        r1 = g - g_hi.astype(F32)
        g_mid = r1.astype(BF16)
        g_lo = (r1 - g_mid.astype(F32)).astype(BF16)
        cum = (jnp.dot(tri, g_hi, preferred_element_type=F32)
               + jnp.dot(tri, g_mid, preferred_element_type=F32)
               + jnp.dot(tri, g_lo, preferred_element_type=F32))

        for h in range(HEADS):
            sl = slice(h * dk, (h + 1) * dk)
            q3 = q_all[:, rows, sl]
            k3 = k_all[:, rows, sl]
            q = q3.reshape(c, dk)
            k = k3.reshape(c, dk)
            v = v_all[:, rows, sl].reshape(c, dk).astype(BF16)
            b = cum[:, sl]
            b3 = b.reshape(RES, per, dk)
            b_last = b[c - 1:c, :]
            st = st_ref[h]

            q_dec = (q * jnp.exp(b)).astype(BF16)
            o = lax.dot_general(q_dec, st.astype(BF16), (((1,), (1,)), ((), ())),
                                preferred_element_type=F32)

            a = jnp.where(diag_mask, jnp.sum(q * k, axis=-1, keepdims=True), 0.0)
            for li, half in enumerate(HGRN_LEVELS):
                e = jnp.exp2(jnp.abs(b3 - _level_reference(b3, half)) * -LOG2E)
                q_l = _scale_valid_rows(q3, e, half, lower=True).reshape(c, dk).astype(BF16)
                k_l = _scale_valid_rows(k3, e, half, lower=False).reshape(c, dk).astype(BF16)
                p_l = lax.dot_general(q_l, k_l, (((1,), (1,)), ((), ())), preferred_element_type=F32)
                a = jnp.where(level_masks[li], p_l, a)
            o = o + jnp.dot(a.astype(BF16), v, preferred_element_type=F32)

            k_dec = (k * jnp.exp(b_last - b)).astype(BF16)
            st_ref[h] = st * jnp.exp(b_last) + lax.dot_general(
                v, k_dec, (((0,), (0,)), ((), ())), preferred_element_type=F32)

            r = o * _rms_scale(o) * gain_ref[:, sl] * gate_all[:, rows, sl].reshape(c, dk)
            outs[h][ci] = r.reshape(RES, per, dk)

    for h in range(HEADS):
        o_ref[:, :, h * dk:(h + 1) * dk] = jnp.concatenate(outs[h], axis=1).astype(o_ref.dtype)


def _hgrn_operands(proj, keep, gain, *, first_col_block, rows):
    batch, _, per, _ = proj.shape
    tiles = per // rows
    tri = jnp.asarray(_hgrn_cumsum_matrix(), BF16)

    def col_spec(col_block):
        return pl.BlockSpec((None, RES, rows, GROUP_WIDTH), lambda i: (i // tiles, 0, i % tiles, col_block))

    const = lambda i: (0, 0)
    return dict(
        inputs=[proj, proj, proj, proj, keep, gain, tri],
        in_specs=[col_spec(first_col_block + k) for k in range(4)]
        + [col_spec(0), pl.BlockSpec(gain.shape, const), pl.BlockSpec(tri.shape, const)],
        out_specs=[col_spec(0)],
        out_shape=[jax.ShapeDtypeStruct((batch, RES, per, GROUP_WIDTH), BF16)],
        scratch=[pltpu.VMEM((HEADS, HEAD_DIM, HEAD_DIM), F32)],
        tiles=tiles)


def _mixers(proj, keep, gain, *, hgrn_col_block):
    batch, _, per, _ = proj.shape
    rows = ATTN_SUB_BLOCKS * ATTN_BACK // RES
    steps = batch * per // rows
    parts = [_attn_operands(proj, g, dilation) for g, (_, dilation) in enumerate(ATTN_GROUPS)]
    parts.append(_hgrn_operands(proj, keep, gain, first_col_block=hgrn_col_block, rows=rows))
    n_in = [len(p["inputs"]) for p in parts]
    n_out = [len(p["out_shape"]) for p in parts]
    n_scr = [len(p["scratch"]) for p in parts]

    def kernel_body(*refs):
        ins, rest = refs[:sum(n_in)], refs[sum(n_in):]
        outs, scr = rest[:sum(n_out)], rest[sum(n_out):]
        i = pl.program_id(0)
        take = lambda seq, counts, k: seq[sum(counts[:k]):sum(counts[:k + 1])]
        for k, part in enumerate(parts[:-1]):
            _attn_carry(i % part["nblk"], *take(scr, n_scr, k))
        k = len(parts) - 1
        first = i % parts[k]["tiles"] == 0

        @pl.when(first)
        def _():
            st_ref = take(scr, n_scr, k)[0]
            st_ref[...] = jnp.zeros(st_ref.shape, F32)

        for k, part in enumerate(parts[:-1]):
            _attn_body(i % part["nblk"], *take(ins, n_in, k), *take(outs, n_out, k), *take(scr, n_scr, k),
                       rows_per_residue=part["rows_per_residue"])
        k = len(parts) - 1
        _hgrn_body(first, *take(ins, n_in, k), *take(outs, n_out, k), *take(scr, n_scr, k))

    flat = lambda key: [v for p in parts for v in p[key]]
    res = pl.pallas_call(
        kernel_body,
        grid=(steps,),
        in_specs=flat("in_specs"),
        out_specs=flat("out_specs"),
        out_shape=flat("out_shape"),
        scratch_shapes=flat("scratch"),
        compiler_params=_params(("arbitrary",)),
        name="token_mixers",
    )(*flat("inputs"))
    lse_w = HEADS * LSE_LANES
    outs = [res[2 * g].reshape(batch, RES, per, GROUP_WIDTH) for g in range(N_GROUPS)]
    lses = [res[2 * g + 1].reshape(batch, RES, per, lse_w) for g in range(N_GROUPS)]
    return outs, lses, res[2 * N_GROUPS]


def _merge_kernel(o1_ref, o2_ref, o3_ref, l1_ref, l2_ref, l3_ref, r_ref,
                  ga0_ref, ga1_ref, gr0_ref, gr1_ref, x_ref,
                  wpa_ref, wpr_ref, wo_ref, gpost_ref, gffn_ref,
                  h_ref, a2_ref, attn_s):
    tm = x_ref.shape[0]
    per = tm // RES
    flat = lambda ref: ref[...].reshape(tm, ref.shape[-1])

    la, lb, lc = flat(l1_ref), flat(l2_ref), flat(l3_ref)
    lm = jnp.maximum(la, jnp.maximum(lb, lc))
    ea, eb, ec = jnp.exp(la - lm), jnp.exp(lb - lm), jnp.exp(lc - lm)
    den = ea + eb + ec
    wa, wb, wc = ea / den, eb / den, ec / den
    o1, o2, o3 = flat(o1_ref), flat(o2_ref), flat(o3_ref)
    for h in range(HEADS):
        sl = slice(h * HEAD_DIM, (h + 1) * HEAD_DIM)
        c0 = h * LSE_LANES
        mixed = (wa[:, c0:c0 + 1] * o1[:, sl].astype(F32)
                 + wb[:, c0:c0 + 1] * o2[:, sl].astype(F32)
                 + wc[:, c0:c0 + 1] * o3[:, sl].astype(F32))
        attn_s[:, sl] = mixed.astype(BF16)

    tok = lax.broadcasted_iota(jnp.int32, (tm, tm), 0)
    src = _block_position(lax.broadcasted_iota(jnp.int32, (tm, tm), 1), per, tm)
    unpermute = jnp.where(src == tok, 1.0, 0.0).astype(BF16)

    attn = attn_s[...]
    r = flat(r_ref)
    half_w = GROUP_WIDTH
    halves = [slice(half * half_w, (half + 1) * half_w) for half in range(2)]
    branch = [(jnp.dot(attn, wpa_ref[:, cs], preferred_element_type=F32),
               jnp.dot(r, wpr_ref[:, cs], preferred_element_type=F32)) for cs in halves]
    z = None
    for cs, (ya, yr), (ga_ref, gr_ref) in zip(halves, branch, ((ga0_ref, gr0_ref), (ga1_ref, gr1_ref))):
        y = _sigmoid(flat(ga_ref).astype(F32)) * ya + _sigmoid(flat(gr_ref).astype(F32)) * yr
        y_nat = jnp.dot(unpermute, y.astype(BF16), preferred_element_type=F32).astype(BF16)
        part = jnp.dot(y_nat, wo_ref[cs, :], preferred_element_type=F32)
        z = part if z is None else z + part
    h1 = x_ref[...] + z * _rms_scale(z) * gpost_ref[...]
    h_ref[...] = h1
    a2_ref[...] = (h1 * _rms_scale(h1) * gffn_ref[...]).astype(BF16)


def _merge(outs, lses, r_out, proj, x2, wpa, wpr, wo, gpost, gffn, *, gate_col_block, tm=256):
    t, d = x2.shape
    batch, _, per, _ = proj.shape
    rows = tm // RES
    tiles_per_seq = per // rows
    const = lambda i: (0, 0)
    nat = pl.BlockSpec((tm, d), lambda i: (i, 0))

    def lay(width, col_block=0):
        return pl.BlockSpec((None, RES, rows, width),
                            lambda i: (i // tiles_per_seq, 0, i % tiles_per_seq, col_block))

    def resident(shape):
        return pl.BlockSpec(shape, const, pipeline_mode=pl.Buffered(1))

    lse_w = HEADS * LSE_LANES
    return pl.pallas_call(
        _merge_kernel,
        grid=(t // tm,),
        in_specs=[lay(GROUP_WIDTH)] * 3 + [lay(lse_w)] * 3 + [lay(GROUP_WIDTH)]
        + [lay(GROUP_WIDTH, gate_col_block + k) for k in range(4)]
        + [nat, resident(wpa.shape), resident(wpr.shape), resident(wo.shape),
           resident(gpost.shape), resident(gffn.shape)],
        out_specs=[nat, nat],
        out_shape=[jax.ShapeDtypeStruct((t, d), F32), jax.ShapeDtypeStruct((t, d), BF16)],
        scratch_shapes=[pltpu.VMEM((tm, GROUP_WIDTH), BF16)],
        compiler_params=_params(("parallel",)),
        name="merge_out_proj",
    )(*outs, *lses, r_out, proj, proj, proj, proj, x2, wpa, wpr, wo, gpost, gffn)


def _ffn_hidden_kernel(a_ref, wg_ref, wu_ref, o_ref, wg_b, wu_b):
    @pl.when(pl.program_id(1) == 0)
    def _():
        wg_b[...] = wg_ref[...].astype(BF16)
        wu_b[...] = wu_ref[...].astype(BF16)

    a = a_ref[...]
    gt = jnp.dot(a, wg_b[...], preferred_element_type=F32)
    up = jnp.dot(a, wu_b[...], preferred_element_type=F32)
    o_ref[...] = (_silu(gt) * up).astype(o_ref.dtype)


def _ffn_out_kernel(hid_ref, wd_ref, h_ref, gain_ref, o_ref):
    ff = jnp.dot(hid_ref[...], wd_ref[...], preferred_element_type=F32)
    o_ref[...] = h_ref[...] + ff * _rms_scale(ff) * gain_ref[...]


def _ffn(a2, w_gate_up, w_down, h1, gain, *, tm_hidden=1024, tf=512, tm_out=256):
    t, d = a2.shape
    d_ff = w_down.shape[0]
    nf = d_ff // tf
    hidden = pl.pallas_call(
        _ffn_hidden_kernel,
        grid=(nf, t // tm_hidden),
        in_specs=[
            pl.BlockSpec((tm_hidden, d), lambda j, i: (i, 0)),
            pl.BlockSpec((d, tf), lambda j, i: (0, j)),
            pl.BlockSpec((d, tf), lambda j, i: (0, nf + j)),
        ],
        out_specs=pl.BlockSpec((tm_hidden, tf), lambda j, i: (i, j)),
        out_shape=jax.ShapeDtypeStruct((t, d_ff), BF16),
        scratch_shapes=[pltpu.VMEM((d, tf), BF16), pltpu.VMEM((d, tf), BF16)],
        compiler_params=_params(("arbitrary", "arbitrary")),
        name="swiglu_hidden",
    )(a2, w_gate_up, w_gate_up)
    rows = pl.BlockSpec((tm_out, d), lambda i: (i, 0))
    return pl.pallas_call(
        _ffn_out_kernel,
        grid=(t // tm_out,),
        in_specs=[
            pl.BlockSpec((tm_out, d_ff), lambda i: (i, 0)),
            pl.BlockSpec((d_ff, d), lambda i: (0, 0), pipeline_mode=pl.Buffered(1)),
            rows,
            pl.BlockSpec((1, d), lambda i: (0, 0)),
        ],
        out_specs=rows,
        out_shape=jax.ShapeDtypeStruct((t, d), F32),
        compiler_params=_params(("parallel",)),
        name="swiglu_out",
    )(hidden, w_down, h1, gain)


def _rotary_tables(seq):
    inv_freq = ROPE_THETA ** (-np.arange(0, HEAD_DIM, 2, dtype=np.float64) / HEAD_DIM)
    ang = np.arange(seq, dtype=np.float64)[:, None] * inv_freq[None, :]
    cos = np.concatenate([np.cos(ang), np.cos(ang)], axis=-1)
    sin_signed = np.concatenate([-np.sin(ang), np.sin(ang)], axis=-1)
    lay = lambda tab: jnp.asarray(tab.reshape(seq // RES, RES, HEAD_DIM).transpose(1, 0, 2), F32)
    return lay(cos), lay(sin_signed)


def kernel(x, w_in, w_attn_branch, w_hgrn_branch, w_mix_out, hgrn_lower_bounds, hgrn_norm_gain,
           norm_mix_pre, norm_mix_post, w_ffn_gate_up, w_ffn_down, norm_ffn_pre, norm_ffn_post):
    batch, seq, d = x.shape
    assert w_in.shape[0] == 1, "single-layer block"
    x2 = x.reshape(batch * seq, d)
    row = lambda v: v[0].reshape(1, -1).astype(F32)

    for window, dilation in ATTN_GROUPS:
        assert window // dilation == ATTN_BACK and seq % window == 0 and RES % dilation == 0
    assert w_in.shape[2] == (QKV_BLOCKS + 8) * GROUP_WIDTH
    cos_r, sin_r = _rotary_tables(seq)
    proj, keep = _in_proj(x2, row(norm_mix_pre), w_in[0].astype(BF16), cos_r, sin_r,
                          hgrn_lower_bounds.astype(F32), batch, seq)

    outs, lses, r_out = _mixers(proj, keep, row(hgrn_norm_gain), hgrn_col_block=QKV_BLOCKS)

    h1, a2 = _merge(outs, lses, r_out, proj, x2,
                    w_attn_branch[0].astype(BF16), w_hgrn_branch[0].astype(BF16),
                    w_mix_out[0].astype(BF16), row(norm_mix_post), row(norm_ffn_pre),
                    gate_col_block=QKV_BLOCKS + 4)

    out = _ffn(a2, w_ffn_gate_up[0].astype(F32), w_ffn_down[0].astype(BF16), h1, row(norm_ffn_post))
    return out.reshape(batch, seq, d)
```

```python
import functools
import math

import numpy as np
import jax
import jax.numpy as jnp
from jax import lax
from jax.experimental import pallas as pl
from jax.experimental.pallas import tpu as pltpu

F32 = jnp.float32
BF16 = jnp.bfloat16

HEAD_DIM = 128
ATTN_GROUPS = ((128, 1), (512, 4), (2048, 16))
N_GROUPS = len(ATTN_GROUPS)
HEADS = 8
GROUP_WIDTH = HEADS * HEAD_DIM
ROTARY_BLOCKS = 2 * N_GROUPS
QKV_BLOCKS = 3 * N_GROUPS
FORGET_BLOCK = QKV_BLOCKS + 1
ATTN_BACK = 128
ATTN_SUB_BLOCKS = 2
RES = 16
ROPE_THETA = 10000.0
NORM_EPS = 1e-6
HGRN_CHUNK = 128
HGRN_LEVELS = (64, 32, 16, 8, 4, 2, 1)
LSE_LANES = 16
BF16_SUBLANES = 16
NEG_BIG = -1e30
MERGE_ROW_SPLITS = 2
PERMUTE_GROUP = 256
LOG2E = math.log2(math.e)

VMEM_LIMIT_BYTES = 56 * 1024 * 1024


def _params(semantics):
    return pltpu.CompilerParams(dimension_semantics=semantics, vmem_limit_bytes=VMEM_LIMIT_BYTES)


def _rms_scale(v):
    return lax.rsqrt(jnp.mean(v * v, axis=-1, keepdims=True) + NORM_EPS)


def _sigmoid(v):
    return 0.5 * jnp.tanh(0.5 * v) + 0.5


def _silu(v):
    return v * _sigmoid(v)


def _block_position(i, rows_per_residue, block_rows):
    shift = rows_per_residue.bit_length() - 1
    n_res_shift = (block_rows // rows_per_residue).bit_length() - 1
    return ((i & (rows_per_residue - 1)) << n_res_shift) | lax.shift_right_logical(i, shift)


def _in_proj_kernel(x_ref, gain_ref, w_ref, cos_ref, sin_ref, hlb_ref, o_ref, keep_ref, a_ref):
    tm = x_ref.shape[0]
    per_res = tm // RES
    grp = PERMUTE_GROUP
    per_grp = grp // RES
    j = pl.program_id(1)

    @pl.when(j == 0)
    def _():
        row_token = _block_position(lax.broadcasted_iota(jnp.int32, (grp, grp), 0), per_grp, grp)
        permute = jnp.where(row_token == lax.broadcasted_iota(jnp.int32, (grp, grp), 1), 1.0, 0.0).astype(BF16)
        gain = gain_ref[...]
        for gi in range(tm // grp):
            xg = x_ref[gi * grp:(gi + 1) * grp, :]
            ag = (xg * _rms_scale(xg) * gain).astype(BF16)
            pg = jnp.dot(permute, ag, preferred_element_type=F32).astype(BF16)
            for r in range(RES):
                dst = r * per_res + gi * per_grp
                a_ref[dst:dst + per_grp, :] = pg[r * per_grp:(r + 1) * per_grp, :]

    def project():
        return jnp.dot(a_ref[...], w_ref[...].astype(BF16), preferred_element_type=F32)

    def store(ref, val):
        ref[...] = val.reshape(RES, per_res, val.shape[-1]).astype(ref.dtype)

    @pl.when(j < ROTARY_BLOCKS)
    def _():
        res = project()
        scale = jnp.where(j < N_GROUPS, HEAD_DIM ** -0.5 * LOG2E, 1.0)
        cos = cos_ref[...].reshape(tm, HEAD_DIM) * scale
        sin = sin_ref[...].reshape(tm, HEAD_DIM) * scale
        heads = []
        for h in range(HEADS):
            t = res[:, h * HEAD_DIM:(h + 1) * HEAD_DIM]
            heads.append(t * cos + pltpu.roll(t, HEAD_DIM // 2, 1) * sin)
        store(o_ref, jnp.concatenate(heads, axis=1))

    is_swish = (j == FORGET_BLOCK - 1) | (j == FORGET_BLOCK + 2)

    @pl.when(is_swish)
    def _():
        store(o_ref, _silu(project()))

    @pl.when((j >= ROTARY_BLOCKS) & (j != FORGET_BLOCK) & jnp.logical_not(is_swish))
    def _():
        store(o_ref, project())

    @pl.when(j == FORGET_BLOCK)
    def _():
        res = project()
        h0 = hlb_ref[0:1, :]
        h1 = hlb_ref[1:2, :]
        hm = jnp.maximum(h0, h1)
        e0 = jnp.exp(h0 - hm)
        e1 = jnp.exp(h1 - hm)
        lb = e0 / (e0 + e1)
        c1 = 0.5 * (1.0 - lb)
        c0 = 1.0 - c1
        c1_tanh = c1 * jnp.tanh(0.5 * res)
        store(o_ref, jnp.log(c0 + c1_tanh))
        store(keep_ref, c1 - c1_tanh)


def _in_proj(x2, gain, w, cos_r, sin_r, lower_bounds, batch, seq, *, tm=1024, tn=GROUP_WIDTH):
    assert lower_bounds.shape[0] == 2, "forget-gate bound rows: DEPTH + 1 with DEPTH == 1"
    t, d = x2.shape
    n = w.shape[1]
    tiles_per_seq = seq // tm
    table = pl.BlockSpec((RES, tm // RES, HEAD_DIM), lambda i, j: (0, i % tiles_per_seq, 0))
    lay = lambda col: pl.BlockSpec((None, RES, tm // RES, tn),
                                   lambda i, j: (i // tiles_per_seq, 0, i % tiles_per_seq, col(j)))
    return pl.pallas_call(
        _in_proj_kernel,
        grid=(t // tm, n // tn),
        in_specs=[
            pl.BlockSpec((tm, d), lambda i, j: (i, 0)),
            pl.BlockSpec((1, d), lambda i, j: (0, 0)),
            pl.BlockSpec((d, tn), lambda i, j: (0, j)),
            table, table,
            pl.BlockSpec(lower_bounds.shape, lambda i, j: (0, 0)),
        ],
        out_specs=[lay(lambda j: j), lay(lambda j: 0)],
        out_shape=[jax.ShapeDtypeStruct((batch, RES, seq // RES, n), BF16),
                   jax.ShapeDtypeStruct((batch, RES, seq // RES, tn), BF16)],
        scratch_shapes=[pltpu.VMEM((tm, d), BF16)],
        compiler_params=_params(("parallel", "arbitrary")),
        name="in_proj",
    )(x2, gain, w, cos_r, sin_r, lower_bounds)


def _sub_blocks(ref, sl, nsub):
    v = ref[:, :, sl]
    a, u, w = v.shape
    step = u // nsub
    if step % BF16_SUBLANES:
        v = v.astype(F32)
    return [v[:, s * step:(s + 1) * step, :].reshape(a * step, w).astype(ref.dtype) for s in range(nsub)]


def _store_sub_blocks(ref, sl, blocks):
    a = ref.shape[0]
    parts = [b.reshape(a, b.shape[0] // a, b.shape[1]) for b in blocks]
    ref[:, :, sl] = jnp.concatenate(parts, axis=1).astype(ref.dtype)


def _attn_carry(n, kk_ref, vv_ref):
    blk = ATTN_BACK
    nsub = ATTN_SUB_BLOCKS

    @pl.when(n == 0)
    def _():
        kk_ref[0:blk, :] = jnp.zeros((blk, GROUP_WIDTH), BF16)
        vv_ref[0:blk, :] = jnp.zeros((blk, GROUP_WIDTH), BF16)

    @pl.when(n > 0)
    def _():
        kk_ref[0:blk, :] = kk_ref[nsub * blk:(nsub + 1) * blk, :]
        vv_ref[0:blk, :] = vv_ref[nsub * blk:(nsub + 1) * blk, :]


def _attn_body(n, q_ref, k_ref, v_ref, o_ref, lse_ref, kk_ref, vv_ref, *, rows_per_residue):
    blk = ATTN_BACK
    nsub = ATTN_SUB_BLOCKS

    qi = lax.broadcasted_iota(jnp.int32, (blk, 2 * blk), 0)
    kj = lax.broadcasted_iota(jnp.int32, (blk, 2 * blk), 1)
    q_pos = _block_position(qi, rows_per_residue, blk)
    k_pos = _block_position(kj & (blk - 1), rows_per_residue, blk) - jnp.where(kj < blk, blk, 0)
    dist = q_pos - k_pos
    band = (dist >= 0) & (dist <= ATTN_BACK)
    first_mask = band & (kj >= jnp.where(n > 0, 0, blk))
    lane_head = lax.shift_right_logical(lax.broadcasted_iota(jnp.int32, (blk, HEADS * LSE_LANES), 1),
                                        LSE_LANES.bit_length() - 1)
    lse_tiles = [jnp.zeros((blk, HEADS * LSE_LANES), F32) for _ in range(nsub)]

    for h in range(HEADS):
        sl = slice(h * HEAD_DIM, (h + 1) * HEAD_DIM)
        for s, (kb, vb) in enumerate(zip(_sub_blocks(k_ref, sl, nsub), _sub_blocks(v_ref, sl, nsub))):
            rows = slice((s + 1) * blk, (s + 2) * blk)
            kk_ref[rows, sl] = kb
            vv_ref[rows, sl] = vb
        outs = []
        for s, qb in enumerate(_sub_blocks(q_ref, sl, nsub)):
            keys = slice(s * blk, (s + 2) * blk)
            sc = lax.dot_general(qb, kk_ref[keys, sl], (((1,), (1,)), ((), ())), preferred_element_type=F32)
            sc = jnp.where(first_mask if s == 0 else band, sc, NEG_BIG)
            m = jnp.max(sc, axis=-1, keepdims=True)
            p = jnp.exp2(sc - m)
            den = jnp.sum(p, axis=-1, keepdims=True)
            o = jnp.dot(p.astype(BF16), vv_ref[keys, sl], preferred_element_type=F32)
            outs.append(o / den)
            lse = m * math.log(2.0) + jnp.log(den)
            lse_tiles[s] = jnp.where(lane_head == h, lse, lse_tiles[s])
        _store_sub_blocks(o_ref, sl, outs)
    _store_sub_blocks(lse_ref, slice(None), lse_tiles)


def _attn_operands(proj, g, dilation):
    batch, _, per, _ = proj.shape
    a = RES // dilation
    rows = ATTN_BACK // a
    rows_blk = rows * ATTN_SUB_BLOCKS
    nblk = per // rows_blk
    view = lambda arr: arr.reshape(arr.shape[:-3] + (a, dilation) + arr.shape[-2:])

    def spec(width, col_block):
        return pl.BlockSpec((None, a, None, rows_blk, width),
                            lambda i: (i // (dilation * nblk), 0, (i // nblk) % dilation, i % nblk, col_block))

    lse_w = HEADS * LSE_LANES
    proj_v = view(proj)
    key_rows = (ATTN_SUB_BLOCKS + 1) * ATTN_BACK
    return dict(
        inputs=[proj_v, proj_v, proj_v],
        in_specs=[spec(GROUP_WIDTH, which * N_GROUPS + g) for which in range(3)],
        out_specs=[spec(GROUP_WIDTH, 0), spec(lse_w, 0)],
        out_shape=[jax.ShapeDtypeStruct((batch, a, dilation, per, GROUP_WIDTH), BF16),
                   jax.ShapeDtypeStruct((batch, a, dilation, per, lse_w), F32)],
        scratch=[pltpu.VMEM((key_rows, GROUP_WIDTH), BF16), pltpu.VMEM((key_rows, GROUP_WIDTH), BF16)],
        rows_per_residue=rows, nblk=nblk)


def _hgrn_cumsum_matrix():
    c = HGRN_CHUNK
    i = np.arange(c)
    per = c // RES
    pos = (i % per) * RES + i // per
    return (pos[None, :] <= pos[:, None]).astype(np.float32)


def _level_reference(b3, half):
    res, per, dk = b3.shape
    if half < res:
        two = 2 * half
        parts = [jnp.broadcast_to(b3[blk * two + half - 1][None], (two, per, dk)) for blk in range(res // two)]
        return parts[0] if len(parts) == 1 else jnp.concatenate(parts, axis=0)
    last = b3[res - 1]
    step = half // res
    u = lax.broadcasted_iota(jnp.int32, (per, dk), 0)
    ref = None
    for blk in reversed(range(per // (2 * step))):
        row = jnp.broadcast_to(last[blk * 2 * step + step - 1][None], (per, dk))
        ref = row if ref is None else jnp.where(u < (blk + 1) * 2 * step, row, ref)
    return jnp.broadcast_to(ref[None], (res, per, dk))


def _scale_valid_rows(t3, e, half, lower):
    res = t3.shape[0]
    if half >= res:
        return t3 * e
    parts = []
    for lo in range(0, res, 2 * half):
        up, dn = slice(lo, lo + half), slice(lo + half, lo + 2 * half)
        parts += [t3[up], t3[dn] * e[dn]] if lower else [t3[up] * e[up], t3[dn]]
    return jnp.concatenate(parts, axis=0)


def _hgrn_body(first, q_ref, logf_ref, v_ref, gate_ref, keep_ref, gain_ref, tri_ref, o_ref, st_ref):
    c = HGRN_CHUNK
    dk = HEAD_DIM
    per = c // RES
    del first

    g_all = logf_ref[...].astype(F32)
    k_all = keep_ref[...].astype(F32)
    q_all = q_ref[...].astype(F32)
    v_all = v_ref[...].astype(F32)
    gate_all = gate_ref[...].astype(F32)

    ti = _block_position(lax.broadcasted_iota(jnp.int32, (c, c), 0), per, c)
    si = _block_position(lax.broadcasted_iota(jnp.int32, (c, c), 1), per, c)
    diag_mask = ti == si
    level_masks = []
    for half in HGRN_LEVELS:
        shift = half.bit_length()
        level_masks.append((lax.shift_right_logical(ti, shift) == lax.shift_right_logical(si, shift))
                           & ((ti & half) != 0) & ((si & half) == 0))
    tri = tri_ref[...]

    n_chunks = o_ref.shape[1] // per
    outs = [[None] * n_chunks for _ in range(HEADS)]
    for ci in range(n_chunks):
        rows = slice(ci * per, (ci + 1) * per)
        g = g_all[:, rows, :].reshape(c, GROUP_WIDTH)
        g_hi = g.astype(BF16)
        r1 = g - g_hi.astype(F32)
        g_mid = r1.astype(BF16)
        g_lo = (r1 - g_mid.astype(F32)).astype(BF16)
        cum = (jnp.dot(tri, g_hi, preferred_element_type=F32)
               + jnp.dot(tri, g_mid, preferred_element_type=F32)
               + jnp.dot(tri, g_lo, preferred_element_type=F32))

        for h in range(HEADS):
            sl = slice(h * dk, (h + 1) * dk)
            q3 = q_all[:, rows, sl]
            k3 = k_all[:, rows, sl]
            q = q3.reshape(c, dk)
            k = k3.reshape(c, dk)
            v = v_all[:, rows, sl].reshape(c, dk).astype(BF16)
            b = cum[:, sl]
            b3 = b.reshape(RES, per, dk)
            b_last = b[c - 1:c, :]
            st = st_ref[h]

            q_dec = (q * jnp.exp(b)).astype(BF16)
            o = lax.dot_general(q_dec, st.astype(BF16), (((1,), (1,)), ((), ())),
                                preferred_element_type=F32)

            a = jnp.where(diag_mask, jnp.sum(q * k, axis=-1, keepdims=True), 0.0)
            for li, half in enumerate(HGRN_LEVELS):
                e = jnp.exp2(jnp.abs(b3 - _level_reference(b3, half)) * -LOG2E)
                q_l = _scale_valid_rows(q3, e, half, lower=True).reshape(c, dk).astype(BF16)
                k_l = _scale_valid_rows(k3, e, half, lower=False).reshape(c, dk).astype(BF16)
                p_l = lax.dot_general(q_l, k_l, (((1,), (1,)), ((), ())), preferred_element_type=F32)
                a = jnp.where(level_masks[li], p_l, a)
            o = o + jnp.dot(a.astype(BF16), v, preferred_element_type=F32)

            k_dec = (k * jnp.exp(b_last - b)).astype(BF16)
            st_ref[h] = st * jnp.exp(b_last) + lax.dot_general(
                v, k_dec, (((0,), (0,)), ((), ())), preferred_element_type=F32)

            r = o * _rms_scale(o) * gain_ref[:, sl] * gate_all[:, rows, sl].reshape(c, dk)
            outs[h][ci] = r.reshape(RES, per, dk)

    for h in range(HEADS):
        o_ref[:, :, h * dk:(h + 1) * dk] = jnp.concatenate(outs[h], axis=1).astype(o_ref.dtype)


def _hgrn_operands(proj, keep, gain, *, first_col_block, rows):
    batch, _, per, _ = proj.shape
    tiles = per // rows
    tri = jnp.asarray(_hgrn_cumsum_matrix(), BF16)

    def col_spec(col_block):
        return pl.BlockSpec((None, RES, rows, GROUP_WIDTH), lambda i: (i // tiles, 0, i % tiles, col_block))

    const = lambda i: (0, 0)
    return dict(
        inputs=[proj, proj, proj, proj, keep, gain, tri],
        in_specs=[col_spec(first_col_block + k) for k in range(4)]
        + [col_spec(0), pl.BlockSpec(gain.shape, const), pl.BlockSpec(tri.shape, const)],
        out_specs=[col_spec(0)],
        out_shape=[jax.ShapeDtypeStruct((batch, RES, per, GROUP_WIDTH), BF16)],
        scratch=[pltpu.VMEM((HEADS, HEAD_DIM, HEAD_DIM), F32)],
        tiles=tiles)


def _mixers(proj, keep, gain, *, hgrn_col_block):
    batch, _, per, _ = proj.shape
    rows = ATTN_SUB_BLOCKS * ATTN_BACK // RES
    steps = batch * per // rows
    parts = [_attn_operands(proj, g, dilation) for g, (_, dilation) in enumerate(ATTN_GROUPS)]
    parts.append(_hgrn_operands(proj, keep, gain, first_col_block=hgrn_col_block, rows=rows))
    n_in = [len(p["inputs"]) for p in parts]
    n_out = [len(p["out_shape"]) for p in parts]
    n_scr = [len(p["scratch"]) for p in parts]

    def kernel_body(*refs):
        ins, rest = refs[:sum(n_in)], refs[sum(n_in):]
        outs, scr = rest[:sum(n_out)], rest[sum(n_out):]
        i = pl.program_id(0)
        take = lambda seq, counts, k: seq[sum(counts[:k]):sum(counts[:k + 1])]
        for k, part in enumerate(parts[:-1]):
            _attn_carry(i % part["nblk"], *take(scr, n_scr, k))
        k = len(parts) - 1
        first = i % parts[k]["tiles"] == 0

        @pl.when(first)
        def _():
            st_ref = take(scr, n_scr, k)[0]
            st_ref[...] = jnp.zeros(st_ref.shape, F32)

        for k, part in enumerate(parts[:-1]):
            _attn_body(i % part["nblk"], *take(ins, n_in, k), *take(outs, n_out, k), *take(scr, n_scr, k),
                       rows_per_residue=part["rows_per_residue"])
        k = len(parts) - 1
        _hgrn_body(first, *take(ins, n_in, k), *take(outs, n_out, k), *take(scr, n_scr, k))

    flat = lambda key: [v for p in parts for v in p[key]]
    res = pl.pallas_call(
        kernel_body,
        grid=(steps,),
        in_specs=flat("in_specs"),
        out_specs=flat("out_specs"),
        out_shape=flat("out_shape"),
        scratch_shapes=flat("scratch"),
        compiler_params=_params(("arbitrary",)),
        name="token_mixers",
    )(*flat("inputs"))
    lse_w = HEADS * LSE_LANES
    outs = [res[2 * g].reshape(batch, RES, per, GROUP_WIDTH) for g in range(N_GROUPS)]
    lses = [res[2 * g + 1].reshape(batch, RES, per, lse_w) for g in range(N_GROUPS)]
    return outs, lses, res[2 * N_GROUPS]


def _merge_kernel(o1_ref, o2_ref, o3_ref, l1_ref, l2_ref, l3_ref, r_ref,
                  ga0_ref, ga1_ref, gr0_ref, gr1_ref, x_ref,
                  wpa_ref, wpr_ref, wo_ref, gpost_ref, gffn_ref,
                  h_ref, a2_ref):
    tm = x_ref.shape[0]
    ns = MERGE_ROW_SPLITS
    hm = tm // ns
    hp = hm // RES
    split = lambda ref: _sub_blocks(ref, slice(None), ns)
    o_parts = [split(ref) for ref in (o1_ref, o2_ref, o3_ref)]
    l_parts = [split(ref) for ref in (l1_ref, l2_ref, l3_ref)]
    r_parts = split(r_ref)
    gate_parts = [(split(ga0_ref), split(gr0_ref)), (split(ga1_ref), split(gr1_ref))]

    tok = lax.broadcasted_iota(jnp.int32, (hm, hm), 0)
    src = _block_position(lax.broadcasted_iota(jnp.int32, (hm, hm), 1), hp, hm)
    unpermute = jnp.where(src == tok, 1.0, 0.0).astype(BF16)
    half_w = GROUP_WIDTH
    col_halves = [slice(c * half_w, (c + 1) * half_w) for c in range(2)]

    for hf in range(ns):
        la, lb, lc = (lp[hf] for lp in l_parts)
        lm = jnp.maximum(la, jnp.maximum(lb, lc))
        ea, eb, ec = jnp.exp(la - lm), jnp.exp(lb - lm), jnp.exp(lc - lm)
        den = ea + eb + ec
        wa, wb, wc = ea / den, eb / den, ec / den
        o1, o2, o3 = (op[hf] for op in o_parts)
        heads = []
        for h in range(HEADS):
            sl = slice(h * HEAD_DIM, (h + 1) * HEAD_DIM)
            c0 = h * LSE_LANES
            heads.append((wa[:, c0:c0 + 1] * o1[:, sl].astype(F32)
                          + wb[:, c0:c0 + 1] * o2[:, sl].astype(F32)
                          + wc[:, c0:c0 + 1] * o3[:, sl].astype(F32)).astype(BF16))
        attn = jnp.concatenate(heads, axis=1)
        r = r_parts[hf]
        z = None
        for cs, (ga, gr) in zip(col_halves, gate_parts):
            ya = jnp.dot(attn, wpa_ref[:, cs], preferred_element_type=F32)
            yr = jnp.dot(r, wpr_ref[:, cs], preferred_element_type=F32)
            y = _sigmoid(ga[hf].astype(F32)) * ya + _sigmoid(gr[hf].astype(F32)) * yr
            y_nat = jnp.dot(unpermute, y.astype(BF16), preferred_element_type=F32).astype(BF16)
            part = jnp.dot(y_nat, wo_ref[cs, :], preferred_element_type=F32)
            z = part if z is None else z + part
        rows = slice(hf * hm, (hf + 1) * hm)
        h1 = x_ref[rows, :] + z * _rms_scale(z) * gpost_ref[...]
        h_ref[rows, :] = h1
        a2_ref[rows, :] = (h1 * _rms_scale(h1) * gffn_ref[...]).astype(BF16)


def _merge(outs, lses, r_out, proj, x2, wpa, wpr, wo, gpost, gffn, *, gate_col_block, tm=256):
    t, d = x2.shape
    batch, _, per, _ = proj.shape
    rows = tm // RES
    tiles_per_seq = per // rows
    const = lambda i: (0, 0)
    nat = pl.BlockSpec((tm, d), lambda i: (i, 0))

    def lay(width, col_block=0):
        return pl.BlockSpec((None, RES, rows, width),
                            lambda i: (i // tiles_per_seq, 0, i % tiles_per_seq, col_block))

    def resident(shape):
        return pl.BlockSpec(shape, const, pipeline_mode=pl.Buffered(1))

    lse_w = HEADS * LSE_LANES
    return pl.pallas_call(
        _merge_kernel,
        grid=(t // tm,),
        in_specs=[lay(GROUP_WIDTH)] * 3 + [lay(lse_w)] * 3 + [lay(GROUP_WIDTH)]
        + [lay(GROUP_WIDTH, gate_col_block + k) for k in range(4)]
        + [nat, resident(wpa.shape), resident(wpr.shape), resident(wo.shape),
           resident(gpost.shape), resident(gffn.shape)],
        out_specs=[nat, nat],
        out_shape=[jax.ShapeDtypeStruct((t, d), F32), jax.ShapeDtypeStruct((t, d), BF16)],
        compiler_params=_params(("parallel",)),
        name="merge_out_proj",
    )(*outs, *lses, r_out, proj, proj, proj, proj, x2, wpa, wpr, wo, gpost, gffn)


def _ffn_hidden_kernel(a_ref, wg_ref, wu_ref, o_ref, wg_b, wu_b):
    @pl.when(pl.program_id(1) == 0)
    def _():
        wg_b[...] = wg_ref[...].astype(BF16)
        wu_b[...] = wu_ref[...].astype(BF16)

    a = a_ref[...]
    gt = jnp.dot(a, wg_b[...], preferred_element_type=F32)
    up = jnp.dot(a, wu_b[...], preferred_element_type=F32)
    o_ref[...] = (_silu(gt) * up).astype(o_ref.dtype)


def _ffn_out_kernel(hid_ref, wd_ref, h_ref, gain_ref, o_ref):
    ff = jnp.dot(hid_ref[...], wd_ref[...], preferred_element_type=F32)
    o_ref[...] = h_ref[...] + ff * _rms_scale(ff) * gain_ref[...]


def _ffn(a2, w_gate_up, w_down, h1, gain, *, tm_hidden=1024, tf=512, tm_out=256):
    t, d = a2.shape
    d_ff = w_down.shape[0]
    nf = d_ff // tf
    hidden = pl.pallas_call(
        _ffn_hidden_kernel,
        grid=(nf, t // tm_hidden),
        in_specs=[
            pl.BlockSpec((tm_hidden, d), lambda j, i: (i, 0)),
            pl.BlockSpec((d, tf), lambda j, i: (0, j)),
            pl.BlockSpec((d, tf), lambda j, i: (0, nf + j)),
        ],
        out_specs=pl.BlockSpec((tm_hidden, tf), lambda j, i: (i, j)),
        out_shape=jax.ShapeDtypeStruct((t, d_ff), BF16),
        scratch_shapes=[pltpu.VMEM((d, tf), BF16), pltpu.VMEM((d, tf), BF16)],
        compiler_params=_params(("arbitrary", "arbitrary")),
        name="swiglu_hidden",
    )(a2, w_gate_up, w_gate_up)
    rows = pl.BlockSpec((tm_out, d), lambda i: (i, 0))
    return pl.pallas_call(
        _ffn_out_kernel,
        grid=(t // tm_out,),
        in_specs=[
            pl.BlockSpec((tm_out, d_ff), lambda i: (i, 0)),
            pl.BlockSpec((d_ff, d), lambda i: (0, 0), pipeline_mode=pl.Buffered(1)),
            rows,
            pl.BlockSpec((1, d), lambda i: (0, 0)),
        ],
        out_specs=rows,
        out_shape=jax.ShapeDtypeStruct((t, d), F32),
        compiler_params=_params(("parallel",)),
        name="swiglu_out",
    )(hidden, w_down, h1, gain)


def _rotary_tables(seq):
    inv_freq = ROPE_THETA ** (-np.arange(0, HEAD_DIM, 2, dtype=np.float64) / HEAD_DIM)
    ang = np.arange(seq, dtype=np.float64)[:, None] * inv_freq[None, :]
    cos = np.concatenate([np.cos(ang), np.cos(ang)], axis=-1)
    sin_signed = np.concatenate([-np.sin(ang), np.sin(ang)], axis=-1)
    lay = lambda tab: jnp.asarray(tab.reshape(seq // RES, RES, HEAD_DIM).transpose(1, 0, 2), F32)
    return lay(cos), lay(sin_signed)


def kernel(x, w_in, w_attn_branch, w_hgrn_branch, w_mix_out, hgrn_lower_bounds, hgrn_norm_gain,
           norm_mix_pre, norm_mix_post, w_ffn_gate_up, w_ffn_down, norm_ffn_pre, norm_ffn_post):
    batch, seq, d = x.shape
    assert w_in.shape[0] == 1, "single-layer block"
    x2 = x.reshape(batch * seq, d)
    row = lambda v: v[0].reshape(1, -1).astype(F32)

    for window, dilation in ATTN_GROUPS:
        assert window // dilation == ATTN_BACK and seq % window == 0 and RES % dilation == 0
    assert w_in.shape[2] == (QKV_BLOCKS + 8) * GROUP_WIDTH
    cos_r, sin_r = _rotary_tables(seq)
    proj, keep = _in_proj(x2, row(norm_mix_pre), w_in[0].astype(F32), cos_r, sin_r,
                          hgrn_lower_bounds.astype(F32), batch, seq)

    outs, lses, r_out = _mixers(proj, keep, row(hgrn_norm_gain), hgrn_col_block=QKV_BLOCKS)

    h1, a2 = _merge(outs, lses, r_out, proj, x2,
                    w_attn_branch[0].astype(BF16), w_hgrn_branch[0].astype(BF16),
                    w_mix_out[0].astype(BF16), row(norm_mix_post), row(norm_ffn_pre),
                    gate_col_block=QKV_BLOCKS + 4)

    out = _ffn(a2, w_ffn_gate_up[0].astype(F32), w_ffn_down[0].astype(BF16), h1, row(norm_ffn_post))
    return out.reshape(batch, seq, d)
```

```python
import math

import numpy as np
import jax
import jax.numpy as jnp
from jax import lax
from jax.experimental import pallas as pl
from jax.experimental.pallas import tpu as pltpu

F32 = jnp.float32
BF16 = jnp.bfloat16

HEAD_DIM = 128
ATTN_GROUPS = ((128, 1), (512, 4), (2048, 16))
N_GROUPS = len(ATTN_GROUPS)
HEADS = 8
GROUP_WIDTH = HEADS * HEAD_DIM
ROTARY_BLOCKS = 2 * N_GROUPS
QKV_BLOCKS = 3 * N_GROUPS
FORGET_BLOCK = QKV_BLOCKS + 1
ATTN_BACK = 128
ATTN_SUB_BLOCKS = 2
RES = 16
ROPE_THETA = 10000.0
NORM_EPS = 1e-6
HGRN_CHUNK = 128
HGRN_LEVELS = (64, 32, 16, 8, 4, 2, 1)
LSE_LANES = 16
BF16_SUBLANES = 16
NEG_BIG = -1e30
PERMUTE_GROUP = 256
LOG2E = math.log2(math.e)

VMEM_LIMIT_BYTES = 56 * 1024 * 1024


def _params(semantics):
    return pltpu.CompilerParams(dimension_semantics=semantics, vmem_limit_bytes=VMEM_LIMIT_BYTES)


def _rms_scale(v):
    return lax.rsqrt(jnp.mean(v * v, axis=-1, keepdims=True) + NORM_EPS)


def _sigmoid(v):
    return 0.5 * jnp.tanh(0.5 * v) + 0.5


def _silu(v):
    return v * _sigmoid(v)


def _block_position(i, rows_per_residue, block_rows):
    shift = rows_per_residue.bit_length() - 1
    n_res_shift = (block_rows // rows_per_residue).bit_length() - 1
    return ((i & (rows_per_residue - 1)) << n_res_shift) | lax.shift_right_logical(i, shift)


def _in_proj_kernel(x_ref, gain_ref, w_ref, cos_ref, sin_ref, hlb_ref, o_ref, keep_ref, a_ref):
    tm = x_ref.shape[0]
    per_res = tm // RES
    grp = PERMUTE_GROUP
    per_grp = grp // RES
    j = pl.program_id(1)

    @pl.when(j == 0)
    def _():
        row_token = _block_position(lax.broadcasted_iota(jnp.int32, (grp, grp), 0), per_grp, grp)
        permute = jnp.where(row_token == lax.broadcasted_iota(jnp.int32, (grp, grp), 1), 1.0, 0.0).astype(BF16)
        gain = gain_ref[...]
        for gi in range(tm // grp):
            xg = x_ref[gi * grp:(gi + 1) * grp, :]
            ag = (xg * _rms_scale(xg) * gain).astype(BF16)
            pg = jnp.dot(permute, ag, preferred_element_type=F32).astype(BF16)
            for r in range(RES):
                dst = r * per_res + gi * per_grp
                a_ref[dst:dst + per_grp, :] = pg[r * per_grp:(r + 1) * per_grp, :]

    def project():
        return jnp.dot(a_ref[...], w_ref[...].astype(BF16), preferred_element_type=F32)

    def store(ref, val):
        ref[...] = val.reshape(RES, per_res, val.shape[-1]).astype(ref.dtype)

    @pl.when(j < ROTARY_BLOCKS)
    def _():
        res = project()
        scale = jnp.where(j < N_GROUPS, HEAD_DIM ** -0.5 * LOG2E, 1.0)
        cos = cos_ref[...].reshape(tm, HEAD_DIM) * scale
        sin = sin_ref[...].reshape(tm, HEAD_DIM) * scale
        heads = []
        for h in range(HEADS):
            t = res[:, h * HEAD_DIM:(h + 1) * HEAD_DIM]
            heads.append(t * cos + pltpu.roll(t, HEAD_DIM // 2, 1) * sin)
        store(o_ref, jnp.concatenate(heads, axis=1))

    is_swish = (j == FORGET_BLOCK - 1) | (j == FORGET_BLOCK + 2)

    @pl.when(is_swish)
    def _():
        store(o_ref, _silu(project()))

    @pl.when((j >= ROTARY_BLOCKS) & (j != FORGET_BLOCK) & jnp.logical_not(is_swish))
    def _():
        store(o_ref, project())

    @pl.when(j == FORGET_BLOCK)
    def _():
        res = project()
        h0 = hlb_ref[0:1, :]
        h1 = hlb_ref[1:2, :]
        hm = jnp.maximum(h0, h1)
        e0 = jnp.exp(h0 - hm)
        e1 = jnp.exp(h1 - hm)
        lb = e0 / (e0 + e1)
        c1 = 0.5 * (1.0 - lb)
        c0 = 1.0 - c1
        c1_tanh = c1 * jnp.tanh(0.5 * res)
        store(o_ref, jnp.log(c0 + c1_tanh))
        store(keep_ref, c1 - c1_tanh)


def _in_proj(x2, gain, w, cos_r, sin_r, lower_bounds, batch, seq, *, tm=1024, tn=GROUP_WIDTH):
    assert lower_bounds.shape[0] == 2, "forget-gate bound rows: DEPTH + 1 with DEPTH == 1"
    t, d = x2.shape
    n = w.shape[1]
    tiles_per_seq = seq // tm
    table = pl.BlockSpec((RES, tm // RES, HEAD_DIM), lambda i, j: (0, i % tiles_per_seq, 0))
    lay = lambda col: pl.BlockSpec((None, RES, tm // RES, tn),
                                   lambda i, j: (i // tiles_per_seq, 0, i % tiles_per_seq, col(j)))
    return pl.pallas_call(
        _in_proj_kernel,
        grid=(t // tm, n // tn),
        in_specs=[
            pl.BlockSpec((tm, d), lambda i, j: (i, 0)),
            pl.BlockSpec((1, d), lambda i, j: (0, 0)),
            pl.BlockSpec((d, tn), lambda i, j: (0, j)),
            table, table,
            pl.BlockSpec(lower_bounds.shape, lambda i, j: (0, 0)),
        ],
        out_specs=[lay(lambda j: j), lay(lambda j: 0)],
        out_shape=[jax.ShapeDtypeStruct((batch, RES, seq // RES, n), BF16),
                   jax.ShapeDtypeStruct((batch, RES, seq // RES, tn), BF16)],
        scratch_shapes=[pltpu.VMEM((tm, d), BF16)],
        compiler_params=_params(("parallel", "arbitrary")),
        name="in_proj",
    )(x2, gain, w, cos_r, sin_r, lower_bounds)


def _sub_blocks(ref, sl, nsub):
    v = ref[:, :, sl]
    a, u, w = v.shape
    step = u // nsub
    if step % BF16_SUBLANES:
        v = v.astype(F32)
    return [v[:, s * step:(s + 1) * step, :].reshape(a * step, w).astype(ref.dtype) for s in range(nsub)]


def _store_sub_blocks(ref, sl, blocks):
    a = ref.shape[0]
    parts = [b.reshape(a, b.shape[0] // a, b.shape[1]) for b in blocks]
    ref[:, :, sl] = jnp.concatenate(parts, axis=1).astype(ref.dtype)


def _attn_carry(n, kk_ref, vv_ref):
    blk = ATTN_BACK
    nsub = ATTN_SUB_BLOCKS

    @pl.when(n == 0)
    def _():
        kk_ref[0:blk, :] = jnp.zeros((blk, GROUP_WIDTH), BF16)
        vv_ref[0:blk, :] = jnp.zeros((blk, GROUP_WIDTH), BF16)

    @pl.when(n > 0)
    def _():
        kk_ref[0:blk, :] = kk_ref[nsub * blk:(nsub + 1) * blk, :]
        vv_ref[0:blk, :] = vv_ref[nsub * blk:(nsub + 1) * blk, :]


def _attn_body(n, q_ref, k_ref, v_ref, o_ref, lse_ref, kk_ref, vv_ref, *, rows_per_residue):
    blk = ATTN_BACK
    nsub = ATTN_SUB_BLOCKS

    qi = lax.broadcasted_iota(jnp.int32, (blk, 2 * blk), 0)
    kj = lax.broadcasted_iota(jnp.int32, (blk, 2 * blk), 1)
    q_pos = _block_position(qi, rows_per_residue, blk)
    k_pos = _block_position(kj & (blk - 1), rows_per_residue, blk) - jnp.where(kj < blk, blk, 0)
    dist = q_pos - k_pos
    band = (dist >= 0) & (dist <= ATTN_BACK)
    first_mask = band & (kj >= jnp.where(n > 0, 0, blk))
    lane_head = lax.shift_right_logical(lax.broadcasted_iota(jnp.int32, (blk, HEADS * LSE_LANES), 1),
                                        LSE_LANES.bit_length() - 1)
    lse_tiles = [jnp.zeros((blk, HEADS * LSE_LANES), F32) for _ in range(nsub)]

    for h in range(HEADS):
        sl = slice(h * HEAD_DIM, (h + 1) * HEAD_DIM)
        for s, (kb, vb) in enumerate(zip(_sub_blocks(k_ref, sl, nsub), _sub_blocks(v_ref, sl, nsub))):
            rows = slice((s + 1) * blk, (s + 2) * blk)
            kk_ref[rows, sl] = kb
            vv_ref[rows, sl] = vb
        outs = []
        for s, qb in enumerate(_sub_blocks(q_ref, sl, nsub)):
            keys = slice(s * blk, (s + 2) * blk)
            sc = lax.dot_general(qb, kk_ref[keys, sl], (((1,), (1,)), ((), ())), preferred_element_type=F32)
            sc = jnp.where(first_mask if s == 0 else band, sc, NEG_BIG)
            m = jnp.max(sc, axis=-1, keepdims=True)
            p = jnp.exp2(sc - m)
            den = jnp.sum(p, axis=-1, keepdims=True)
            o = jnp.dot(p.astype(BF16), vv_ref[keys, sl], preferred_element_type=F32)
            outs.append(o / den)
            lse = m * math.log(2.0) + jnp.log(den)
            lse_tiles[s] = jnp.where(lane_head == h, lse, lse_tiles[s])
        _store_sub_blocks(o_ref, sl, outs)
    _store_sub_blocks(lse_ref, slice(None), lse_tiles)


def _attn_operands(proj, g, dilation):
    batch, _, per, _ = proj.shape
    a = RES // dilation
    rows = ATTN_BACK // a
    rows_blk = rows * ATTN_SUB_BLOCKS
    nblk = per // rows_blk
    view = lambda arr: arr.reshape(arr.shape[:-3] + (a, dilation) + arr.shape[-2:])

    def spec(width, col_block):
        return pl.BlockSpec((None, a, None, rows_blk, width),
                            lambda i: (i // (dilation * nblk), 0, (i // nblk) % dilation, i % nblk, col_block))

    lse_w = HEADS * LSE_LANES
    proj_v = view(proj)
    key_rows = (ATTN_SUB_BLOCKS + 1) * ATTN_BACK
    return dict(
        inputs=[proj_v, proj_v, proj_v],
        in_specs=[spec(GROUP_WIDTH, which * N_GROUPS + g) for which in range(3)],
        out_specs=[spec(GROUP_WIDTH, 0), spec(lse_w, 0)],
        out_shape=[jax.ShapeDtypeStruct((batch, a, dilation, per, GROUP_WIDTH), BF16),
                   jax.ShapeDtypeStruct((batch, a, dilation, per, lse_w), F32)],
        scratch=[pltpu.VMEM((key_rows, GROUP_WIDTH), BF16), pltpu.VMEM((key_rows, GROUP_WIDTH), BF16)],
        rows_per_residue=rows, nblk=nblk)


def _hgrn_cumsum_matrix():
    c = HGRN_CHUNK
    i = np.arange(c)
    per = c // RES
    pos = (i % per) * RES + i // per
    return (pos[None, :] <= pos[:, None]).astype(np.float32)


def _level_reference(b3, half):
    res, per, dk = b3.shape
    if half < res:
        two = 2 * half
        parts = [jnp.broadcast_to(b3[blk * two + half - 1][None], (two, per, dk)) for blk in range(res // two)]
        return parts[0] if len(parts) == 1 else jnp.concatenate(parts, axis=0)
    last = b3[res - 1]
    step = half // res
    u = lax.broadcasted_iota(jnp.int32, (per, dk), 0)
    ref = None
    for blk in reversed(range(per // (2 * step))):
        row = jnp.broadcast_to(last[blk * 2 * step + step - 1][None], (per, dk))
        ref = row if ref is None else jnp.where(u < (blk + 1) * 2 * step, row, ref)
    return jnp.broadcast_to(ref[None], (res, per, dk))


def _scale_valid_rows(t3, e, half, lower):
    res = t3.shape[0]
    if half >= res:
        return t3 * e
    parts = []
    for lo in range(0, res, 2 * half):
        up, dn = slice(lo, lo + half), slice(lo + half, lo + 2 * half)
        parts += [t3[up], t3[dn] * e[dn]] if lower else [t3[up] * e[up], t3[dn]]
    return jnp.concatenate(parts, axis=0)


def _hgrn_body(q_ref, logf_ref, v_ref, gate_ref, keep_ref, gain_ref, tri_ref, o_ref, st_ref):
    c = HGRN_CHUNK
    dk = HEAD_DIM
    per = c // RES

    g_all = logf_ref[...].astype(F32)
    k_all = keep_ref[...].astype(F32)
    q_all = q_ref[...].astype(F32)
    v_all = v_ref[...].astype(F32)
    gate_all = gate_ref[...].astype(F32)

    ti = _block_position(lax.broadcasted_iota(jnp.int32, (c, c), 0), per, c)
    si = _block_position(lax.broadcasted_iota(jnp.int32, (c, c), 1), per, c)
    diag_mask = ti == si
    level_masks = []
    for half in HGRN_LEVELS:
        shift = half.bit_length()
        level_masks.append((lax.shift_right_logical(ti, shift) == lax.shift_right_logical(si, shift))
                           & ((ti & half) != 0) & ((si & half) == 0))
    tri = tri_ref[...]

    n_chunks = o_ref.shape[1] // per
    outs = [[None] * n_chunks for _ in range(HEADS)]
    for ci in range(n_chunks):
        rows = slice(ci * per, (ci + 1) * per)
        g = g_all[:, rows, :].reshape(c, GROUP_WIDTH)
        g_hi = g.astype(BF16)
        r1 = g - g_hi.astype(F32)
        g_mid = r1.astype(BF16)
        g_lo = (r1 - g_mid.astype(F32)).astype(BF16)
        cum = (jnp.dot(tri, g_hi, preferred_element_type=F32)
               + jnp.dot(tri, g_mid, preferred_element_type=F32)
               + jnp.dot(tri, g_lo, preferred_element_type=F32))

        for h in range(HEADS):
            sl = slice(h * dk, (h + 1) * dk)
            q3 = q_all[:, rows, sl]
            k3 = k_all[:, rows, sl]
            q = q3.reshape(c, dk)
            k = k3.reshape(c, dk)
            v = v_all[:, rows, sl].reshape(c, dk).astype(BF16)
            b = cum[:, sl]
            b3 = b.reshape(RES, per, dk)
            b_last = b[c - 1:c, :]
            st = st_ref[h]

            q_dec = (q * jnp.exp(b)).astype(BF16)
            o = lax.dot_general(q_dec, st.astype(BF16), (((1,), (1,)), ((), ())),
                                preferred_element_type=F32)

            a = jnp.where(diag_mask, jnp.sum(q * k, axis=-1, keepdims=True), 0.0)
            for li, half in enumerate(HGRN_LEVELS):
                e = jnp.exp2(jnp.abs(b3 - _level_reference(b3, half)) * -LOG2E)
                q_l = _scale_valid_rows(q3, e, half, lower=True).reshape(c, dk).astype(BF16)
                k_l = _scale_valid_rows(k3, e, half, lower=False).reshape(c, dk).astype(BF16)
                p_l = lax.dot_general(q_l, k_l, (((1,), (1,)), ((), ())), preferred_element_type=F32)
                a = jnp.where(level_masks[li], p_l, a)
            o = o + jnp.dot(a.astype(BF16), v, preferred_element_type=F32)

            k_dec = (k * jnp.exp(b_last - b)).astype(BF16)
            st_ref[h] = st * jnp.exp(b_last) + lax.dot_general(
                v, k_dec, (((0,), (0,)), ((), ())), preferred_element_type=F32)

            r = o * _rms_scale(o) * gain_ref[:, sl] * gate_all[:, rows, sl].reshape(c, dk)
            outs[h][ci] = r.reshape(RES, per, dk)

    for h in range(HEADS):
        o_ref[:, :, h * dk:(h + 1) * dk] = jnp.concatenate(outs[h], axis=1).astype(o_ref.dtype)


def _hgrn_operands(proj, keep, gain, *, first_col_block, rows):
    batch, _, per, _ = proj.shape
    tiles = per // rows
    tri = jnp.asarray(_hgrn_cumsum_matrix(), BF16)

    def col_spec(col_block):
        return pl.BlockSpec((None, RES, rows, GROUP_WIDTH), lambda i: (i // tiles, 0, i % tiles, col_block))

    const = lambda i: (0, 0)
    return dict(
        inputs=[proj, proj, proj, proj, keep, gain, tri],
        in_specs=[col_spec(first_col_block + k) for k in range(4)]
        + [col_spec(0), pl.BlockSpec(gain.shape, const), pl.BlockSpec(tri.shape, const)],
        out_specs=[col_spec(0)],
        out_shape=[jax.ShapeDtypeStruct((batch, RES, per, GROUP_WIDTH), BF16)],
        scratch=[pltpu.VMEM((HEADS, HEAD_DIM, HEAD_DIM), F32)],
        tiles=tiles)


def _mixers(proj, keep, gain, *, hgrn_col_block):
    batch, _, per, _ = proj.shape
    rows = ATTN_SUB_BLOCKS * ATTN_BACK // RES
    steps = batch * per // rows
    parts = [_attn_operands(proj, g, dilation) for g, (_, dilation) in enumerate(ATTN_GROUPS)]
    parts.append(_hgrn_operands(proj, keep, gain, first_col_block=hgrn_col_block, rows=rows))
    n_in = [len(p["inputs"]) for p in parts]
    n_out = [len(p["out_shape"]) for p in parts]
    n_scr = [len(p["scratch"]) for p in parts]

    def kernel_body(*refs):
        ins, rest = refs[:sum(n_in)], refs[sum(n_in):]
        outs, scr = rest[:sum(n_out)], rest[sum(n_out):]
        i = pl.program_id(0)
        take = lambda seq, counts, k: seq[sum(counts[:k]):sum(counts[:k + 1])]
        for k, part in enumerate(parts[:-1]):
            _attn_carry(i % part["nblk"], *take(scr, n_scr, k))
        k = len(parts) - 1

        @pl.when(i % parts[k]["tiles"] == 0)
        def _():
            st_ref = take(scr, n_scr, k)[0]
            st_ref[...] = jnp.zeros(st_ref.shape, F32)

        for k, part in enumerate(parts[:-1]):
            _attn_body(i % part["nblk"], *take(ins, n_in, k), *take(outs, n_out, k), *take(scr, n_scr, k),
                       rows_per_residue=part["rows_per_residue"])
        k = len(parts) - 1
        _hgrn_body(*take(ins, n_in, k), *take(outs, n_out, k), *take(scr, n_scr, k))

    flat = lambda key: [v for p in parts for v in p[key]]
    res = pl.pallas_call(
        kernel_body,
        grid=(steps,),
        in_specs=flat("in_specs"),
        out_specs=flat("out_specs"),
        out_shape=flat("out_shape"),
        scratch_shapes=flat("scratch"),
        compiler_params=_params(("arbitrary",)),
        name="token_mixers",
    )(*flat("inputs"))
    lse_w = HEADS * LSE_LANES
    outs = [res[2 * g].reshape(batch, RES, per, GROUP_WIDTH) for g in range(N_GROUPS)]
    lses = [res[2 * g + 1].reshape(batch, RES, per, lse_w) for g in range(N_GROUPS)]
    return outs, lses, res[2 * N_GROUPS]


def _merge_kernel(o1_ref, o2_ref, o3_ref, l1_ref, l2_ref, l3_ref, r_ref,
                  ga0_ref, ga1_ref, gr0_ref, gr1_ref, x_ref,
                  wpa_ref, wpr_ref, wo_ref, gpost_ref, gffn_ref,
                  h_ref, a2_ref, attn_s):
    tm = x_ref.shape[0]
    per = tm // RES
    flat = lambda ref: ref[...].reshape(tm, ref.shape[-1])

    la, lb, lc = flat(l1_ref), flat(l2_ref), flat(l3_ref)
    lm = jnp.maximum(la, jnp.maximum(lb, lc))
    ea, eb, ec = jnp.exp(la - lm), jnp.exp(lb - lm), jnp.exp(lc - lm)
    den = ea + eb + ec
    wa, wb, wc = ea / den, eb / den, ec / den
    o1, o2, o3 = flat(o1_ref), flat(o2_ref), flat(o3_ref)
    for h in range(HEADS):
        sl = slice(h * HEAD_DIM, (h + 1) * HEAD_DIM)
        c0 = h * LSE_LANES
        mixed = (wa[:, c0:c0 + 1] * o1[:, sl].astype(F32)
                 + wb[:, c0:c0 + 1] * o2[:, sl].astype(F32)
                 + wc[:, c0:c0 + 1] * o3[:, sl].astype(F32))
        attn_s[:, sl] = mixed.astype(BF16)

    tok = lax.broadcasted_iota(jnp.int32, (tm, tm), 0)
    src = _block_position(lax.broadcasted_iota(jnp.int32, (tm, tm), 1), per, tm)
    unpermute = jnp.where(src == tok, 1.0, 0.0).astype(BF16)

    attn = attn_s[...]
    r = flat(r_ref)
    half_w = GROUP_WIDTH
    halves = [slice(half * half_w, (half + 1) * half_w) for half in range(2)]
    branch = [(jnp.dot(attn, wpa_ref[:, cs], preferred_element_type=F32),
               jnp.dot(r, wpr_ref[:, cs], preferred_element_type=F32)) for cs in halves]
    z = None
    for cs, (ya, yr), (ga_ref, gr_ref) in zip(halves, branch, ((ga0_ref, gr0_ref), (ga1_ref, gr1_ref))):
        y = _sigmoid(flat(ga_ref).astype(F32)) * ya + _sigmoid(flat(gr_ref).astype(F32)) * yr
        y_nat = jnp.dot(unpermute, y.astype(BF16), preferred_element_type=F32).astype(BF16)
        part = jnp.dot(y_nat, wo_ref[cs, :], preferred_element_type=F32)
        z = part if z is None else z + part
    h1 = x_ref[...] + z * _rms_scale(z) * gpost_ref[...]
    h_ref[...] = h1
    a2_ref[...] = (h1 * _rms_scale(h1) * gffn_ref[...]).astype(BF16)


def _merge(outs, lses, r_out, proj, x2, wpa, wpr, wo, gpost, gffn, *, gate_col_block, tm=256):
    t, d = x2.shape
    batch, _, per, _ = proj.shape
    rows = tm // RES
    tiles_per_seq = per // rows
    const = lambda i: (0, 0)
    nat = pl.BlockSpec((tm, d), lambda i: (i, 0))

    def lay(width, col_block=0):
        return pl.BlockSpec((None, RES, rows, width),
                            lambda i: (i // tiles_per_seq, 0, i % tiles_per_seq, col_block))

    def resident(shape):
        return pl.BlockSpec(shape, const, pipeline_mode=pl.Buffered(1))

    lse_w = HEADS * LSE_LANES
    return pl.pallas_call(
        _merge_kernel,
        grid=(t // tm,),
        in_specs=[lay(GROUP_WIDTH)] * 3 + [lay(lse_w)] * 3 + [lay(GROUP_WIDTH)]
        + [lay(GROUP_WIDTH, gate_col_block + k) for k in range(4)]
        + [nat, resident(wpa.shape), resident(wpr.shape), resident(wo.shape),
           resident(gpost.shape), resident(gffn.shape)],
        out_specs=[nat, nat],
        out_shape=[jax.ShapeDtypeStruct((t, d), F32), jax.ShapeDtypeStruct((t, d), BF16)],
        scratch_shapes=[pltpu.VMEM((tm, GROUP_WIDTH), BF16)],
        compiler_params=_params(("parallel",)),
        name="merge_out_proj",
    )(*outs, *lses, r_out, proj, proj, proj, proj, x2, wpa, wpr, wo, gpost, gffn)


def _ffn_hidden_kernel(a_ref, wg_ref, wu_ref, o_ref, wg_b, wu_b):
    @pl.when(pl.program_id(1) == 0)
    def _():
        wg_b[...] = wg_ref[...].astype(BF16)
        wu_b[...] = wu_ref[...].astype(BF16)

    a = a_ref[...]
    gt = jnp.dot(a, wg_b[...], preferred_element_type=F32)
    up = jnp.dot(a, wu_b[...], preferred_element_type=F32)
    o_ref[...] = (_silu(gt) * up).astype(o_ref.dtype)


def _ffn_out_kernel(hid_ref, wd_ref, h_ref, gain_ref, o_ref):
    ff = jnp.dot(hid_ref[...], wd_ref[...], preferred_element_type=F32)
    o_ref[...] = h_ref[...] + ff * _rms_scale(ff) * gain_ref[...]


def _ffn(a2, w_gate_up, w_down, h1, gain, *, tm_hidden=1024, tf=512, tm_out=256):
    t, d = a2.shape
    d_ff = w_down.shape[0]
    nf = d_ff // tf
    hidden = pl.pallas_call(
        _ffn_hidden_kernel,
        grid=(nf, t // tm_hidden),
        in_specs=[
            pl.BlockSpec((tm_hidden, d), lambda j, i: (i, 0)),
            pl.BlockSpec((d, tf), lambda j, i: (0, j)),
            pl.BlockSpec((d, tf), lambda j, i: (0, nf + j)),
        ],
        out_specs=pl.BlockSpec((tm_hidden, tf), lambda j, i: (i, j)),
        out_shape=jax.ShapeDtypeStruct((t, d_ff), BF16),
        scratch_shapes=[pltpu.VMEM((d, tf), BF16), pltpu.VMEM((d, tf), BF16)],
        compiler_params=_params(("arbitrary", "arbitrary")),
        name="swiglu_hidden",
    )(a2, w_gate_up, w_gate_up)
    rows = pl.BlockSpec((tm_out, d), lambda i: (i, 0))
    return pl.pallas_call(
        _ffn_out_kernel,
        grid=(t // tm_out,),
        in_specs=[
            pl.BlockSpec((tm_out, d_ff), lambda i: (i, 0)),
            pl.BlockSpec((d_ff, d), lambda i: (0, 0), pipeline_mode=pl.Buffered(1)),
            rows,
            pl.BlockSpec((1, d), lambda i: (0, 0)),
        ],
        out_specs=rows,
        out_shape=jax.ShapeDtypeStruct((t, d), F32),
        compiler_params=_params(("parallel",)),
        name="swiglu_out",
    )(hidden, w_down, h1, gain)


def _rotary_tables(seq):
    inv_freq = ROPE_THETA ** (-np.arange(0, HEAD_DIM, 2, dtype=np.float64) / HEAD_DIM)
    ang = np.arange(seq, dtype=np.float64)[:, None] * inv_freq[None, :]
    cos = np.concatenate([np.cos(ang), np.cos(ang)], axis=-1)
    sin_signed = np.concatenate([-np.sin(ang), np.sin(ang)], axis=-1)
    lay = lambda tab: jnp.asarray(tab.reshape(seq // RES, RES, HEAD_DIM).transpose(1, 0, 2), F32)
    return lay(cos), lay(sin_signed)


def kernel(x, w_in, w_attn_branch, w_hgrn_branch, w_mix_out, hgrn_lower_bounds, hgrn_norm_gain,
           norm_mix_pre, norm_mix_post, w_ffn_gate_up, w_ffn_down, norm_ffn_pre, norm_ffn_post):
    batch, seq, d = x.shape
    assert w_in.shape[0] == 1, "single-layer block"
    x2 = x.reshape(batch * seq, d)
    row = lambda v: v[0].reshape(1, -1).astype(F32)

    for window, dilation in ATTN_GROUPS:
        assert window // dilation == ATTN_BACK and seq % window == 0 and RES % dilation == 0
    assert w_in.shape[2] == (QKV_BLOCKS + 8) * GROUP_WIDTH
    cos_r, sin_r = _rotary_tables(seq)
    proj, keep = _in_proj(x2, row(norm_mix_pre), w_in[0].astype(F32), cos_r, sin_r,
                          hgrn_lower_bounds.astype(F32), batch, seq)

    outs, lses, r_out = _mixers(proj, keep, row(hgrn_norm_gain), hgrn_col_block=QKV_BLOCKS)

    h1, a2 = _merge(outs, lses, r_out, proj, x2,
                    w_attn_branch[0].astype(BF16), w_hgrn_branch[0].astype(BF16),
                    w_mix_out[0].astype(BF16), row(norm_mix_post), row(norm_ffn_pre),
                    gate_col_block=QKV_BLOCKS + 4)

    out = _ffn(a2, w_ffn_gate_up[0].astype(F32), w_ffn_down[0].astype(BF16), h1, row(norm_ffn_post))
    return out.reshape(batch, seq, d)
```

```python
import math

import numpy as np
import jax
import jax.numpy as jnp
from jax import lax
from jax.experimental import pallas as pl
from jax.experimental.pallas import tpu as pltpu

F32 = jnp.float32
BF16 = jnp.bfloat16

HEAD_DIM = 128
ATTN_GROUPS = ((128, 1), (512, 4), (2048, 16))
N_GROUPS = len(ATTN_GROUPS)
HEADS = 8
GROUP_WIDTH = HEADS * HEAD_DIM
ROTARY_BLOCKS = 2 * N_GROUPS
QKV_BLOCKS = 3 * N_GROUPS
FORGET_BLOCK = QKV_BLOCKS + 1
ATTN_BACK = 128
ATTN_SUB_BLOCKS = 2
RES = 16
ROPE_THETA = 10000.0
NORM_EPS = 1e-6
HGRN_CHUNK = 128
HGRN_LEVELS = (64, 32, 16, 8, 4, 2, 1)
LSE_LANES = 16
BF16_SUBLANES = 16
NEG_BIG = -1e30
PERMUTE_GROUP = 256
LOG2E = math.log2(math.e)

VMEM_LIMIT_BYTES = 56 * 1024 * 1024


def _params(semantics):
    return pltpu.CompilerParams(dimension_semantics=semantics, vmem_limit_bytes=VMEM_LIMIT_BYTES)


def _rms_scale(v):
    return lax.rsqrt(jnp.mean(v * v, axis=-1, keepdims=True) + NORM_EPS)


def _sigmoid(v):
    return 0.5 * jnp.tanh(0.5 * v) + 0.5


def _silu(v):
    return v * _sigmoid(v)


def _side_cast_specs(weights, steps, step_index):
    specs = []
    for w in weights:
        rows, cols = w.shape
        block_rows = BF16_SUBLANES * pl.cdiv(rows, BF16_SUBLANES * steps)
        assert rows % block_rows == 0
        last = rows // block_rows - 1
        specs.append(pl.BlockSpec((block_rows, cols),
                                  lambda *idx, last=last: (jnp.minimum(step_index(*idx), last), 0)))
    return specs


def _block_position(i, rows_per_residue, block_rows):
    shift = rows_per_residue.bit_length() - 1
    n_res_shift = (block_rows // rows_per_residue).bit_length() - 1
    return ((i & (rows_per_residue - 1)) << n_res_shift) | lax.shift_right_logical(i, shift)


def _in_proj_kernel(x_ref, gain_ref, w_ref, cos_ref, sin_ref, hlb_ref, o_ref, keep_ref, a_ref):
    tm = x_ref.shape[0]
    per_res = tm // RES
    grp = PERMUTE_GROUP
    per_grp = grp // RES
    j = pl.program_id(1)

    @pl.when(j == 0)
    def _():
        row_token = _block_position(lax.broadcasted_iota(jnp.int32, (grp, grp), 0), per_grp, grp)
        permute = jnp.where(row_token == lax.broadcasted_iota(jnp.int32, (grp, grp), 1), 1.0, 0.0).astype(BF16)
        gain = gain_ref[...]
        for gi in range(tm // grp):
            xg = x_ref[gi * grp:(gi + 1) * grp, :]
            ag = (xg * _rms_scale(xg) * gain).astype(BF16)
            pg = jnp.dot(permute, ag, preferred_element_type=F32).astype(BF16)
            for r in range(RES):
                dst = r * per_res + gi * per_grp
                a_ref[dst:dst + per_grp, :] = pg[r * per_grp:(r + 1) * per_grp, :]

    def project():
        return jnp.dot(a_ref[...], w_ref[...].astype(BF16), preferred_element_type=F32)

    def store(ref, val):
        ref[...] = val.reshape(RES, per_res, val.shape[-1]).astype(ref.dtype)

    @pl.when(j < ROTARY_BLOCKS)
    def _():
        res = project()
        scale = jnp.where(j < N_GROUPS, HEAD_DIM ** -0.5 * LOG2E, 1.0)
        cos = cos_ref[...].reshape(tm, HEAD_DIM) * scale
        sin = sin_ref[...].reshape(tm, HEAD_DIM) * scale
        heads = []
        for h in range(HEADS):
            t = res[:, h * HEAD_DIM:(h + 1) * HEAD_DIM]
            heads.append(t * cos + pltpu.roll(t, HEAD_DIM // 2, 1) * sin)
        store(o_ref, jnp.concatenate(heads, axis=1))

    is_swish = (j == FORGET_BLOCK - 1) | (j == FORGET_BLOCK + 2)

    @pl.when(is_swish)
    def _():
        store(o_ref, _silu(project()))

    @pl.when((j >= ROTARY_BLOCKS) & (j != FORGET_BLOCK) & jnp.logical_not(is_swish))
    def _():
        store(o_ref, project())

    @pl.when(j == FORGET_BLOCK)
    def _():
        res = project()
        h0 = hlb_ref[0:1, :]
        h1 = hlb_ref[1:2, :]
        hm = jnp.maximum(h0, h1)
        e0 = jnp.exp(h0 - hm)
        e1 = jnp.exp(h1 - hm)
        lb = e0 / (e0 + e1)
        f = lb + (1.0 - lb) * jax.nn.sigmoid(res)
        store(o_ref, jnp.log(f))
        store(keep_ref, 1.0 - f)


def _in_proj(x2, gain, w, cos_r, sin_r, lower_bounds, batch, seq, *, tm=1024, tn=GROUP_WIDTH):
    assert lower_bounds.shape[0] == 2, "forget-gate bound rows: DEPTH + 1 with DEPTH == 1"
    t, d = x2.shape
    n = w.shape[1]
    tiles_per_seq = seq // tm
    table = pl.BlockSpec((RES, tm // RES, HEAD_DIM), lambda i, j: (0, i % tiles_per_seq, 0))
    lay = lambda col: pl.BlockSpec((None, RES, tm // RES, tn),
                                   lambda i, j: (i // tiles_per_seq, 0, i % tiles_per_seq, col(j)))
    return pl.pallas_call(
        _in_proj_kernel,
        grid=(t // tm, n // tn),
        in_specs=[
            pl.BlockSpec((tm, d), lambda i, j: (i, 0)),
            pl.BlockSpec((1, d), lambda i, j: (0, 0)),
            pl.BlockSpec((d, tn), lambda i, j: (0, j)),
            table, table,
            pl.BlockSpec(lower_bounds.shape, lambda i, j: (0, 0)),
        ],
        out_specs=[lay(lambda j: j), lay(lambda j: 0)],
        out_shape=[jax.ShapeDtypeStruct((batch, RES, seq // RES, n), BF16),
                   jax.ShapeDtypeStruct((batch, RES, seq // RES, tn), BF16)],
        scratch_shapes=[pltpu.VMEM((tm, d), BF16)],
        compiler_params=_params(("parallel", "arbitrary")),
        name="in_proj",
    )(x2, gain, w, cos_r, sin_r, lower_bounds)


def _sub_blocks(ref, sl, nsub):
    v = ref[:, :, sl]
    a, u, w = v.shape
    step = u // nsub
    if step % BF16_SUBLANES:
        v = v.astype(F32)
    return [v[:, s * step:(s + 1) * step, :].reshape(a * step, w).astype(ref.dtype) for s in range(nsub)]


def _store_sub_blocks(ref, sl, blocks):
    a = ref.shape[0]
    parts = [b.reshape(a, b.shape[0] // a, b.shape[1]) for b in blocks]
    ref[:, :, sl] = jnp.concatenate(parts, axis=1).astype(ref.dtype)


def _attn_carry(n, kk_ref, vv_ref):
    blk = ATTN_BACK
    nsub = ATTN_SUB_BLOCKS

    @pl.when(n == 0)
    def _():
        kk_ref[0:blk, :] = jnp.zeros((blk, GROUP_WIDTH), BF16)
        vv_ref[0:blk, :] = jnp.zeros((blk, GROUP_WIDTH), BF16)

    @pl.when(n > 0)
    def _():
        kk_ref[0:blk, :] = kk_ref[nsub * blk:(nsub + 1) * blk, :]
        vv_ref[0:blk, :] = vv_ref[nsub * blk:(nsub + 1) * blk, :]


def _attn_body(n, q_ref, k_ref, v_ref, o_ref, lse_ref, kk_ref, vv_ref, *, rows_per_residue):
    blk = ATTN_BACK
    nsub = ATTN_SUB_BLOCKS

    qi = lax.broadcasted_iota(jnp.int32, (blk, 2 * blk), 0)
    kj = lax.broadcasted_iota(jnp.int32, (blk, 2 * blk), 1)
    q_pos = _block_position(qi, rows_per_residue, blk)
    k_pos = _block_position(kj & (blk - 1), rows_per_residue, blk) - jnp.where(kj < blk, blk, 0)
    dist = q_pos - k_pos
    band = (dist >= 0) & (dist <= ATTN_BACK)
    first_mask = band & (kj >= jnp.where(n > 0, 0, blk))
    lane_head = lax.shift_right_logical(lax.broadcasted_iota(jnp.int32, (blk, HEADS * LSE_LANES), 1),
                                        LSE_LANES.bit_length() - 1)
    lse_tiles = [jnp.zeros((blk, HEADS * LSE_LANES), F32) for _ in range(nsub)]

    for h in range(HEADS):
        sl = slice(h * HEAD_DIM, (h + 1) * HEAD_DIM)
        for s, (kb, vb) in enumerate(zip(_sub_blocks(k_ref, sl, nsub), _sub_blocks(v_ref, sl, nsub))):
            rows = slice((s + 1) * blk, (s + 2) * blk)
            kk_ref[rows, sl] = kb
            vv_ref[rows, sl] = vb
        outs = []
        for s, qb in enumerate(_sub_blocks(q_ref, sl, nsub)):
            keys = slice(s * blk, (s + 2) * blk)
            sc = lax.dot_general(qb, kk_ref[keys, sl], (((1,), (1,)), ((), ())), preferred_element_type=F32)
            sc = jnp.where(first_mask if s == 0 else band, sc, NEG_BIG)
            m = jnp.max(sc, axis=-1, keepdims=True)
            p = jnp.exp2(sc - m)
            den = jnp.sum(p, axis=-1, keepdims=True)
            o = jnp.dot(p.astype(BF16), vv_ref[keys, sl], preferred_element_type=F32)
            outs.append(o / den)
            lse = m * math.log(2.0) + jnp.log(den)
            lse_tiles[s] = jnp.where(lane_head == h, lse, lse_tiles[s])
        _store_sub_blocks(o_ref, sl, outs)
    _store_sub_blocks(lse_ref, slice(None), lse_tiles)


def _attn_operands(proj, g, dilation):
    batch, _, per, _ = proj.shape
    a = RES // dilation
    rows = ATTN_BACK // a
    rows_blk = rows * ATTN_SUB_BLOCKS
    nblk = per // rows_blk
    view = lambda arr: arr.reshape(arr.shape[:-3] + (a, dilation) + arr.shape[-2:])

    def spec(width, col_block):
        return pl.BlockSpec((None, a, None, rows_blk, width),
                            lambda i: (i // (dilation * nblk), 0, (i // nblk) % dilation, i % nblk, col_block))

    lse_w = HEADS * LSE_LANES
    proj_v = view(proj)
    key_rows = (ATTN_SUB_BLOCKS + 1) * ATTN_BACK
    return dict(
        inputs=[proj_v, proj_v, proj_v],
        in_specs=[spec(GROUP_WIDTH, which * N_GROUPS + g) for which in range(3)],
        out_specs=[spec(GROUP_WIDTH, 0), spec(lse_w, 0)],
        out_shape=[jax.ShapeDtypeStruct((batch, a, dilation, per, GROUP_WIDTH), BF16),
                   jax.ShapeDtypeStruct((batch, a, dilation, per, lse_w), F32)],
        scratch=[pltpu.VMEM((key_rows, GROUP_WIDTH), BF16), pltpu.VMEM((key_rows, GROUP_WIDTH), BF16)],
        rows_per_residue=rows, nblk=nblk)


def _hgrn_cumsum_matrix():
    c = HGRN_CHUNK
    i = np.arange(c)
    per = c // RES
    pos = (i % per) * RES + i // per
    return (pos[None, :] <= pos[:, None]).astype(np.float32)


def _level_reference(b3, half):
    res, per, dk = b3.shape
    if half < res:
        two = 2 * half
        parts = [jnp.broadcast_to(b3[blk * two + half - 1][None], (two, per, dk)) for blk in range(res // two)]
        return parts[0] if len(parts) == 1 else jnp.concatenate(parts, axis=0)
    last = b3[res - 1]
    step = half // res
    u = lax.broadcasted_iota(jnp.int32, (per, dk), 0)
    ref = None
    for blk in reversed(range(per // (2 * step))):
        row = jnp.broadcast_to(last[blk * 2 * step + step - 1][None], (per, dk))
        ref = row if ref is None else jnp.where(u < (blk + 1) * 2 * step, row, ref)
    return jnp.broadcast_to(ref[None], (res, per, dk))


def _scale_valid_rows(t3, e, half, lower):
    res = t3.shape[0]
    if half >= res:
        return t3 * e
    parts = []
    for lo in range(0, res, 2 * half):
        up, dn = slice(lo, lo + half), slice(lo + half, lo + 2 * half)
        parts += [t3[up], t3[dn] * e[dn]] if lower else [t3[up] * e[up], t3[dn]]
    return jnp.concatenate(parts, axis=0)


def _hgrn_body(q_ref, logf_ref, v_ref, gate_ref, keep_ref, gain_ref, tri_ref, o_ref, st_ref):
    c = HGRN_CHUNK
    dk = HEAD_DIM
    per = c // RES

    g_all = logf_ref[...].astype(F32)
    k_all = keep_ref[...].astype(F32)
    q_all = q_ref[...].astype(F32)
    v_all = v_ref[...].astype(F32)
    gate_all = gate_ref[...].astype(F32)

    ti = _block_position(lax.broadcasted_iota(jnp.int32, (c, c), 0), per, c)
    si = _block_position(lax.broadcasted_iota(jnp.int32, (c, c), 1), per, c)
    diag_mask = ti == si
    level_masks = []
    for half in HGRN_LEVELS:
        shift = half.bit_length()
        level_masks.append((lax.shift_right_logical(ti, shift) == lax.shift_right_logical(si, shift))
                           & ((ti & half) != 0) & ((si & half) == 0))
    tri = tri_ref[...]

    n_chunks = o_ref.shape[1] // per
    outs = [[None] * n_chunks for _ in range(HEADS)]
    for ci in range(n_chunks):
        rows = slice(ci * per, (ci + 1) * per)
        g = g_all[:, rows, :].reshape(c, GROUP_WIDTH)
        g_hi = g.astype(BF16)
        r1 = g - g_hi.astype(F32)
        g_mid = r1.astype(BF16)
        g_lo = (r1 - g_mid.astype(F32)).astype(BF16)
        cum = (jnp.dot(tri, g_hi, preferred_element_type=F32)
               + jnp.dot(tri, g_mid, preferred_element_type=F32)
               + jnp.dot(tri, g_lo, preferred_element_type=F32))

        for h in range(HEADS):
            sl = slice(h * dk, (h + 1) * dk)
            q3 = q_all[:, rows, sl]
            k3 = k_all[:, rows, sl]
            q = q3.reshape(c, dk)
            k = k3.reshape(c, dk)
            v = v_all[:, rows, sl].reshape(c, dk).astype(BF16)
            b = cum[:, sl]
            b3 = b.reshape(RES, per, dk)
            b_last = b[c - 1:c, :]
            st = st_ref[h]

            q_dec = (q * jnp.exp(b)).astype(BF16)
            o = lax.dot_general(q_dec, st.astype(BF16), (((1,), (1,)), ((), ())),
                                preferred_element_type=F32)

            a = jnp.where(diag_mask, jnp.sum(q * k, axis=-1, keepdims=True), 0.0)
            for li, half in enumerate(HGRN_LEVELS):
                e = jnp.exp2(jnp.abs(b3 - _level_reference(b3, half)) * -LOG2E)
                q_l = _scale_valid_rows(q3, e, half, lower=True).reshape(c, dk).astype(BF16)
                k_l = _scale_valid_rows(k3, e, half, lower=False).reshape(c, dk).astype(BF16)
                p_l = lax.dot_general(q_l, k_l, (((1,), (1,)), ((), ())), preferred_element_type=F32)
                a = jnp.where(level_masks[li], p_l, a)
            o = o + jnp.dot(a.astype(BF16), v, preferred_element_type=F32)

            k_dec = (k * jnp.exp(b_last - b)).astype(BF16)
            st_ref[h] = st * jnp.exp(b_last) + lax.dot_general(
                v, k_dec, (((0,), (0,)), ((), ())), preferred_element_type=F32)

            r = o * _rms_scale(o) * gain_ref[:, sl] * gate_all[:, rows, sl].reshape(c, dk)
            outs[h][ci] = r.reshape(RES, per, dk)

    for h in range(HEADS):
        o_ref[:, :, h * dk:(h + 1) * dk] = jnp.concatenate(outs[h], axis=1).astype(o_ref.dtype)


def _hgrn_operands(proj, keep, gain, *, first_col_block, rows):
    batch, _, per, _ = proj.shape
    tiles = per // rows
    tri = jnp.asarray(_hgrn_cumsum_matrix(), BF16)

    def col_spec(col_block):
        return pl.BlockSpec((None, RES, rows, GROUP_WIDTH), lambda i: (i // tiles, 0, i % tiles, col_block))

    const = lambda i: (0, 0)
    return dict(
        inputs=[proj, proj, proj, proj, keep, gain, tri],
        in_specs=[col_spec(first_col_block + k) for k in range(4)]
        + [col_spec(0), pl.BlockSpec(gain.shape, const), pl.BlockSpec(tri.shape, const)],
        out_specs=[col_spec(0)],
        out_shape=[jax.ShapeDtypeStruct((batch, RES, per, GROUP_WIDTH), BF16)],
        scratch=[pltpu.VMEM((HEADS, HEAD_DIM, HEAD_DIM), F32)],
        tiles=tiles)


def _mixers(proj, keep, gain, *, hgrn_col_block):
    batch, _, per, _ = proj.shape
    rows = ATTN_SUB_BLOCKS * ATTN_BACK // RES
    steps = batch * per // rows
    parts = [_attn_operands(proj, g, dilation) for g, (_, dilation) in enumerate(ATTN_GROUPS)]
    parts.append(_hgrn_operands(proj, keep, gain, first_col_block=hgrn_col_block, rows=rows))
    n_in = [len(p["inputs"]) for p in parts]
    n_out = [len(p["out_shape"]) for p in parts]
    n_scr = [len(p["scratch"]) for p in parts]

    def kernel_body(*refs):
        ins, rest = refs[:sum(n_in)], refs[sum(n_in):]
        outs, scr = rest[:sum(n_out)], rest[sum(n_out):]
        i = pl.program_id(0)
        take = lambda seq, counts, k: seq[sum(counts[:k]):sum(counts[:k + 1])]
        for k, part in enumerate(parts[:-1]):
            _attn_carry(i % part["nblk"], *take(scr, n_scr, k))
        k = len(parts) - 1

        @pl.when(i % parts[k]["tiles"] == 0)
        def _():
            st_ref = take(scr, n_scr, k)[0]
            st_ref[...] = jnp.zeros(st_ref.shape, F32)

        for k, part in enumerate(parts[:-1]):
            _attn_body(i % part["nblk"], *take(ins, n_in, k), *take(outs, n_out, k), *take(scr, n_scr, k),
                       rows_per_residue=part["rows_per_residue"])
        k = len(parts) - 1
        _hgrn_body(*take(ins, n_in, k), *take(outs, n_out, k), *take(scr, n_scr, k))

    flat = lambda key: [v for p in parts for v in p[key]]
    res = pl.pallas_call(
        kernel_body,
        grid=(steps,),
        in_specs=flat("in_specs"),
        out_specs=flat("out_specs"),
        out_shape=flat("out_shape"),
        scratch_shapes=flat("scratch"),
        compiler_params=_params(("arbitrary",)),
        name="token_mixers",
    )(*flat("inputs"))
    lse_w = HEADS * LSE_LANES
    outs = [res[2 * g].reshape(batch, RES, per, GROUP_WIDTH) for g in range(N_GROUPS)]
    lses = [res[2 * g + 1].reshape(batch, RES, per, lse_w) for g in range(N_GROUPS)]
    return outs, lses, res[2 * N_GROUPS]


def _merge_kernel(o1_ref, o2_ref, o3_ref, l1_ref, l2_ref, l3_ref, r_ref,
                  ga0_ref, ga1_ref, gr0_ref, gr1_ref, x_ref,
                  wpa_ref, wpr_ref, wo_ref, gpost_ref, gffn_ref,
                  h_ref, a2_ref, attn_s):
    tm = x_ref.shape[0]
    per = tm // RES
    flat = lambda ref: ref[...].reshape(tm, ref.shape[-1])

    la, lb, lc = flat(l1_ref), flat(l2_ref), flat(l3_ref)
    lm = jnp.maximum(la, jnp.maximum(lb, lc))
    ea, eb, ec = jnp.exp(la - lm), jnp.exp(lb - lm), jnp.exp(lc - lm)
    den = ea + eb + ec
    wa, wb, wc = ea / den, eb / den, ec / den
    o1, o2, o3 = flat(o1_ref), flat(o2_ref), flat(o3_ref)
    for h in range(HEADS):
        sl = slice(h * HEAD_DIM, (h + 1) * HEAD_DIM)
        c0 = h * LSE_LANES
        mixed = (wa[:, c0:c0 + 1] * o1[:, sl].astype(F32)
                 + wb[:, c0:c0 + 1] * o2[:, sl].astype(F32)
                 + wc[:, c0:c0 + 1] * o3[:, sl].astype(F32))
        attn_s[:, sl] = mixed.astype(BF16)

    tok = lax.broadcasted_iota(jnp.int32, (tm, tm), 0)
    src = _block_position(lax.broadcasted_iota(jnp.int32, (tm, tm), 1), per, tm)
    unpermute = jnp.where(src == tok, 1.0, 0.0).astype(BF16)

    attn = attn_s[...]
    r = flat(r_ref)
    half_w = GROUP_WIDTH
    halves = [slice(half * half_w, (half + 1) * half_w) for half in range(2)]
    branch = [(jnp.dot(attn, wpa_ref[:, cs], preferred_element_type=F32),
               jnp.dot(r, wpr_ref[:, cs], preferred_element_type=F32)) for cs in halves]
    z = None
    for cs, (ya, yr), (ga_ref, gr_ref) in zip(halves, branch, ((ga0_ref, gr0_ref), (ga1_ref, gr1_ref))):
        y = _sigmoid(flat(ga_ref).astype(F32)) * ya + _sigmoid(flat(gr_ref).astype(F32)) * yr
        y_nat = jnp.dot(unpermute, y.astype(BF16), preferred_element_type=F32).astype(BF16)
        part = jnp.dot(y_nat, wo_ref[cs, :], preferred_element_type=F32)
        z = part if z is None else z + part
    h1 = x_ref[...] + z * _rms_scale(z) * gpost_ref[...]
    h_ref[...] = h1
    a2_ref[...] = (h1 * _rms_scale(h1) * gffn_ref[...]).astype(BF16)


def _merge(outs, lses, r_out, proj, x2, wpa, wpr, wo, gpost, gffn, *, gate_col_block, tm=256):
    t, d = x2.shape
    batch, _, per, _ = proj.shape
    rows = tm // RES
    tiles_per_seq = per // rows
    const = lambda i: (0, 0)
    nat = pl.BlockSpec((tm, d), lambda i: (i, 0))

    def lay(width, col_block=0):
        return pl.BlockSpec((None, RES, rows, width),
                            lambda i: (i // tiles_per_seq, 0, i % tiles_per_seq, col_block))

    def resident(shape):
        return pl.BlockSpec(shape, const, pipeline_mode=pl.Buffered(1))

    lse_w = HEADS * LSE_LANES
    return pl.pallas_call(
        _merge_kernel,
        grid=(t // tm,),
        in_specs=[lay(GROUP_WIDTH)] * 3 + [lay(lse_w)] * 3 + [lay(GROUP_WIDTH)]
        + [lay(GROUP_WIDTH, gate_col_block + k) for k in range(4)]
        + [nat, resident(wpa.shape), resident(wpr.shape), resident(wo.shape),
           resident(gpost.shape), resident(gffn.shape)],
        out_specs=[nat, nat],
        out_shape=[jax.ShapeDtypeStruct((t, d), F32), jax.ShapeDtypeStruct((t, d), BF16)],
        scratch_shapes=[pltpu.VMEM((tm, GROUP_WIDTH), BF16)],
        compiler_params=_params(("parallel",)),
        name="merge_out_proj",
    )(*outs, *lses, r_out, proj, proj, proj, proj, x2, wpa, wpr, wo, gpost, gffn)


def _ffn_hidden_kernel(a_ref, wg_ref, wu_ref, side_ref, o_ref, cast_ref, wg_b, wu_b):
    @pl.when(pl.program_id(1) == 0)
    def _():
        wg_b[...] = wg_ref[...].astype(BF16)
        wu_b[...] = wu_ref[...].astype(BF16)

    cast_ref[...] = side_ref[...].astype(BF16)
    a = a_ref[...]
    gt = jnp.dot(a, wg_b[...], preferred_element_type=F32)
    up = jnp.dot(a, wu_b[...], preferred_element_type=F32)
    o_ref[...] = (_silu(gt) * up).astype(o_ref.dtype)


def _ffn_out_kernel(hid_ref, wd_ref, h_ref, gain_ref, o_ref):
    ff = jnp.dot(hid_ref[...], wd_ref[...], preferred_element_type=F32)
    o_ref[...] = h_ref[...] + ff * _rms_scale(ff) * gain_ref[...]


def _ffn(a2, w_gate_up, w_down, h1, gain, *, tm_hidden=1024, tf=512, tm_out=256):
    t, d = a2.shape
    d_ff = w_down.shape[0]
    nf = d_ff // tf
    n_row = t // tm_hidden
    side = _side_cast_specs([w_down], nf * n_row, lambda j, i: j * n_row + i)
    hidden, w_down = pl.pallas_call(
        _ffn_hidden_kernel,
        grid=(nf, n_row),
        in_specs=[
            pl.BlockSpec((tm_hidden, d), lambda j, i: (i, 0)),
            pl.BlockSpec((d, tf), lambda j, i: (0, j)),
            pl.BlockSpec((d, tf), lambda j, i: (0, nf + j)),
        ] + side,
        out_specs=[pl.BlockSpec((tm_hidden, tf), lambda j, i: (i, j))] + side,
        out_shape=[jax.ShapeDtypeStruct((t, d_ff), BF16), jax.ShapeDtypeStruct(w_down.shape, BF16)],
        scratch_shapes=[pltpu.VMEM((d, tf), BF16), pltpu.VMEM((d, tf), BF16)],
        compiler_params=_params(("arbitrary", "arbitrary")),
        name="swiglu_hidden",
    )(a2, w_gate_up, w_gate_up, w_down)
    rows = pl.BlockSpec((tm_out, d), lambda i: (i, 0))
    return pl.pallas_call(
        _ffn_out_kernel,
        grid=(t // tm_out,),
        in_specs=[
            pl.BlockSpec((tm_out, d_ff), lambda i: (i, 0)),
            pl.BlockSpec((d_ff, d), lambda i: (0, 0), pipeline_mode=pl.Buffered(1)),
            rows,
            pl.BlockSpec((1, d), lambda i: (0, 0)),
        ],
        out_specs=rows,
        out_shape=jax.ShapeDtypeStruct((t, d), F32),
        compiler_params=_params(("parallel",)),
        name="swiglu_out",
    )(hidden, w_down, h1, gain)


def _rotary_tables(seq):
    inv_freq = ROPE_THETA ** (-np.arange(0, HEAD_DIM, 2, dtype=np.float64) / HEAD_DIM)
    ang = np.arange(seq, dtype=np.float64)[:, None] * inv_freq[None, :]
    cos = np.concatenate([np.cos(ang), np.cos(ang)], axis=-1)
    sin_signed = np.concatenate([-np.sin(ang), np.sin(ang)], axis=-1)
    lay = lambda tab: jnp.asarray(tab.reshape(seq // RES, RES, HEAD_DIM).transpose(1, 0, 2), F32)
    return lay(cos), lay(sin_signed)


def kernel(x, w_in, w_attn_branch, w_hgrn_branch, w_mix_out, hgrn_lower_bounds, hgrn_norm_gain,
           norm_mix_pre, norm_mix_post, w_ffn_gate_up, w_ffn_down, norm_ffn_pre, norm_ffn_post):
    batch, seq, d = x.shape
    assert w_in.shape[0] == 1, "single-layer block"
    x2 = x.reshape(batch * seq, d)
    row = lambda v: v[0].reshape(1, -1).astype(F32)

    for window, dilation in ATTN_GROUPS:
        assert window // dilation == ATTN_BACK and seq % window == 0 and RES % dilation == 0
    assert w_in.shape[2] == (QKV_BLOCKS + 8) * GROUP_WIDTH
    cos_r, sin_r = _rotary_tables(seq)
    f32 = lambda wt: wt[0].astype(F32)
    bf16 = lambda wt: wt[0].astype(BF16)
    proj, keep = _in_proj(x2, row(norm_mix_pre), f32(w_in), cos_r, sin_r, hgrn_lower_bounds.astype(F32),
                          batch, seq)

    outs, lses, r_out = _mixers(proj, keep, row(hgrn_norm_gain), hgrn_col_block=QKV_BLOCKS)

    h1, a2 = _merge(outs, lses, r_out, proj, x2, bf16(w_attn_branch), bf16(w_hgrn_branch), bf16(w_mix_out),
                    row(norm_mix_post), row(norm_ffn_pre), gate_col_block=QKV_BLOCKS + 4)

    out = _ffn(a2, f32(w_ffn_gate_up), f32(w_ffn_down), h1, row(norm_ffn_post))
    return out.reshape(batch, seq, d)
```

```python
import math

import numpy as np
import jax
import jax.numpy as jnp
from jax import lax
from jax.experimental import pallas as pl
from jax.experimental.pallas import tpu as pltpu

F32 = jnp.float32
BF16 = jnp.bfloat16

HEAD_DIM = 128
ATTN_GROUPS = ((128, 1), (512, 4), (2048, 16))
N_GROUPS = len(ATTN_GROUPS)
HEADS = 8
GROUP_WIDTH = HEADS * HEAD_DIM
ROTARY_BLOCKS = 2 * N_GROUPS
QKV_BLOCKS = 3 * N_GROUPS
FORGET_BLOCK = QKV_BLOCKS + 1
ATTN_BACK = 128
ATTN_SUB_BLOCKS = 2
RES = 16
ROPE_THETA = 10000.0
NORM_EPS = 1e-6
HGRN_CHUNK = 128
HGRN_LEVELS = (64, 32, 16, 8, 4, 2, 1)
LSE_LANES = 16
BF16_SUBLANES = 16
NEG_BIG = -1e30
PERMUTE_GROUP = 256
LOG2E = math.log2(math.e)

VMEM_LIMIT_BYTES = 56 * 1024 * 1024


def _params(semantics):
    return pltpu.CompilerParams(dimension_semantics=semantics, vmem_limit_bytes=VMEM_LIMIT_BYTES)


def _rms_scale(v):
    return lax.rsqrt(jnp.mean(v * v, axis=-1, keepdims=True) + NORM_EPS)


def _sigmoid(v):
    return 0.5 * jnp.tanh(0.5 * v) + 0.5


def _silu(v):
    return v * _sigmoid(v)


def _side_cast_specs(weights, steps, step_index):
    specs = []
    for w in weights:
        rows, cols = w.shape
        block_rows = BF16_SUBLANES * pl.cdiv(rows, BF16_SUBLANES * steps)
        assert rows % block_rows == 0
        last = rows // block_rows - 1
        specs.append(pl.BlockSpec((block_rows, cols),
                                  lambda *idx, last=last: (jnp.minimum(step_index(*idx), last), 0)))
    return specs


def _block_position(i, rows_per_residue, block_rows):
    shift = rows_per_residue.bit_length() - 1
    n_res_shift = (block_rows // rows_per_residue).bit_length() - 1
    return ((i & (rows_per_residue - 1)) << n_res_shift) | lax.shift_right_logical(i, shift)


def _in_proj_kernel(x_lo_ref, x_hi_ref, gain_ref, w_ref, cos_ref, sin_ref, hlb_ref, o_ref, keep_ref, a_ref):
    half_rows = x_lo_ref.shape[0]
    tm = 2 * half_rows
    per_res = tm // RES
    grp = PERMUTE_GROUP
    per_grp = grp // RES
    j = pl.program_id(1)

    @pl.when(j == 0)
    def _():
        row_token = _block_position(lax.broadcasted_iota(jnp.int32, (grp, grp), 0), per_grp, grp)
        permute = jnp.where(row_token == lax.broadcasted_iota(jnp.int32, (grp, grp), 1), 1.0, 0.0).astype(BF16)
        gain = gain_ref[...]
        for gi in range(tm // grp):
            x_ref, first = (x_lo_ref, gi * grp) if gi * grp < half_rows else (x_hi_ref, gi * grp - half_rows)
            xg = x_ref[first:first + grp, :]
            ag = (xg * _rms_scale(xg) * gain).astype(BF16)
            pg = jnp.dot(permute, ag, preferred_element_type=F32).astype(BF16)
            for r in range(RES):
                dst = r * per_res + gi * per_grp
                a_ref[dst:dst + per_grp, :] = pg[r * per_grp:(r + 1) * per_grp, :]

    def project():
        return jnp.dot(a_ref[...], w_ref[...].astype(BF16), preferred_element_type=F32)

    def store(ref, val):
        ref[...] = val.reshape(RES, per_res, val.shape[-1]).astype(ref.dtype)

    @pl.when(j < ROTARY_BLOCKS)
    def _():
        res = project()
        scale = jnp.where(j < N_GROUPS, HEAD_DIM ** -0.5 * LOG2E, 1.0)
        cos = cos_ref[...].reshape(tm, HEAD_DIM) * scale
        sin = sin_ref[...].reshape(tm, HEAD_DIM) * scale
        heads = []
        for h in range(HEADS):
            t = res[:, h * HEAD_DIM:(h + 1) * HEAD_DIM]
            heads.append(t * cos + pltpu.roll(t, HEAD_DIM // 2, 1) * sin)
        store(o_ref, jnp.concatenate(heads, axis=1))

    is_swish = (j == FORGET_BLOCK - 1) | (j == FORGET_BLOCK + 2)

    @pl.when(is_swish)
    def _():
        store(o_ref, _silu(project()))

    @pl.when((j >= ROTARY_BLOCKS) & (j != FORGET_BLOCK) & jnp.logical_not(is_swish))
    def _():
        store(o_ref, project())

    @pl.when(j == FORGET_BLOCK)
    def _():
        res = project()
        h0 = hlb_ref[0:1, :]
        h1 = hlb_ref[1:2, :]
        hm = jnp.maximum(h0, h1)
        e0 = jnp.exp(h0 - hm)
        e1 = jnp.exp(h1 - hm)
        lb = e0 / (e0 + e1)
        f = lb + (1.0 - lb) * jax.nn.sigmoid(res)
        store(o_ref, jnp.log(f))
        store(keep_ref, 1.0 - f)


def _in_proj(x2, gain, w, cos_r, sin_r, lower_bounds, batch, seq, *, tm=1024, tn=GROUP_WIDTH):
    assert lower_bounds.shape[0] == 2, "forget-gate bound rows: DEPTH + 1 with DEPTH == 1"
    t, d = x2.shape
    n = w.shape[1]
    tiles_per_seq = seq // tm
    table = pl.BlockSpec((RES, tm // RES, HEAD_DIM), lambda i, j: (0, i % tiles_per_seq, 0))
    lay = lambda col: pl.BlockSpec((None, RES, tm // RES, tn),
                                   lambda i, j: (i // tiles_per_seq, 0, i % tiles_per_seq, col(j)))
    n_row, n_col = t // tm, n // tn

    def x_half(which, lead):
        def index(i, j):
            tile = jnp.minimum(i + (j >= n_col - lead).astype(jnp.int32), n_row - 1)
            return (2 * tile + which, 0)
        return pl.BlockSpec((tm // 2, d), index)

    return pl.pallas_call(
        _in_proj_kernel,
        grid=(n_row, n_col),
        in_specs=[
            x_half(0, 2), x_half(1, 1),
            pl.BlockSpec((1, d), lambda i, j: (0, 0)),
            pl.BlockSpec((d, tn), lambda i, j: (0, j)),
            table, table,
            pl.BlockSpec(lower_bounds.shape, lambda i, j: (0, 0)),
        ],
        out_specs=[lay(lambda j: j), lay(lambda j: 0)],
        out_shape=[jax.ShapeDtypeStruct((batch, RES, seq // RES, n), BF16),
                   jax.ShapeDtypeStruct((batch, RES, seq // RES, tn), BF16)],
        scratch_shapes=[pltpu.VMEM((tm, d), BF16)],
        compiler_params=_params(("arbitrary", "arbitrary")),
        name="in_proj",
    )(x2, x2, gain, w, cos_r, sin_r, lower_bounds)


def _sub_blocks(ref, sl, nsub):
    v = ref[:, :, sl]
    a, u, w = v.shape
    step = u // nsub
    if step % BF16_SUBLANES:
        v = v.astype(F32)
    return [v[:, s * step:(s + 1) * step, :].reshape(a * step, w).astype(ref.dtype) for s in range(nsub)]


def _store_sub_blocks(ref, sl, blocks):
    a = ref.shape[0]
    parts = [b.reshape(a, b.shape[0] // a, b.shape[1]) for b in blocks]
    ref[:, :, sl] = jnp.concatenate(parts, axis=1).astype(ref.dtype)


def _attn_carry(n, kk_ref, vv_ref):
    blk = ATTN_BACK
    nsub = ATTN_SUB_BLOCKS

    @pl.when(n == 0)
    def _():
        kk_ref[0:blk, :] = jnp.zeros((blk, GROUP_WIDTH), BF16)
        vv_ref[0:blk, :] = jnp.zeros((blk, GROUP_WIDTH), BF16)

    @pl.when(n > 0)
    def _():
        kk_ref[0:blk, :] = kk_ref[nsub * blk:(nsub + 1) * blk, :]
        vv_ref[0:blk, :] = vv_ref[nsub * blk:(nsub + 1) * blk, :]


def _attn_body(n, q_ref, k_ref, v_ref, o_ref, lse_ref, kk_ref, vv_ref, *, rows_per_residue):
    blk = ATTN_BACK
    nsub = ATTN_SUB_BLOCKS

    qi = lax.broadcasted_iota(jnp.int32, (blk, 2 * blk), 0)
    kj = lax.broadcasted_iota(jnp.int32, (blk, 2 * blk), 1)
    q_pos = _block_position(qi, rows_per_residue, blk)
    k_pos = _block_position(kj & (blk - 1), rows_per_residue, blk) - jnp.where(kj < blk, blk, 0)
    dist = q_pos - k_pos
    band = (dist >= 0) & (dist <= ATTN_BACK)
    first_mask = band & (kj >= jnp.where(n > 0, 0, blk))
    lane_head = lax.shift_right_logical(lax.broadcasted_iota(jnp.int32, (blk, HEADS * LSE_LANES), 1),
                                        LSE_LANES.bit_length() - 1)
    lse_tiles = [jnp.zeros((blk, HEADS * LSE_LANES), F32) for _ in range(nsub)]

    for h in range(HEADS):
        sl = slice(h * HEAD_DIM, (h + 1) * HEAD_DIM)
        for s, (kb, vb) in enumerate(zip(_sub_blocks(k_ref, sl, nsub), _sub_blocks(v_ref, sl, nsub))):
            rows = slice((s + 1) * blk, (s + 2) * blk)
            kk_ref[rows, sl] = kb
            vv_ref[rows, sl] = vb
        outs = []
        for s, qb in enumerate(_sub_blocks(q_ref, sl, nsub)):
            keys = slice(s * blk, (s + 2) * blk)
            sc = lax.dot_general(qb, kk_ref[keys, sl], (((1,), (1,)), ((), ())), preferred_element_type=F32)
            sc = jnp.where(first_mask if s == 0 else band, sc, NEG_BIG)
            m = jnp.max(sc, axis=-1, keepdims=True)
            p = jnp.exp2(sc - m)
            den = jnp.sum(p, axis=-1, keepdims=True)
            o = jnp.dot(p.astype(BF16), vv_ref[keys, sl], preferred_element_type=F32)
            outs.append(o / den)
            lse = m * math.log(2.0) + jnp.log(den)
            lse_tiles[s] = jnp.where(lane_head == h, lse, lse_tiles[s])
        _store_sub_blocks(o_ref, sl, outs)
    _store_sub_blocks(lse_ref, slice(None), lse_tiles)


def _attn_operands(proj, g, dilation):
    batch, _, per, _ = proj.shape
    a = RES // dilation
    rows = ATTN_BACK // a
    rows_blk = rows * ATTN_SUB_BLOCKS
    nblk = per // rows_blk
    view = lambda arr: arr.reshape(arr.shape[:-3] + (a, dilation) + arr.shape[-2:])

    def spec(width, col_block):
        return pl.BlockSpec((None, a, None, rows_blk, width),
                            lambda i: (i // (dilation * nblk), 0, (i // nblk) % dilation, i % nblk, col_block))

    lse_w = HEADS * LSE_LANES
    proj_v = view(proj)
    key_rows = (ATTN_SUB_BLOCKS + 1) * ATTN_BACK
    return dict(
        inputs=[proj_v, proj_v, proj_v],
        in_specs=[spec(GROUP_WIDTH, which * N_GROUPS + g) for which in range(3)],
        out_specs=[spec(GROUP_WIDTH, 0), spec(lse_w, 0)],
        out_shape=[jax.ShapeDtypeStruct((batch, a, dilation, per, GROUP_WIDTH), BF16),
                   jax.ShapeDtypeStruct((batch, a, dilation, per, lse_w), F32)],
        scratch=[pltpu.VMEM((key_rows, GROUP_WIDTH), BF16), pltpu.VMEM((key_rows, GROUP_WIDTH), BF16)],
        rows_per_residue=rows, nblk=nblk)


def _hgrn_cumsum_matrix():
    c = HGRN_CHUNK
    i = np.arange(c)
    per = c // RES
    pos = (i % per) * RES + i // per
    return (pos[None, :] <= pos[:, None]).astype(np.float32)


def _level_reference(b3, half):
    res, per, dk = b3.shape
    if half < res:
        two = 2 * half
        parts = [jnp.broadcast_to(b3[blk * two + half - 1][None], (two, per, dk)) for blk in range(res // two)]
        return parts[0] if len(parts) == 1 else jnp.concatenate(parts, axis=0)
    last = b3[res - 1]
    step = half // res
    u = lax.broadcasted_iota(jnp.int32, (per, dk), 0)
    ref = None
    for blk in reversed(range(per // (2 * step))):
        row = jnp.broadcast_to(last[blk * 2 * step + step - 1][None], (per, dk))
        ref = row if ref is None else jnp.where(u < (blk + 1) * 2 * step, row, ref)
    return jnp.broadcast_to(ref[None], (res, per, dk))


def _scale_valid_rows(t3, e, half, lower):
    res = t3.shape[0]
    if half >= res:
        return t3 * e
    parts = []
    for lo in range(0, res, 2 * half):
        up, dn = slice(lo, lo + half), slice(lo + half, lo + 2 * half)
        parts += [t3[up], t3[dn] * e[dn]] if lower else [t3[up] * e[up], t3[dn]]
    return jnp.concatenate(parts, axis=0)


def _hgrn_body(q_ref, logf_ref, v_ref, gate_ref, keep_ref, gain_ref, tri_ref, o_ref, st_ref):
    c = HGRN_CHUNK
    dk = HEAD_DIM
    per = c // RES

    g_all = logf_ref[...].astype(F32)
    k_all = keep_ref[...].astype(F32)
    q_all = q_ref[...].astype(F32)
    v_all = v_ref[...].astype(F32)
    gate_all = gate_ref[...].astype(F32)

    ti = _block_position(lax.broadcasted_iota(jnp.int32, (c, c), 0), per, c)
    si = _block_position(lax.broadcasted_iota(jnp.int32, (c, c), 1), per, c)
    diag_mask = ti == si
    level_masks = []
    for half in HGRN_LEVELS:
        shift = half.bit_length()
        level_masks.append((lax.shift_right_logical(ti, shift) == lax.shift_right_logical(si, shift))
                           & ((ti & half) != 0) & ((si & half) == 0))
    tri = tri_ref[...]

    n_chunks = o_ref.shape[1] // per
    outs = [[None] * n_chunks for _ in range(HEADS)]
    for ci in range(n_chunks):
        rows = slice(ci * per, (ci + 1) * per)
        g = g_all[:, rows, :].reshape(c, GROUP_WIDTH)
        g_hi = g.astype(BF16)
        r1 = g - g_hi.astype(F32)
        g_mid = r1.astype(BF16)
        g_lo = (r1 - g_mid.astype(F32)).astype(BF16)
        cum = (jnp.dot(tri, g_hi, preferred_element_type=F32)
               + jnp.dot(tri, g_mid, preferred_element_type=F32)
               + jnp.dot(tri, g_lo, preferred_element_type=F32))

        for h in range(HEADS):
            sl = slice(h * dk, (h + 1) * dk)
            q3 = q_all[:, rows, sl]
            k3 = k_all[:, rows, sl]
            q = q3.reshape(c, dk)
            k = k3.reshape(c, dk)
            v = v_all[:, rows, sl].reshape(c, dk).astype(BF16)
            b = cum[:, sl]
            b3 = b.reshape(RES, per, dk)
            b_last = b[c - 1:c, :]
            st = st_ref[h]

            q_dec = (q * jnp.exp(b)).astype(BF16)
            o = lax.dot_general(q_dec, st.astype(BF16), (((1,), (1,)), ((), ())),
                                preferred_element_type=F32)

            a = jnp.where(diag_mask, jnp.sum(q * k, axis=-1, keepdims=True), 0.0)
            for li, half in enumerate(HGRN_LEVELS):
                e = jnp.exp2(jnp.abs(b3 - _level_reference(b3, half)) * -LOG2E)
                q_l = _scale_valid_rows(q3, e, half, lower=True).reshape(c, dk).astype(BF16)
                k_l = _scale_valid_rows(k3, e, half, lower=False).reshape(c, dk).astype(BF16)
                p_l = lax.dot_general(q_l, k_l, (((1,), (1,)), ((), ())), preferred_element_type=F32)
                a = jnp.where(level_masks[li], p_l, a)
            o = o + jnp.dot(a.astype(BF16), v, preferred_element_type=F32)

            k_dec = (k * jnp.exp(b_last - b)).astype(BF16)
            st_ref[h] = st * jnp.exp(b_last) + lax.dot_general(
                v, k_dec, (((0,), (0,)), ((), ())), preferred_element_type=F32)

            r = o * _rms_scale(o) * gain_ref[:, sl] * gate_all[:, rows, sl].reshape(c, dk)
            outs[h][ci] = r.reshape(RES, per, dk)

    for h in range(HEADS):
        o_ref[:, :, h * dk:(h + 1) * dk] = jnp.concatenate(outs[h], axis=1).astype(o_ref.dtype)


def _hgrn_operands(proj, keep, gain, *, first_col_block, rows):
    batch, _, per, _ = proj.shape
    tiles = per // rows
    tri = jnp.asarray(_hgrn_cumsum_matrix(), BF16)

    def col_spec(col_block):
        return pl.BlockSpec((None, RES, rows, GROUP_WIDTH), lambda i: (i // tiles, 0, i % tiles, col_block))

    const = lambda i: (0, 0)
    return dict(
        inputs=[proj, proj, proj, proj, keep, gain, tri],
        in_specs=[col_spec(first_col_block + k) for k in range(4)]
        + [col_spec(0), pl.BlockSpec(gain.shape, const), pl.BlockSpec(tri.shape, const)],
        out_specs=[col_spec(0)],
        out_shape=[jax.ShapeDtypeStruct((batch, RES, per, GROUP_WIDTH), BF16)],
        scratch=[pltpu.VMEM((HEADS, HEAD_DIM, HEAD_DIM), F32)],
        tiles=tiles)


def _mixers(proj, keep, gain, *, hgrn_col_block):
    batch, _, per, _ = proj.shape
    rows = ATTN_SUB_BLOCKS * ATTN_BACK // RES
    steps = batch * per // rows
    parts = [_attn_operands(proj, g, dilation) for g, (_, dilation) in enumerate(ATTN_GROUPS)]
    parts.append(_hgrn_operands(proj, keep, gain, first_col_block=hgrn_col_block, rows=rows))
    n_in = [len(p["inputs"]) for p in parts]
    n_out = [len(p["out_shape"]) for p in parts]
    n_scr = [len(p["scratch"]) for p in parts]

    def kernel_body(*refs):
        ins, rest = refs[:sum(n_in)], refs[sum(n_in):]
        outs, scr = rest[:sum(n_out)], rest[sum(n_out):]
        i = pl.program_id(0)
        take = lambda seq, counts, k: seq[sum(counts[:k]):sum(counts[:k + 1])]
        for k, part in enumerate(parts[:-1]):
            _attn_carry(i % part["nblk"], *take(scr, n_scr, k))
        k = len(parts) - 1

        @pl.when(i % parts[k]["tiles"] == 0)
        def _():
            st_ref = take(scr, n_scr, k)[0]
            st_ref[...] = jnp.zeros(st_ref.shape, F32)

        for k, part in enumerate(parts[:-1]):
            _attn_body(i % part["nblk"], *take(ins, n_in, k), *take(outs, n_out, k), *take(scr, n_scr, k),
                       rows_per_residue=part["rows_per_residue"])
        k = len(parts) - 1
        _hgrn_body(*take(ins, n_in, k), *take(outs, n_out, k), *take(scr, n_scr, k))

    flat = lambda key: [v for p in parts for v in p[key]]
    res = pl.pallas_call(
        kernel_body,
        grid=(steps,),
        in_specs=flat("in_specs"),
        out_specs=flat("out_specs"),
        out_shape=flat("out_shape"),
        scratch_shapes=flat("scratch"),
        compiler_params=_params(("arbitrary",)),
        name="token_mixers",
    )(*flat("inputs"))
    lse_w = HEADS * LSE_LANES
    outs = [res[2 * g].reshape(batch, RES, per, GROUP_WIDTH) for g in range(N_GROUPS)]
    lses = [res[2 * g + 1].reshape(batch, RES, per, lse_w) for g in range(N_GROUPS)]
    return outs, lses, res[2 * N_GROUPS]


def _merge_kernel(o1_ref, o2_ref, o3_ref, l1_ref, l2_ref, l3_ref, r_ref,
                  ga0_ref, ga1_ref, gr0_ref, gr1_ref, x_ref,
                  wpa_ref, wpr_ref, wo_ref, gpost_ref, gffn_ref,
                  h_ref, a2_ref, attn_s):
    tm = x_ref.shape[0]
    per = tm // RES
    flat = lambda ref: ref[...].reshape(tm, ref.shape[-1])

    la, lb, lc = flat(l1_ref), flat(l2_ref), flat(l3_ref)
    lm = jnp.maximum(la, jnp.maximum(lb, lc))
    ea, eb, ec = jnp.exp(la - lm), jnp.exp(lb - lm), jnp.exp(lc - lm)
    den = ea + eb + ec
    wa, wb, wc = ea / den, eb / den, ec / den
    o1, o2, o3 = flat(o1_ref), flat(o2_ref), flat(o3_ref)
    for h in range(HEADS):
        sl = slice(h * HEAD_DIM, (h + 1) * HEAD_DIM)
        c0 = h * LSE_LANES
        mixed = (wa[:, c0:c0 + 1] * o1[:, sl].astype(F32)
                 + wb[:, c0:c0 + 1] * o2[:, sl].astype(F32)
                 + wc[:, c0:c0 + 1] * o3[:, sl].astype(F32))
        attn_s[:, sl] = mixed.astype(BF16)

    tok = lax.broadcasted_iota(jnp.int32, (tm, tm), 0)
    src = _block_position(lax.broadcasted_iota(jnp.int32, (tm, tm), 1), per, tm)
    unpermute = jnp.where(src == tok, 1.0, 0.0).astype(BF16)

    attn = attn_s[...]
    r = flat(r_ref)
    half_w = GROUP_WIDTH
    halves = [slice(half * half_w, (half + 1) * half_w) for half in range(2)]
    branch = [(jnp.dot(attn, wpa_ref[:, cs], preferred_element_type=F32),
               jnp.dot(r, wpr_ref[:, cs], preferred_element_type=F32)) for cs in halves]
    z = None
    for cs, (ya, yr), (ga_ref, gr_ref) in zip(halves, branch, ((ga0_ref, gr0_ref), (ga1_ref, gr1_ref))):
        y = _sigmoid(flat(ga_ref).astype(F32)) * ya + _sigmoid(flat(gr_ref).astype(F32)) * yr
        y_nat = jnp.dot(unpermute, y.astype(BF16), preferred_element_type=F32).astype(BF16)
        part = jnp.dot(y_nat, wo_ref[cs, :], preferred_element_type=F32)
        z = part if z is None else z + part
    h1 = x_ref[...] + z * _rms_scale(z) * gpost_ref[...]
    h_ref[...] = h1
    a2_ref[...] = (h1 * _rms_scale(h1) * gffn_ref[...]).astype(BF16)


def _merge(outs, lses, r_out, proj, x2, wpa, wpr, wo, gpost, gffn, *, gate_col_block, tm=256):
    t, d = x2.shape
    batch, _, per, _ = proj.shape
    rows = tm // RES
    tiles_per_seq = per // rows
    const = lambda i: (0, 0)
    nat = pl.BlockSpec((tm, d), lambda i: (i, 0))

    def lay(width, col_block=0):
        return pl.BlockSpec((None, RES, rows, width),
                            lambda i: (i // tiles_per_seq, 0, i % tiles_per_seq, col_block))

    def resident(shape):
        return pl.BlockSpec(shape, const, pipeline_mode=pl.Buffered(1))

    lse_w = HEADS * LSE_LANES
    return pl.pallas_call(
        _merge_kernel,
        grid=(t // tm,),
        in_specs=[lay(GROUP_WIDTH)] * 3 + [lay(lse_w)] * 3 + [lay(GROUP_WIDTH)]
        + [lay(GROUP_WIDTH, gate_col_block + k) for k in range(4)]
        + [nat, resident(wpa.shape), resident(wpr.shape), resident(wo.shape),
           resident(gpost.shape), resident(gffn.shape)],
        out_specs=[nat, nat],
        out_shape=[jax.ShapeDtypeStruct((t, d), F32), jax.ShapeDtypeStruct((t, d), BF16)],
        scratch_shapes=[pltpu.VMEM((tm, GROUP_WIDTH), BF16)],
        compiler_params=_params(("parallel",)),
        name="merge_out_proj",
    )(*outs, *lses, r_out, proj, proj, proj, proj, x2, wpa, wpr, wo, gpost, gffn)


def _ffn_hidden_kernel(a_ref, wg_ref, wu_ref, side_ref, o_ref, cast_ref, wg_b, wu_b):
    @pl.when(pl.program_id(1) == 0)
    def _():
        wg_b[...] = wg_ref[...].astype(BF16)
        wu_b[...] = wu_ref[...].astype(BF16)

    cast_ref[...] = side_ref[...].astype(BF16)
    a = a_ref[...]
    gt = jnp.dot(a, wg_b[...], preferred_element_type=F32)
    up = jnp.dot(a, wu_b[...], preferred_element_type=F32)
    o_ref[...] = (_silu(gt) * up).astype(o_ref.dtype)


def _ffn_out_kernel(hid_ref, wd_ref, h_ref, gain_ref, o_ref):
    ff = jnp.dot(hid_ref[...], wd_ref[...], preferred_element_type=F32)
    o_ref[...] = h_ref[...] + ff * _rms_scale(ff) * gain_ref[...]


def _ffn(a2, w_gate_up, w_down, h1, gain, *, tm_hidden=1024, tf=512, tm_out=256):
    t, d = a2.shape
    d_ff = w_down.shape[0]
    nf = d_ff // tf
    n_row = t // tm_hidden
    side = _side_cast_specs([w_down], nf * n_row, lambda j, i: j * n_row + i)
    hidden, w_down = pl.pallas_call(
        _ffn_hidden_kernel,
        grid=(nf, n_row),
        in_specs=[
            pl.BlockSpec((tm_hidden, d), lambda j, i: (i, 0)),
            pl.BlockSpec((d, tf), lambda j, i: (0, j)),
            pl.BlockSpec((d, tf), lambda j, i: (0, nf + j)),
        ] + side,
        out_specs=[pl.BlockSpec((tm_hidden, tf), lambda j, i: (i, j))] + side,
        out_shape=[jax.ShapeDtypeStruct((t, d_ff), BF16), jax.ShapeDtypeStruct(w_down.shape, BF16)],
        scratch_shapes=[pltpu.VMEM((d, tf), BF16), pltpu.VMEM((d, tf), BF16)],
        compiler_params=_params(("arbitrary", "arbitrary")),
        name="swiglu_hidden",
    )(a2, w_gate_up, w_gate_up, w_down)
    rows = pl.BlockSpec((tm_out, d), lambda i: (i, 0))
    return pl.pallas_call(
        _ffn_out_kernel,
        grid=(t // tm_out,),
        in_specs=[
            pl.BlockSpec((tm_out, d_ff), lambda i: (i, 0)),
            pl.BlockSpec((d_ff, d), lambda i: (0, 0), pipeline_mode=pl.Buffered(1)),
            rows,
            pl.BlockSpec((1, d), lambda i: (0, 0)),
        ],
        out_specs=rows,
        out_shape=jax.ShapeDtypeStruct((t, d), F32),
        compiler_params=_params(("parallel",)),
        name="swiglu_out",
    )(hidden, w_down, h1, gain)


def _rotary_tables(seq):
    inv_freq = ROPE_THETA ** (-np.arange(0, HEAD_DIM, 2, dtype=np.float64) / HEAD_DIM)
    ang = np.arange(seq, dtype=np.float64)[:, None] * inv_freq[None, :]
    cos = np.concatenate([np.cos(ang), np.cos(ang)], axis=-1)
    sin_signed = np.concatenate([-np.sin(ang), np.sin(ang)], axis=-1)
    lay = lambda tab: jnp.asarray(tab.reshape(seq // RES, RES, HEAD_DIM).transpose(1, 0, 2), F32)
    return lay(cos), lay(sin_signed)


def kernel(x, w_in, w_attn_branch, w_hgrn_branch, w_mix_out, hgrn_lower_bounds, hgrn_norm_gain,
           norm_mix_pre, norm_mix_post, w_ffn_gate_up, w_ffn_down, norm_ffn_pre, norm_ffn_post):
    batch, seq, d = x.shape
    assert w_in.shape[0] == 1, "single-layer block"
    x2 = x.reshape(batch * seq, d)
    row = lambda v: v[0].reshape(1, -1).astype(F32)

    for window, dilation in ATTN_GROUPS:
        assert window // dilation == ATTN_BACK and seq % window == 0 and RES % dilation == 0
    assert w_in.shape[2] == (QKV_BLOCKS + 8) * GROUP_WIDTH
    cos_r, sin_r = _rotary_tables(seq)
    f32 = lambda wt: wt[0].astype(F32)
    bf16 = lambda wt: wt[0].astype(BF16)
    proj, keep = _in_proj(x2, row(norm_mix_pre), f32(w_in), cos_r, sin_r, hgrn_lower_bounds.astype(F32),
                          batch, seq)

    outs, lses, r_out = _mixers(proj, keep, row(hgrn_norm_gain), hgrn_col_block=QKV_BLOCKS)

    h1, a2 = _merge(outs, lses, r_out, proj, x2, bf16(w_attn_branch), bf16(w_hgrn_branch), bf16(w_mix_out),
                    row(norm_mix_post), row(norm_ffn_pre), gate_col_block=QKV_BLOCKS + 4)

    out = _ffn(a2, f32(w_ffn_gate_up), f32(w_ffn_down), h1, row(norm_ffn_post))
    return out.reshape(batch, seq, d)
```

```python
import math

import numpy as np
import jax
import jax.numpy as jnp
from jax import lax
from jax.experimental import pallas as pl
from jax.experimental.pallas import tpu as pltpu

F32 = jnp.float32
BF16 = jnp.bfloat16

HEAD_DIM = 128
ATTN_GROUPS = ((128, 1), (512, 4), (2048, 16))
N_GROUPS = len(ATTN_GROUPS)
HEADS = 8
GROUP_WIDTH = HEADS * HEAD_DIM
ROTARY_BLOCKS = 2 * N_GROUPS
QKV_BLOCKS = 3 * N_GROUPS
FORGET_BLOCK = QKV_BLOCKS + 1
ATTN_BACK = 128
ATTN_SUB_BLOCKS = 2
RES = 16
ROPE_THETA = 10000.0
NORM_EPS = 1e-6
HGRN_CHUNK = 128
HGRN_LEVELS = (64, 32, 16, 8, 4, 2, 1)
LSE_LANES = 16
BF16_SUBLANES = 16
NEG_BIG = -1e30
PERMUTE_GROUP = 256
LOG2E = math.log2(math.e)

VMEM_LIMIT_BYTES = 56 * 1024 * 1024


def _params(semantics):
    return pltpu.CompilerParams(dimension_semantics=semantics, vmem_limit_bytes=VMEM_LIMIT_BYTES)


def _rms_scale(v):
    return lax.rsqrt(jnp.mean(v * v, axis=-1, keepdims=True) + NORM_EPS)


def _sigmoid(v):
    return 0.5 * jnp.tanh(0.5 * v) + 0.5


def _silu(v):
    return v * _sigmoid(v)


def _side_cast_specs(weights, steps, step_index):
    specs = []
    for w in weights:
        rows, cols = w.shape
        block_rows = BF16_SUBLANES * pl.cdiv(rows, BF16_SUBLANES * steps)
        assert rows % block_rows == 0
        last = rows // block_rows - 1
        specs.append(pl.BlockSpec((block_rows, cols),
                                  lambda *idx, last=last: (jnp.minimum(step_index(*idx), last), 0)))
    return specs


def _block_position(i, rows_per_residue, block_rows):
    shift = rows_per_residue.bit_length() - 1
    n_res_shift = (block_rows // rows_per_residue).bit_length() - 1
    return ((i & (rows_per_residue - 1)) << n_res_shift) | lax.shift_right_logical(i, shift)


def _in_proj_kernel(x_lo_ref, x_hi_ref, gain_ref, w_ref, rot_ref, hlb_ref, o_ref, keep_ref, a_ref):
    half_rows = x_lo_ref.shape[0]
    tm = 2 * half_rows
    per_res = tm // RES
    grp = PERMUTE_GROUP
    per_grp = grp // RES
    j = pl.program_id(1)

    @pl.when(j == 0)
    def _():
        row_token = _block_position(lax.broadcasted_iota(jnp.int32, (grp, grp), 0), per_grp, grp)
        permute = jnp.where(row_token == lax.broadcasted_iota(jnp.int32, (grp, grp), 1), 1.0, 0.0).astype(BF16)
        gain = gain_ref[...]
        for gi in range(tm // grp):
            x_ref, first = (x_lo_ref, gi * grp) if gi * grp < half_rows else (x_hi_ref, gi * grp - half_rows)
            xg = x_ref[first:first + grp, :]
            ag = (xg * _rms_scale(xg) * gain).astype(BF16)
            pg = jnp.dot(permute, ag, preferred_element_type=F32).astype(BF16)
            for r in range(RES):
                dst = r * per_res + gi * per_grp
                a_ref[dst:dst + per_grp, :] = pg[r * per_grp:(r + 1) * per_grp, :]

    def project():
        return jnp.dot(a_ref[...], w_ref[...].astype(BF16), preferred_element_type=F32)

    def store(ref, val):
        ref[...] = val.reshape(RES, per_res, val.shape[-1]).astype(ref.dtype)

    @pl.when(j < ROTARY_BLOCKS)
    def _():
        res = project()
        scale = jnp.where(j < N_GROUPS, HEAD_DIM ** -0.5 * LOG2E, 1.0)
        table = rot_ref[...].reshape(tm, 2 * HEAD_DIM) * scale
        cos = table[:, :HEAD_DIM]
        sin = table[:, HEAD_DIM:]
        heads = []
        for h in range(HEADS):
            t = res[:, h * HEAD_DIM:(h + 1) * HEAD_DIM]
            heads.append(t * cos + pltpu.roll(t, HEAD_DIM // 2, 1) * sin)
        store(o_ref, jnp.concatenate(heads, axis=1))

    is_swish = (j == FORGET_BLOCK - 1) | (j == FORGET_BLOCK + 2)

    @pl.when(is_swish)
    def _():
        store(o_ref, _silu(project()))

    @pl.when((j >= ROTARY_BLOCKS) & (j != FORGET_BLOCK) & jnp.logical_not(is_swish))
    def _():
        store(o_ref, project())

    @pl.when(j == FORGET_BLOCK)
    def _():
        res = project()
        h0 = hlb_ref[0:1, :]
        h1 = hlb_ref[1:2, :]
        hm = jnp.maximum(h0, h1)
        e0 = jnp.exp(h0 - hm)
        e1 = jnp.exp(h1 - hm)
        lb = e0 / (e0 + e1)
        f = lb + (1.0 - lb) * jax.nn.sigmoid(res)
        store(o_ref, jnp.log(f))
        store(keep_ref, 1.0 - f)


def _in_proj(x2, gain, w, rotary, lower_bounds, batch, seq, *, tm=1024, tn=GROUP_WIDTH):
    assert lower_bounds.shape[0] == 2, "forget-gate bound rows: DEPTH + 1 with DEPTH == 1"
    t, d = x2.shape
    n = w.shape[1]
    tiles_per_seq = seq // tm
    table = pl.BlockSpec((RES, tm // RES, 2 * HEAD_DIM), lambda i, j: (0, i % tiles_per_seq, 0))
    lay = lambda col: pl.BlockSpec((None, RES, tm // RES, tn),
                                   lambda i, j: (i // tiles_per_seq, 0, i % tiles_per_seq, col(j)))
    n_row, n_col = t // tm, n // tn

    def x_half(which, lead):
        def index(i, j):
            tile = jnp.minimum(i + (j >= n_col - lead).astype(jnp.int32), n_row - 1)
            return (2 * tile + which, 0)
        return pl.BlockSpec((tm // 2, d), index)

    return pl.pallas_call(
        _in_proj_kernel,
        grid=(n_row, n_col),
        in_specs=[
            x_half(0, 2), x_half(1, 1),
            pl.BlockSpec((1, d), lambda i, j: (0, 0)),
            pl.BlockSpec((d, tn), lambda i, j: (0, j)),
            table,
            pl.BlockSpec(lower_bounds.shape, lambda i, j: (0, 0)),
        ],
        out_specs=[lay(lambda j: j), lay(lambda j: 0)],
        out_shape=[jax.ShapeDtypeStruct((batch, RES, seq // RES, n), BF16),
                   jax.ShapeDtypeStruct((batch, RES, seq // RES, tn), BF16)],
        scratch_shapes=[pltpu.VMEM((tm, d), BF16)],
        compiler_params=_params(("arbitrary", "arbitrary")),
        name="in_proj",
    )(x2, x2, gain, w, rotary, lower_bounds)


def _sub_blocks(ref, sl, nsub):
    v = ref[:, :, sl]
    a, u, w = v.shape
    step = u // nsub
    if step % BF16_SUBLANES:
        v = v.astype(F32)
    return [v[:, s * step:(s + 1) * step, :].reshape(a * step, w).astype(ref.dtype) for s in range(nsub)]


def _store_sub_blocks(ref, sl, blocks):
    a = ref.shape[0]
    parts = [b.reshape(a, b.shape[0] // a, b.shape[1]) for b in blocks]
    ref[:, :, sl] = jnp.concatenate(parts, axis=1).astype(ref.dtype)


def _attn_carry(n, kk_ref, vv_ref):
    blk = ATTN_BACK
    nsub = ATTN_SUB_BLOCKS

    @pl.when(n == 0)
    def _():
        kk_ref[0:blk, :] = jnp.zeros((blk, GROUP_WIDTH), BF16)
        vv_ref[0:blk, :] = jnp.zeros((blk, GROUP_WIDTH), BF16)

    @pl.when(n > 0)
    def _():
        kk_ref[0:blk, :] = kk_ref[nsub * blk:(nsub + 1) * blk, :]
        vv_ref[0:blk, :] = vv_ref[nsub * blk:(nsub + 1) * blk, :]


def _attn_body(n, q_ref, k_ref, v_ref, o_ref, lse_ref, kk_ref, vv_ref, *, rows_per_residue):
    blk = ATTN_BACK
    nsub = ATTN_SUB_BLOCKS

    qi = lax.broadcasted_iota(jnp.int32, (blk, 2 * blk), 0)
    kj = lax.broadcasted_iota(jnp.int32, (blk, 2 * blk), 1)
    q_pos = _block_position(qi, rows_per_residue, blk)
    k_pos = _block_position(kj & (blk - 1), rows_per_residue, blk) - jnp.where(kj < blk, blk, 0)
    dist = q_pos - k_pos
    band = (dist >= 0) & (dist <= ATTN_BACK)
    band_bias = jnp.where(band, 0.0, NEG_BIG)
    first_bias = jnp.where(band & (kj >= jnp.where(n > 0, 0, blk)), 0.0, NEG_BIG)
    lane_head = lax.shift_right_logical(lax.broadcasted_iota(jnp.int32, (blk, HEADS * LSE_LANES), 1),
                                        LSE_LANES.bit_length() - 1)
    lse_tiles = [jnp.zeros((blk, HEADS * LSE_LANES), F32) for _ in range(nsub)]

    for h in range(HEADS):
        sl = slice(h * HEAD_DIM, (h + 1) * HEAD_DIM)
        for s, (kb, vb) in enumerate(zip(_sub_blocks(k_ref, sl, nsub), _sub_blocks(v_ref, sl, nsub))):
            rows = slice((s + 1) * blk, (s + 2) * blk)
            kk_ref[rows, sl] = kb
            vv_ref[rows, sl] = vb
        outs = []
        for s, qb in enumerate(_sub_blocks(q_ref, sl, nsub)):
            keys = slice(s * blk, (s + 2) * blk)
            sc = lax.dot_general(qb, kk_ref[keys, sl], (((1,), (1,)), ((), ())), preferred_element_type=F32)
            sc = sc + (first_bias if s == 0 else band_bias)
            m = jnp.max(sc, axis=-1, keepdims=True)
            p = jnp.exp2(sc - m)
            den = jnp.sum(p, axis=-1, keepdims=True)
            o = jnp.dot(p.astype(BF16), vv_ref[keys, sl], preferred_element_type=F32)
            outs.append(o / den)
            lse = m * math.log(2.0) + jnp.log(den)
            lse_tiles[s] = jnp.where(lane_head == h, lse, lse_tiles[s])
        _store_sub_blocks(o_ref, sl, outs)
    _store_sub_blocks(lse_ref, slice(None), lse_tiles)


def _attn_operands(proj, g, dilation):
    batch, _, per, _ = proj.shape
    a = RES // dilation
    rows = ATTN_BACK // a
    rows_blk = rows * ATTN_SUB_BLOCKS
    nblk = per // rows_blk
    view = lambda arr: arr.reshape(arr.shape[:-3] + (a, dilation) + arr.shape[-2:])

    def spec(width, col_block):
        return pl.BlockSpec((None, a, None, rows_blk, width),
                            lambda i: (i // (dilation * nblk), 0, (i // nblk) % dilation, i % nblk, col_block))

    lse_w = HEADS * LSE_LANES
    proj_v = view(proj)
    key_rows = (ATTN_SUB_BLOCKS + 1) * ATTN_BACK
    return dict(
        inputs=[proj_v, proj_v, proj_v],
        in_specs=[spec(GROUP_WIDTH, which * N_GROUPS + g) for which in range(3)],
        out_specs=[spec(GROUP_WIDTH, 0), spec(lse_w, 0)],
        out_shape=[jax.ShapeDtypeStruct((batch, a, dilation, per, GROUP_WIDTH), BF16),
                   jax.ShapeDtypeStruct((batch, a, dilation, per, lse_w), F32)],
        scratch=[pltpu.VMEM((key_rows, GROUP_WIDTH), BF16), pltpu.VMEM((key_rows, GROUP_WIDTH), BF16)],
        rows_per_residue=rows, nblk=nblk)


def _hgrn_cumsum_matrix():
    c = HGRN_CHUNK
    i = np.arange(c)
    per = c // RES
    pos = (i % per) * RES + i // per
    return (pos[None, :] <= pos[:, None]).astype(np.float32)


def _level_reference(b3, half):
    res, per, dk = b3.shape
    if half < res:
        two = 2 * half
        parts = [jnp.broadcast_to(b3[blk * two + half - 1][None], (two, per, dk)) for blk in range(res // two)]
        return parts[0] if len(parts) == 1 else jnp.concatenate(parts, axis=0)
    last = b3[res - 1]
    step = half // res
    u = lax.broadcasted_iota(jnp.int32, (per, dk), 0)
    ref = None
    for blk in reversed(range(per // (2 * step))):
        row = jnp.broadcast_to(last[blk * 2 * step + step - 1][None], (per, dk))
        ref = row if ref is None else jnp.where(u < (blk + 1) * 2 * step, row, ref)
    return jnp.broadcast_to(ref[None], (res, per, dk))


def _scale_valid_rows(t3, e, half, lower):
    res = t3.shape[0]
    if half >= res:
        return t3 * e
    parts = []
    for lo in range(0, res, 2 * half):
        up, dn = slice(lo, lo + half), slice(lo + half, lo + 2 * half)
        parts += [t3[up], t3[dn] * e[dn]] if lower else [t3[up] * e[up], t3[dn]]
    return jnp.concatenate(parts, axis=0)


def _hgrn_body(q_ref, logf_ref, v_ref, gate_ref, keep_ref, gain_ref, tri_ref, o_ref, st_ref):
    c = HGRN_CHUNK
    dk = HEAD_DIM
    per = c // RES

    g_all = logf_ref[...].astype(F32)
    k_all = keep_ref[...].astype(F32)
    q_all = q_ref[...].astype(F32)
    v_all = v_ref[...].astype(F32)
    gate_all = gate_ref[...].astype(F32)

    ti = _block_position(lax.broadcasted_iota(jnp.int32, (c, c), 0), per, c)
    si = _block_position(lax.broadcasted_iota(jnp.int32, (c, c), 1), per, c)
    diag_mask = ti == si
    level_masks = []
    for half in HGRN_LEVELS:
        shift = half.bit_length()
        level_masks.append((lax.shift_right_logical(ti, shift) == lax.shift_right_logical(si, shift))
                           & ((ti & half) != 0) & ((si & half) == 0))
    tri = tri_ref[...]

    n_chunks = o_ref.shape[1] // per
    outs = [[None] * n_chunks for _ in range(HEADS)]
    for ci in range(n_chunks):
        rows = slice(ci * per, (ci + 1) * per)
        g = g_all[:, rows, :].reshape(c, GROUP_WIDTH)
        g_hi = g.astype(BF16)
        r1 = g - g_hi.astype(F32)
        g_mid = r1.astype(BF16)
        g_lo = (r1 - g_mid.astype(F32)).astype(BF16)
        cum = (jnp.dot(tri, g_hi, preferred_element_type=F32)
               + jnp.dot(tri, g_mid, preferred_element_type=F32)
               + jnp.dot(tri, g_lo, preferred_element_type=F32))

        for h in range(HEADS):
            sl = slice(h * dk, (h + 1) * dk)
            q3 = q_all[:, rows, sl]
            k3 = k_all[:, rows, sl]
            q = q3.reshape(c, dk)
            k = k3.reshape(c, dk)
            v = v_all[:, rows, sl].reshape(c, dk).astype(BF16)
            b = cum[:, sl]
            b3 = b.reshape(RES, per, dk)
            b_last = b[c - 1:c, :]
            st = st_ref[h]

            q_dec = (q * jnp.exp(b)).astype(BF16)
            o = lax.dot_general(q_dec, st.astype(BF16), (((1,), (1,)), ((), ())),
                                preferred_element_type=F32)

            a = jnp.where(diag_mask, jnp.sum(q * k, axis=-1, keepdims=True), 0.0)
            for li, half in enumerate(HGRN_LEVELS):
                e = jnp.exp2(jnp.abs(b3 - _level_reference(b3, half)) * -LOG2E)
                q_l = _scale_valid_rows(q3, e, half, lower=True).reshape(c, dk).astype(BF16)
                k_l = _scale_valid_rows(k3, e, half, lower=False).reshape(c, dk).astype(BF16)
                p_l = lax.dot_general(q_l, k_l, (((1,), (1,)), ((), ())), preferred_element_type=F32)
                a = jnp.where(level_masks[li], p_l, a)
            o = o + jnp.dot(a.astype(BF16), v, preferred_element_type=F32)

            k_dec = (k * jnp.exp(b_last - b)).astype(BF16)
            st_ref[h] = st * jnp.exp(b_last) + lax.dot_general(
                v, k_dec, (((0,), (0,)), ((), ())), preferred_element_type=F32)

            r = o * _rms_scale(o) * gain_ref[:, sl] * gate_all[:, rows, sl].reshape(c, dk)
            outs[h][ci] = r.reshape(RES, per, dk)

    for h in range(HEADS):
        o_ref[:, :, h * dk:(h + 1) * dk] = jnp.concatenate(outs[h], axis=1).astype(o_ref.dtype)


def _hgrn_operands(proj, keep, gain, *, first_col_block, rows):
    batch, _, per, _ = proj.shape
    tiles = per // rows
    tri = jnp.asarray(_hgrn_cumsum_matrix(), BF16)

    def col_spec(col_block):
        return pl.BlockSpec((None, RES, rows, GROUP_WIDTH), lambda i: (i // tiles, 0, i % tiles, col_block))

    const = lambda i: (0, 0)
    return dict(
        inputs=[proj, proj, proj, proj, keep, gain, tri],
        in_specs=[col_spec(first_col_block + k) for k in range(4)]
        + [col_spec(0), pl.BlockSpec(gain.shape, const), pl.BlockSpec(tri.shape, const)],
        out_specs=[col_spec(0)],
        out_shape=[jax.ShapeDtypeStruct((batch, RES, per, GROUP_WIDTH), BF16)],
        scratch=[pltpu.VMEM((HEADS, HEAD_DIM, HEAD_DIM), F32)],
        tiles=tiles)


def _mixers(proj, keep, gain, *, hgrn_col_block):
    batch, _, per, _ = proj.shape
    rows = ATTN_SUB_BLOCKS * ATTN_BACK // RES
    steps = batch * per // rows
    parts = [_attn_operands(proj, g, dilation) for g, (_, dilation) in enumerate(ATTN_GROUPS)]
    parts.append(_hgrn_operands(proj, keep, gain, first_col_block=hgrn_col_block, rows=rows))
    n_in = [len(p["inputs"]) for p in parts]
    n_out = [len(p["out_shape"]) for p in parts]
    n_scr = [len(p["scratch"]) for p in parts]

    def kernel_body(*refs):
        ins, rest = refs[:sum(n_in)], refs[sum(n_in):]
        outs, scr = rest[:sum(n_out)], rest[sum(n_out):]
        i = pl.program_id(0)
        take = lambda seq, counts, k: seq[sum(counts[:k]):sum(counts[:k + 1])]
        for k, part in enumerate(parts[:-1]):
            _attn_carry(i % part["nblk"], *take(scr, n_scr, k))
        k = len(parts) - 1

        @pl.when(i % parts[k]["tiles"] == 0)
        def _():
            st_ref = take(scr, n_scr, k)[0]
            st_ref[...] = jnp.zeros(st_ref.shape, F32)

        for k, part in enumerate(parts[:-1]):
            _attn_body(i % part["nblk"], *take(ins, n_in, k), *take(outs, n_out, k), *take(scr, n_scr, k),
                       rows_per_residue=part["rows_per_residue"])
        k = len(parts) - 1
        _hgrn_body(*take(ins, n_in, k), *take(outs, n_out, k), *take(scr, n_scr, k))

    flat = lambda key: [v for p in parts for v in p[key]]
    res = pl.pallas_call(
        kernel_body,
        grid=(steps,),
        in_specs=flat("in_specs"),
        out_specs=flat("out_specs"),
        out_shape=flat("out_shape"),
        scratch_shapes=flat("scratch"),
        compiler_params=_params(("arbitrary",)),
        name="token_mixers",
    )(*flat("inputs"))
    lse_w = HEADS * LSE_LANES
    outs = [res[2 * g].reshape(batch, RES, per, GROUP_WIDTH) for g in range(N_GROUPS)]
    lses = [res[2 * g + 1].reshape(batch, RES, per, lse_w) for g in range(N_GROUPS)]
    return outs, lses, res[2 * N_GROUPS]


def _merge_kernel(o1_ref, o2_ref, o3_ref, l1_ref, l2_ref, l3_ref, r_ref,
                  ga0_ref, ga1_ref, gr0_ref, gr1_ref, x_ref,
                  wpa_ref, wpr_ref, wo_ref, gpost_ref, gffn_ref,
                  h_ref, a2_ref, attn_s):
    tm = x_ref.shape[0]
    per = tm // RES
    flat = lambda ref: ref[...].reshape(tm, ref.shape[-1])

    la, lb, lc = flat(l1_ref), flat(l2_ref), flat(l3_ref)
    lm = jnp.maximum(la, jnp.maximum(lb, lc))
    ea, eb, ec = jnp.exp(la - lm), jnp.exp(lb - lm), jnp.exp(lc - lm)
    den = ea + eb + ec
    wa, wb, wc = ea / den, eb / den, ec / den
    o1, o2, o3 = flat(o1_ref), flat(o2_ref), flat(o3_ref)
    for h in range(HEADS):
        sl = slice(h * HEAD_DIM, (h + 1) * HEAD_DIM)
        c0 = h * LSE_LANES
        mixed = (wa[:, c0:c0 + 1] * o1[:, sl].astype(F32)
                 + wb[:, c0:c0 + 1] * o2[:, sl].astype(F32)
                 + wc[:, c0:c0 + 1] * o3[:, sl].astype(F32))
        attn_s[:, sl] = mixed.astype(BF16)

    tok = lax.broadcasted_iota(jnp.int32, (tm, tm), 0)
    src = _block_position(lax.broadcasted_iota(jnp.int32, (tm, tm), 1), per, tm)
    unpermute = jnp.where(src == tok, 1.0, 0.0).astype(BF16)

    attn = attn_s[...]
    r = flat(r_ref)
    half_w = GROUP_WIDTH
    halves = [slice(half * half_w, (half + 1) * half_w) for half in range(2)]
    branch = [(jnp.dot(attn, wpa_ref[:, cs], preferred_element_type=F32),
               jnp.dot(r, wpr_ref[:, cs], preferred_element_type=F32)) for cs in halves]
    z = None
    for cs, (ya, yr), (ga_ref, gr_ref) in zip(halves, branch, ((ga0_ref, gr0_ref), (ga1_ref, gr1_ref))):
        y = _sigmoid(flat(ga_ref).astype(F32)) * ya + _sigmoid(flat(gr_ref).astype(F32)) * yr
        y_nat = jnp.dot(unpermute, y.astype(BF16), preferred_element_type=F32).astype(BF16)
        part = jnp.dot(y_nat, wo_ref[cs, :], preferred_element_type=F32)
        z = part if z is None else z + part
    h1 = x_ref[...] + z * _rms_scale(z) * gpost_ref[...]
    h_ref[...] = h1
    a2_ref[...] = (h1 * _rms_scale(h1) * gffn_ref[...]).astype(BF16)


def _merge(outs, lses, r_out, proj, x2, wpa, wpr, wo, gpost, gffn, *, gate_col_block, tm=256):
    t, d = x2.shape
    batch, _, per, _ = proj.shape
    rows = tm // RES
    tiles_per_seq = per // rows
    const = lambda i: (0, 0)
    nat = pl.BlockSpec((tm, d), lambda i: (i, 0))

    def lay(width, col_block=0):
        return pl.BlockSpec((None, RES, rows, width),
                            lambda i: (i // tiles_per_seq, 0, i % tiles_per_seq, col_block))

    def resident(shape):
        return pl.BlockSpec(shape, const, pipeline_mode=pl.Buffered(1))

    lse_w = HEADS * LSE_LANES
    return pl.pallas_call(
        _merge_kernel,
        grid=(t // tm,),
        in_specs=[lay(GROUP_WIDTH)] * 3 + [lay(lse_w)] * 3 + [lay(GROUP_WIDTH)]
        + [lay(GROUP_WIDTH, gate_col_block + k) for k in range(4)]
        + [nat, resident(wpa.shape), resident(wpr.shape), resident(wo.shape),
           resident(gpost.shape), resident(gffn.shape)],
        out_specs=[nat, nat],
        out_shape=[jax.ShapeDtypeStruct((t, d), F32), jax.ShapeDtypeStruct((t, d), BF16)],
        scratch_shapes=[pltpu.VMEM((tm, GROUP_WIDTH), BF16)],
        compiler_params=_params(("parallel",)),
        name="merge_out_proj",
    )(*outs, *lses, r_out, proj, proj, proj, proj, x2, wpa, wpr, wo, gpost, gffn)


def _ffn_hidden_kernel(a_ref, wg_ref, wu_ref, side_ref, o_ref, cast_ref, wg_b, wu_b):
    @pl.when(pl.program_id(1) == 0)
    def _():
        wg_b[...] = wg_ref[...].astype(BF16)
        wu_b[...] = wu_ref[...].astype(BF16)

    cast_ref[...] = side_ref[...].astype(BF16)
    a = a_ref[...]
    gt = jnp.dot(a, wg_b[...], preferred_element_type=F32)
    up = jnp.dot(a, wu_b[...], preferred_element_type=F32)
    o_ref[...] = (_silu(gt) * up).astype(o_ref.dtype)


def _ffn_out_kernel(hid_ref, wd_ref, h_ref, gain_ref, o_ref):
    ff = jnp.dot(hid_ref[...], wd_ref[...], preferred_element_type=F32)
    o_ref[...] = h_ref[...] + ff * _rms_scale(ff) * gain_ref[...]


def _ffn(a2, w_gate_up, w_down, h1, gain, *, tm_hidden=1024, tf=512, tm_out=256):
    t, d = a2.shape
    d_ff = w_down.shape[0]
    nf = d_ff // tf
    n_row = t // tm_hidden
    side = _side_cast_specs([w_down], nf * n_row, lambda j, i: j * n_row + i)

    def weight_block(first, lead):
        return pl.BlockSpec((d, tf), lambda j, i: (
            0, first + jnp.minimum(j + (i >= n_row - lead).astype(jnp.int32), nf - 1)))

    hidden, w_down = pl.pallas_call(
        _ffn_hidden_kernel,
        grid=(nf, n_row),
        in_specs=[
            pl.BlockSpec((tm_hidden, d), lambda j, i: (i, 0)),
            weight_block(0, 2),
            weight_block(nf, 1),
        ] + side,
        out_specs=[pl.BlockSpec((tm_hidden, tf), lambda j, i: (i, j))] + side,
        out_shape=[jax.ShapeDtypeStruct((t, d_ff), BF16), jax.ShapeDtypeStruct(w_down.shape, BF16)],
        scratch_shapes=[pltpu.VMEM((d, tf), BF16), pltpu.VMEM((d, tf), BF16)],
        compiler_params=_params(("arbitrary", "arbitrary")),
        name="swiglu_hidden",
    )(a2, w_gate_up, w_gate_up, w_down)
    rows = pl.BlockSpec((tm_out, d), lambda i: (i, 0))
    return pl.pallas_call(
        _ffn_out_kernel,
        grid=(t // tm_out,),
        in_specs=[
            pl.BlockSpec((tm_out, d_ff), lambda i: (i, 0)),
            pl.BlockSpec((d_ff, d), lambda i: (0, 0), pipeline_mode=pl.Buffered(1)),
            rows,
            pl.BlockSpec((1, d), lambda i: (0, 0)),
        ],
        out_specs=rows,
        out_shape=jax.ShapeDtypeStruct((t, d), F32),
        compiler_params=_params(("parallel",)),
        name="swiglu_out",
    )(hidden, w_down, h1, gain)


def _rotary_table(seq):
    inv_freq = ROPE_THETA ** (-np.arange(0, HEAD_DIM, 2, dtype=np.float64) / HEAD_DIM)
    ang = np.arange(seq, dtype=np.float64)[:, None] * inv_freq[None, :]
    table = np.concatenate([np.cos(ang), np.cos(ang), -np.sin(ang), np.sin(ang)], axis=-1)
    return jnp.asarray(table.reshape(seq // RES, RES, 2 * HEAD_DIM).transpose(1, 0, 2), F32)


def kernel(x, w_in, w_attn_branch, w_hgrn_branch, w_mix_out, hgrn_lower_bounds, hgrn_norm_gain,
           norm_mix_pre, norm_mix_post, w_ffn_gate_up, w_ffn_down, norm_ffn_pre, norm_ffn_post):
    batch, seq, d = x.shape
    assert w_in.shape[0] == 1, "single-layer block"
    x2 = x.reshape(batch * seq, d)
    row = lambda v: v[0].reshape(1, -1).astype(F32)

    for window, dilation in ATTN_GROUPS:
        assert window // dilation == ATTN_BACK and seq % window == 0 and RES % dilation == 0
    assert w_in.shape[2] == (QKV_BLOCKS + 8) * GROUP_WIDTH
    rotary = _rotary_table(seq)
    f32 = lambda wt: wt[0].astype(F32)
    bf16 = lambda wt: wt[0].astype(BF16)
    proj, keep = _in_proj(x2, row(norm_mix_pre), f32(w_in), rotary, hgrn_lower_bounds.astype(F32),
                          batch, seq)

    outs, lses, r_out = _mixers(proj, keep, row(hgrn_norm_gain), hgrn_col_block=QKV_BLOCKS)

    h1, a2 = _merge(outs, lses, r_out, proj, x2, bf16(w_attn_branch), bf16(w_hgrn_branch), bf16(w_mix_out),
                    row(norm_mix_post), row(norm_ffn_pre), gate_col_block=QKV_BLOCKS + 4)

    out = _ffn(a2, f32(w_ffn_gate_up), f32(w_ffn_down), h1, row(norm_ffn_post))
    return out.reshape(batch, seq, d)
```

```python
import math

import numpy as np
import jax
import jax.numpy as jnp
from jax import lax
from jax.experimental import pallas as pl
from jax.experimental.pallas import tpu as pltpu

F32 = jnp.float32
BF16 = jnp.bfloat16

HEAD_DIM = 128
ATTN_GROUPS = ((128, 1), (512, 4), (2048, 16))
N_GROUPS = len(ATTN_GROUPS)
HEADS = 8
GROUP_WIDTH = HEADS * HEAD_DIM
ROTARY_BLOCKS = 2 * N_GROUPS
QKV_BLOCKS = 3 * N_GROUPS
FORGET_BLOCK = QKV_BLOCKS + 1
ATTN_BACK = 128
ATTN_SUB_BLOCKS = 2
RES = 16
ROPE_THETA = 10000.0
NORM_EPS = 1e-6
HGRN_CHUNK = 128
HGRN_LEVELS = (64, 32, 16, 8, 4, 2, 1)
LSE_LANES = 16
BF16_SUBLANES = 16
NEG_BIG = -1e30
PERMUTE_GROUP = 256
LOG2E = math.log2(math.e)

VMEM_LIMIT_BYTES = 56 * 1024 * 1024


def _params(semantics):
    return pltpu.CompilerParams(dimension_semantics=semantics, vmem_limit_bytes=VMEM_LIMIT_BYTES)


def _rms_scale(v):
    return lax.rsqrt(jnp.mean(v * v, axis=-1, keepdims=True) + NORM_EPS)


def _sigmoid(v):
    return 0.5 * jnp.tanh(0.5 * v) + 0.5


def _silu(v):
    return v * _sigmoid(v)


def _side_cast_specs(weights, steps, step_index):
    specs = []
    for w in weights:
        rows, cols = w.shape
        block_rows = BF16_SUBLANES * pl.cdiv(rows, BF16_SUBLANES * steps)
        assert rows % block_rows == 0
        last = rows // block_rows - 1
        specs.append(pl.BlockSpec((block_rows, cols),
                                  lambda *idx, last=last: (jnp.minimum(step_index(*idx), last), 0)))
    return specs


def _block_position(i, rows_per_residue, block_rows):
    shift = rows_per_residue.bit_length() - 1
    n_res_shift = (block_rows // rows_per_residue).bit_length() - 1
    return ((i & (rows_per_residue - 1)) << n_res_shift) | lax.shift_right_logical(i, shift)


def _in_proj_kernel(x_lo_ref, x_hi_ref, gain_ref, w_ref, rot_ref, hlb_ref, o_ref, keep_ref, a_ref):
    half_rows = x_lo_ref.shape[0]
    tm = 2 * half_rows
    per_res = tm // RES
    grp = PERMUTE_GROUP
    per_grp = grp // RES
    j = pl.program_id(1)

    @pl.when(j == 0)
    def _():
        row_token = _block_position(lax.broadcasted_iota(jnp.int32, (grp, grp), 0), per_grp, grp)
        permute = jnp.where(row_token == lax.broadcasted_iota(jnp.int32, (grp, grp), 1), 1.0, 0.0).astype(BF16)
        gain = gain_ref[...]
        for gi in range(tm // grp):
            x_ref, first = (x_lo_ref, gi * grp) if gi * grp < half_rows else (x_hi_ref, gi * grp - half_rows)
            xg = x_ref[first:first + grp, :]
            ag = (xg * _rms_scale(xg) * gain).astype(BF16)
            pg = jnp.dot(permute, ag, preferred_element_type=F32).astype(BF16)
            for r in range(RES):
                dst = r * per_res + gi * per_grp
                a_ref[dst:dst + per_grp, :] = pg[r * per_grp:(r + 1) * per_grp, :]

    def project():
        return jnp.dot(a_ref[...], w_ref[...].astype(BF16), preferred_element_type=F32)

    def store(ref, val):
        ref[...] = val.reshape(RES, per_res, val.shape[-1]).astype(ref.dtype)

    @pl.when(j < ROTARY_BLOCKS)
    def _():
        res = project()
        scale = jnp.where(j < N_GROUPS, HEAD_DIM ** -0.5 * LOG2E, 1.0)
        table = rot_ref[...].reshape(tm, 2 * HEAD_DIM) * scale
        cos = table[:, :HEAD_DIM]
        sin = table[:, HEAD_DIM:]
        heads = []
        for h in range(HEADS):
            t = res[:, h * HEAD_DIM:(h + 1) * HEAD_DIM]
            heads.append(t * cos + pltpu.roll(t, HEAD_DIM // 2, 1) * sin)
        store(o_ref, jnp.concatenate(heads, axis=1))

    is_swish = (j == FORGET_BLOCK - 1) | (j == FORGET_BLOCK + 2)

    @pl.when(is_swish)
    def _():
        store(o_ref, _silu(project()))

    @pl.when((j >= ROTARY_BLOCKS) & (j != FORGET_BLOCK) & jnp.logical_not(is_swish))
    def _():
        store(o_ref, project())

    @pl.when(j == FORGET_BLOCK)
    def _():
        res = project()
        h0 = hlb_ref[0:1, :]
        h1 = hlb_ref[1:2, :]
        hm = jnp.maximum(h0, h1)
        e0 = jnp.exp(h0 - hm)
        e1 = jnp.exp(h1 - hm)
        lb = e0 / (e0 + e1)
        f = lb + (1.0 - lb) * jax.nn.sigmoid(res)
        store(o_ref, jnp.log(f))
        store(keep_ref, 1.0 - f)


def _in_proj(x2, gain, w, rotary, lower_bounds, batch, seq, *, tm=1024, tn=GROUP_WIDTH):
    assert lower_bounds.shape[0] == 2, "forget-gate bound rows: DEPTH + 1 with DEPTH == 1"
    t, d = x2.shape
    n = w.shape[1]
    tiles_per_seq = seq // tm
    table = pl.BlockSpec((RES, tm // RES, 2 * HEAD_DIM), lambda i, j: (0, i % tiles_per_seq, 0))
    lay = lambda col: pl.BlockSpec((None, RES, tm // RES, tn),
                                   lambda i, j: (i // tiles_per_seq, 0, i % tiles_per_seq, col(j)))
    n_row, n_col = t // tm, n // tn

    def x_half(which, lead):
        def index(i, j):
            tile = jnp.minimum(i + (j >= n_col - lead).astype(jnp.int32), n_row - 1)
            return (2 * tile + which, 0)
        return pl.BlockSpec((tm // 2, d), index)

    return pl.pallas_call(
        _in_proj_kernel,
        grid=(n_row, n_col),
        in_specs=[
            x_half(0, 2), x_half(1, 1),
            pl.BlockSpec((1, d), lambda i, j: (0, 0)),
            pl.BlockSpec((d, tn), lambda i, j: (0, j)),
            table,
            pl.BlockSpec(lower_bounds.shape, lambda i, j: (0, 0)),
        ],
        out_specs=[lay(lambda j: j), lay(lambda j: 0)],
        out_shape=[jax.ShapeDtypeStruct((batch, RES, seq // RES, n), BF16),
                   jax.ShapeDtypeStruct((batch, RES, seq // RES, tn), BF16)],
        scratch_shapes=[pltpu.VMEM((tm, d), BF16)],
        compiler_params=_params(("arbitrary", "arbitrary")),
        name="in_proj",
    )(x2, x2, gain, w, rotary, lower_bounds)


def _sub_blocks(ref, sl, nsub):
    v = ref[:, :, sl]
    a, u, w = v.shape
    step = u // nsub
    if step % BF16_SUBLANES:
        v = v.astype(F32)
    return [v[:, s * step:(s + 1) * step, :].reshape(a * step, w).astype(ref.dtype) for s in range(nsub)]


def _store_sub_blocks(ref, sl, blocks):
    a = ref.shape[0]
    parts = [b.reshape(a, b.shape[0] // a, b.shape[1]) for b in blocks]
    ref[:, :, sl] = jnp.concatenate(parts, axis=1).astype(ref.dtype)


def _attn_carry(n, kk_ref, vv_ref):
    blk = ATTN_BACK
    nsub = ATTN_SUB_BLOCKS

    @pl.when(n == 0)
    def _():
        kk_ref[0:blk, :] = jnp.zeros((blk, GROUP_WIDTH), BF16)
        vv_ref[0:blk, :] = jnp.zeros((blk, GROUP_WIDTH), BF16)

    @pl.when(n > 0)
    def _():
        kk_ref[0:blk, :] = kk_ref[nsub * blk:(nsub + 1) * blk, :]
        vv_ref[0:blk, :] = vv_ref[nsub * blk:(nsub + 1) * blk, :]


def _attn_body(n, q_ref, k_ref, v_ref, o_ref, lse_ref, kk_ref, vv_ref, *, rows_per_residue):
    blk = ATTN_BACK
    nsub = ATTN_SUB_BLOCKS

    qi = lax.broadcasted_iota(jnp.int32, (blk, 2 * blk), 0)
    kj = lax.broadcasted_iota(jnp.int32, (blk, 2 * blk), 1)
    q_pos = _block_position(qi, rows_per_residue, blk)
    k_pos = _block_position(kj & (blk - 1), rows_per_residue, blk) - jnp.where(kj < blk, blk, 0)
    dist = q_pos - k_pos
    band = (dist >= 0) & (dist <= ATTN_BACK)
    band_bias = jnp.where(band, 0.0, NEG_BIG)
    first_bias = jnp.where(band & (kj >= jnp.where(n > 0, 0, blk)), 0.0, NEG_BIG)
    layout_res, rows_blk, _ = lse_ref.shape
    step = rows_blk // nsub

    for h in range(HEADS):
        sl = slice(h * HEAD_DIM, (h + 1) * HEAD_DIM)
        for s, (kb, vb) in enumerate(zip(_sub_blocks(k_ref, sl, nsub), _sub_blocks(v_ref, sl, nsub))):
            rows = slice((s + 1) * blk, (s + 2) * blk)
            kk_ref[rows, sl] = kb
            vv_ref[rows, sl] = vb
        outs = []
        for s, qb in enumerate(_sub_blocks(q_ref, sl, nsub)):
            keys = slice(s * blk, (s + 2) * blk)
            sc = lax.dot_general(qb, kk_ref[keys, sl], (((1,), (1,)), ((), ())), preferred_element_type=F32)
            sc = sc + (first_bias if s == 0 else band_bias)
            m = jnp.max(sc, axis=-1, keepdims=True)
            p = jnp.exp2(sc - m)
            den = jnp.sum(p, axis=-1, keepdims=True)
            o = jnp.dot(p.astype(BF16), vv_ref[keys, sl], preferred_element_type=F32)
            outs.append(o / den)
            lse = (m * math.log(2.0) + jnp.log(den)).reshape(layout_res, step, 1)
            lse_ref[:, s * step:(s + 1) * step, h * LSE_LANES:(h + 1) * LSE_LANES] = jnp.broadcast_to(
                lse, (layout_res, step, LSE_LANES))
        _store_sub_blocks(o_ref, sl, outs)


def _attn_operands(proj, g, dilation):
    batch, _, per, _ = proj.shape
    a = RES // dilation
    rows = ATTN_BACK // a
    rows_blk = rows * ATTN_SUB_BLOCKS
    nblk = per // rows_blk
    view = lambda arr: arr.reshape(arr.shape[:-3] + (a, dilation) + arr.shape[-2:])

    def spec(width, col_block):
        return pl.BlockSpec((None, a, None, rows_blk, width),
                            lambda i: (i // (dilation * nblk), 0, (i // nblk) % dilation, i % nblk, col_block))

    lse_w = HEADS * LSE_LANES
    proj_v = view(proj)
    key_rows = (ATTN_SUB_BLOCKS + 1) * ATTN_BACK
    return dict(
        inputs=[proj_v, proj_v, proj_v],
        in_specs=[spec(GROUP_WIDTH, which * N_GROUPS + g) for which in range(3)],
        out_specs=[spec(GROUP_WIDTH, 0), spec(lse_w, 0)],
        out_shape=[jax.ShapeDtypeStruct((batch, a, dilation, per, GROUP_WIDTH), BF16),
                   jax.ShapeDtypeStruct((batch, a, dilation, per, lse_w), F32)],
        scratch=[pltpu.VMEM((key_rows, GROUP_WIDTH), BF16), pltpu.VMEM((key_rows, GROUP_WIDTH), BF16)],
        rows_per_residue=rows, nblk=nblk)


def _hgrn_cumsum_matrix():
    c = HGRN_CHUNK
    i = np.arange(c)
    per = c // RES
    pos = (i % per) * RES + i // per
    return (pos[None, :] <= pos[:, None]).astype(np.float32)


def _level_reference(b3, half):
    res, per, dk = b3.shape
    if half < res:
        two = 2 * half
        parts = [jnp.broadcast_to(b3[blk * two + half - 1][None], (two, per, dk)) for blk in range(res // two)]
        return parts[0] if len(parts) == 1 else jnp.concatenate(parts, axis=0)
    last = b3[res - 1]
    step = half // res
    u = lax.broadcasted_iota(jnp.int32, (per, dk), 0)
    ref = None
    for blk in reversed(range(per // (2 * step))):
        row = jnp.broadcast_to(last[blk * 2 * step + step - 1][None], (per, dk))
        ref = row if ref is None else jnp.where(u < (blk + 1) * 2 * step, row, ref)
    return jnp.broadcast_to(ref[None], (res, per, dk))


def _level_decay_exponent(b3, half):
    ref = _level_reference(b3, half)
    res, per, dk = b3.shape
    if half < res:
        parts = []
        for lo in range(0, res, 2 * half):
            up, dn = slice(lo, lo + half), slice(lo + half, lo + 2 * half)
            parts += [ref[up] - b3[up], b3[dn] - ref[dn]]
        return jnp.concatenate(parts, axis=0)
    step = half // res
    u = lax.broadcasted_iota(jnp.int32, (per, dk), 0)
    return (b3 - ref) * jnp.where((u & step) != 0, 1.0, -1.0)


def _hgrn_body(q_ref, logf_ref, v_ref, gate_ref, keep_ref, gain_ref, tri_ref, o_ref, st_ref):
    c = HGRN_CHUNK
    dk = HEAD_DIM
    per = c // RES

    g_all = logf_ref[...].astype(F32)
    k_all = keep_ref[...].astype(F32)
    q_all = q_ref[...].astype(F32)
    v_all = v_ref[...].astype(F32)
    gate_all = gate_ref[...].astype(F32)

    ti = _block_position(lax.broadcasted_iota(jnp.int32, (c, c), 0), per, c)
    si = _block_position(lax.broadcasted_iota(jnp.int32, (c, c), 1), per, c)
    top_bit = 31 - lax.clz(ti ^ si)
    n_levels = len(HGRN_LEVELS)
    level_id = jnp.where(ti > si, top_bit, jnp.where(ti == si, n_levels, n_levels + 1))
    level_b = level_id.astype(F32).astype(BF16)
    tri = tri_ref[...]

    n_chunks = o_ref.shape[1] // per
    outs = [[None] * n_chunks for _ in range(HEADS)]
    for ci in range(n_chunks):
        rows = slice(ci * per, (ci + 1) * per)
        g = g_all[:, rows, :].reshape(c, GROUP_WIDTH)
        g_hi = g.astype(BF16)
        r1 = g - g_hi.astype(F32)
        g_mid = r1.astype(BF16)
        g_lo = (r1 - g_mid.astype(F32)).astype(BF16)
        cum = (jnp.dot(tri, g_hi, preferred_element_type=F32)
               + jnp.dot(tri, g_mid, preferred_element_type=F32)
               + jnp.dot(tri, g_lo, preferred_element_type=F32))

        for h in range(HEADS):
            sl = slice(h * dk, (h + 1) * dk)
            q3 = q_all[:, rows, sl]
            k3 = k_all[:, rows, sl]
            q = q3.reshape(c, dk)
            k = k3.reshape(c, dk)
            v = v_all[:, rows, sl].reshape(c, dk).astype(BF16)
            b = cum[:, sl] * LOG2E
            b3 = b.reshape(RES, per, dk)
            b_last = b[c - 1:c, :]
            st = st_ref[h]

            q_dec = (q * jnp.exp2(b)).astype(BF16)
            o = lax.dot_general(q_dec, st.astype(BF16), (((1,), (1,)), ((), ())),
                                preferred_element_type=F32)

            diag = jnp.broadcast_to(jnp.sum(q * k, axis=-1, keepdims=True), (c, c)).astype(BF16)
            a = jnp.where(level_b == n_levels, diag, jnp.zeros((c, c), BF16))
            q_b = q.astype(BF16)
            k_b = k.astype(BF16)
            for half in HGRN_LEVELS:
                e = jnp.exp2(_level_decay_exponent(b3, half)).reshape(c, dk).astype(BF16)
                p_l = lax.dot_general(q_b * e, k_b * e, (((1,), (1,)), ((), ())), preferred_element_type=F32)
                a = jnp.where(level_b == half.bit_length() - 1, p_l.astype(BF16), a)
            o = o + jnp.dot(a, v, preferred_element_type=F32)

            k_dec = (k * jnp.exp2(b_last - b)).astype(BF16)
            st_ref[h] = st * jnp.exp2(b_last) + lax.dot_general(
                v, k_dec, (((0,), (0,)), ((), ())), preferred_element_type=F32)

            r = o * _rms_scale(o) * gain_ref[:, sl] * gate_all[:, rows, sl].reshape(c, dk)
            outs[h][ci] = r.reshape(RES, per, dk)

    for h in range(HEADS):
        o_ref[:, :, h * dk:(h + 1) * dk] = jnp.concatenate(outs[h], axis=1).astype(o_ref.dtype)


def _hgrn_operands(proj, keep, gain, *, first_col_block, rows):
    batch, _, per, _ = proj.shape
    tiles = per // rows
    tri = jnp.asarray(_hgrn_cumsum_matrix(), BF16)

    def col_spec(col_block):
        return pl.BlockSpec((None, RES, rows, GROUP_WIDTH), lambda i: (i // tiles, 0, i % tiles, col_block))

    const = lambda i: (0, 0)
    return dict(
        inputs=[proj, proj, proj, proj, keep, gain, tri],
        in_specs=[col_spec(first_col_block + k) for k in range(4)]
        + [col_spec(0), pl.BlockSpec(gain.shape, const), pl.BlockSpec(tri.shape, const)],
        out_specs=[col_spec(0)],
        out_shape=[jax.ShapeDtypeStruct((batch, RES, per, GROUP_WIDTH), BF16)],
        scratch=[pltpu.VMEM((HEADS, HEAD_DIM, HEAD_DIM), F32)],
        tiles=tiles)


def _mixers(proj, keep, gain, *, hgrn_col_block):
    batch, _, per, _ = proj.shape
    rows = ATTN_SUB_BLOCKS * ATTN_BACK // RES
    steps = batch * per // rows
    parts = [_attn_operands(proj, g, dilation) for g, (_, dilation) in enumerate(ATTN_GROUPS)]
    parts.append(_hgrn_operands(proj, keep, gain, first_col_block=hgrn_col_block, rows=rows))
    n_in = [len(p["inputs"]) for p in parts]
    n_out = [len(p["out_shape"]) for p in parts]
    n_scr = [len(p["scratch"]) for p in parts]

    def kernel_body(*refs):
        ins, rest = refs[:sum(n_in)], refs[sum(n_in):]
        outs, scr = rest[:sum(n_out)], rest[sum(n_out):]
        i = pl.program_id(0)
        take = lambda seq, counts, k: seq[sum(counts[:k]):sum(counts[:k + 1])]
        for k, part in enumerate(parts[:-1]):
            _attn_carry(i % part["nblk"], *take(scr, n_scr, k))
        k = len(parts) - 1

        @pl.when(i % parts[k]["tiles"] == 0)
        def _():
            st_ref = take(scr, n_scr, k)[0]
            st_ref[...] = jnp.zeros(st_ref.shape, F32)

        for k, part in enumerate(parts[:-1]):
            _attn_body(i % part["nblk"], *take(ins, n_in, k), *take(outs, n_out, k), *take(scr, n_scr, k),
                       rows_per_residue=part["rows_per_residue"])
        k = len(parts) - 1
        _hgrn_body(*take(ins, n_in, k), *take(outs, n_out, k), *take(scr, n_scr, k))

    flat = lambda key: [v for p in parts for v in p[key]]
    res = pl.pallas_call(
        kernel_body,
        grid=(steps,),
        in_specs=flat("in_specs"),
        out_specs=flat("out_specs"),
        out_shape=flat("out_shape"),
        scratch_shapes=flat("scratch"),
        compiler_params=_params(("arbitrary",)),
        name="token_mixers",
    )(*flat("inputs"))
    lse_w = HEADS * LSE_LANES
    outs = [res[2 * g].reshape(batch, RES, per, GROUP_WIDTH) for g in range(N_GROUPS)]
    lses = [res[2 * g + 1].reshape(batch, RES, per, lse_w) for g in range(N_GROUPS)]
    return outs, lses, res[2 * N_GROUPS]


def _merge_kernel(o1_ref, o2_ref, o3_ref, l1_ref, l2_ref, l3_ref, r_ref,
                  ga0_ref, ga1_ref, gr0_ref, gr1_ref, x_ref,
                  wpa_ref, wpr_ref, wo_ref, gpost_ref, gffn_ref,
                  h_ref, a2_ref, attn_s):
    tm = x_ref.shape[0]
    per = tm // RES
    flat = lambda ref: ref[...].reshape(tm, ref.shape[-1])

    la, lb, lc = flat(l1_ref), flat(l2_ref), flat(l3_ref)
    lm = jnp.maximum(la, jnp.maximum(lb, lc))
    ea, eb, ec = jnp.exp(la - lm), jnp.exp(lb - lm), jnp.exp(lc - lm)
    den = ea + eb + ec
    wa, wb, wc = ea / den, eb / den, ec / den
    o1, o2, o3 = flat(o1_ref), flat(o2_ref), flat(o3_ref)
    for h in range(HEADS):
        sl = slice(h * HEAD_DIM, (h + 1) * HEAD_DIM)
        c0 = h * LSE_LANES
        mixed = (wa[:, c0:c0 + 1] * o1[:, sl].astype(F32)
                 + wb[:, c0:c0 + 1] * o2[:, sl].astype(F32)
                 + wc[:, c0:c0 + 1] * o3[:, sl].astype(F32))
        attn_s[:, sl] = mixed.astype(BF16)

    tok = lax.broadcasted_iota(jnp.int32, (tm, tm), 0)
    src = _block_position(lax.broadcasted_iota(jnp.int32, (tm, tm), 1), per, tm)
    unpermute = jnp.where(src == tok, 1.0, 0.0).astype(BF16)

    attn = attn_s[...]
    r = flat(r_ref)
    half_w = GROUP_WIDTH
    halves = [slice(half * half_w, (half + 1) * half_w) for half in range(2)]
    branch = [(jnp.dot(attn, wpa_ref[:, cs], preferred_element_type=F32),
               jnp.dot(r, wpr_ref[:, cs], preferred_element_type=F32)) for cs in halves]
    z = None
    for cs, (ya, yr), (ga_ref, gr_ref) in zip(halves, branch, ((ga0_ref, gr0_ref), (ga1_ref, gr1_ref))):
        y = _sigmoid(flat(ga_ref).astype(F32)) * ya + _sigmoid(flat(gr_ref).astype(F32)) * yr
        y_nat = jnp.dot(unpermute, y.astype(BF16), preferred_element_type=F32).astype(BF16)
        part = jnp.dot(y_nat, wo_ref[cs, :], preferred_element_type=F32)
        z = part if z is None else z + part
    h1 = x_ref[...] + z * _rms_scale(z) * gpost_ref[...]
    h_ref[...] = h1
    a2_ref[...] = (h1 * _rms_scale(h1) * gffn_ref[...]).astype(BF16)


def _merge(outs, lses, r_out, proj, x2, wpa, wpr, wo, gpost, gffn, *, gate_col_block, tm=256):
    t, d = x2.shape
    batch, _, per, _ = proj.shape
    rows = tm // RES
    tiles_per_seq = per // rows
    const = lambda i: (0, 0)
    nat = pl.BlockSpec((tm, d), lambda i: (i, 0))

    def lay(width, col_block=0):
        return pl.BlockSpec((None, RES, rows, width),
                            lambda i: (i // tiles_per_seq, 0, i % tiles_per_seq, col_block))

    def resident(shape):
        return pl.BlockSpec(shape, const, pipeline_mode=pl.Buffered(1))

    lse_w = HEADS * LSE_LANES
    return pl.pallas_call(
        _merge_kernel,
        grid=(t // tm,),
        in_specs=[lay(GROUP_WIDTH)] * 3 + [lay(lse_w)] * 3 + [lay(GROUP_WIDTH)]
        + [lay(GROUP_WIDTH, gate_col_block + k) for k in range(4)]
        + [nat, resident(wpa.shape), resident(wpr.shape), resident(wo.shape),
           resident(gpost.shape), resident(gffn.shape)],
        out_specs=[nat, nat],
        out_shape=[jax.ShapeDtypeStruct((t, d), F32), jax.ShapeDtypeStruct((t, d), BF16)],
        scratch_shapes=[pltpu.VMEM((tm, GROUP_WIDTH), BF16)],
        compiler_params=_params(("parallel",)),
        name="merge_out_proj",
    )(*outs, *lses, r_out, proj, proj, proj, proj, x2, wpa, wpr, wo, gpost, gffn)


def _ffn_hidden_kernel(a_ref, wg_ref, wu_ref, side_ref, o_ref, cast_ref, wg_b, wu_b):
    @pl.when(pl.program_id(1) == 0)
    def _():
        wg_b[...] = wg_ref[...].astype(BF16)
        wu_b[...] = wu_ref[...].astype(BF16)

    cast_ref[...] = side_ref[...].astype(BF16)
    a = a_ref[...]
    gt = jnp.dot(a, wg_b[...], preferred_element_type=F32)
    up = jnp.dot(a, wu_b[...], preferred_element_type=F32)
    o_ref[...] = (_silu(gt) * up).astype(o_ref.dtype)


def _ffn_out_kernel(hid_ref, wd_ref, h_ref, gain_ref, o_ref):
    ff = jnp.dot(hid_ref[...], wd_ref[...], preferred_element_type=F32)
    o_ref[...] = h_ref[...] + ff * _rms_scale(ff) * gain_ref[...]


def _ffn(a2, w_gate_up, w_down, h1, gain, *, tm_hidden=1024, tf=512, tm_out=512):
    t, d = a2.shape
    d_ff = w_down.shape[0]
    nf = d_ff // tf
    n_row = t // tm_hidden
    side = _side_cast_specs([w_down], nf * n_row, lambda j, i: j * n_row + i)

    def weight_block(first, lead):
        return pl.BlockSpec((d, tf), lambda j, i: (
            0, first + jnp.minimum(j + (i >= n_row - lead).astype(jnp.int32), nf - 1)))

    hidden, w_down = pl.pallas_call(
        _ffn_hidden_kernel,
        grid=(nf, n_row),
        in_specs=[
            pl.BlockSpec((tm_hidden, d), lambda j, i: (i, 0)),
            weight_block(0, 2),
            weight_block(nf, 1),
        ] + side,
        out_specs=[pl.BlockSpec((tm_hidden, tf), lambda j, i: (i, j))] + side,
        out_shape=[jax.ShapeDtypeStruct((t, d_ff), BF16), jax.ShapeDtypeStruct(w_down.shape, BF16)],
        scratch_shapes=[pltpu.VMEM((d, tf), BF16), pltpu.VMEM((d, tf), BF16)],
        compiler_params=_params(("arbitrary", "arbitrary")),
        name="swiglu_hidden",
    )(a2, w_gate_up, w_gate_up, w_down)
    rows = pl.BlockSpec((tm_out, d), lambda i: (i, 0))
    return pl.pallas_call(
        _ffn_out_kernel,
        grid=(t // tm_out,),
        in_specs=[
            pl.BlockSpec((tm_out, d_ff), lambda i: (i, 0)),
            pl.BlockSpec((d_ff, d), lambda i: (0, 0), pipeline_mode=pl.Buffered(1)),
            rows,
            pl.BlockSpec((1, d), lambda i: (0, 0)),
        ],
        out_specs=rows,
        out_shape=jax.ShapeDtypeStruct((t, d), F32),
        compiler_params=_params(("parallel",)),
        name="swiglu_out",
    )(hidden, w_down, h1, gain)


def _rotary_table(seq):
    inv_freq = ROPE_THETA ** (-np.arange(0, HEAD_DIM, 2, dtype=np.float64) / HEAD_DIM)
    ang = np.arange(seq, dtype=np.float64)[:, None] * inv_freq[None, :]
    table = np.concatenate([np.cos(ang), np.cos(ang), -np.sin(ang), np.sin(ang)], axis=-1)
    return jnp.asarray(table.reshape(seq // RES, RES, 2 * HEAD_DIM).transpose(1, 0, 2), F32)


def kernel(x, w_in, w_attn_branch, w_hgrn_branch, w_mix_out, hgrn_lower_bounds, hgrn_norm_gain,
           norm_mix_pre, norm_mix_post, w_ffn_gate_up, w_ffn_down, norm_ffn_pre, norm_ffn_post):
    batch, seq, d = x.shape
    assert w_in.shape[0] == 1, "single-layer block"
    x2 = x.reshape(batch * seq, d)
    row = lambda v: v[0].reshape(1, -1).astype(F32)

    for window, dilation in ATTN_GROUPS:
        assert window // dilation == ATTN_BACK and seq % window == 0 and RES % dilation == 0
    assert w_in.shape[2] == (QKV_BLOCKS + 8) * GROUP_WIDTH
    rotary = _rotary_table(seq)
    f32 = lambda wt: wt[0].astype(F32)
    bf16 = lambda wt: wt[0].astype(BF16)
    proj, keep = _in_proj(x2, row(norm_mix_pre), f32(w_in), rotary, hgrn_lower_bounds.astype(F32),
                          batch, seq)

    outs, lses, r_out = _mixers(proj, keep, row(hgrn_norm_gain), hgrn_col_block=QKV_BLOCKS)

    h1, a2 = _merge(outs, lses, r_out, proj, x2, bf16(w_attn_branch), bf16(w_hgrn_branch), bf16(w_mix_out),
                    row(norm_mix_post), row(norm_ffn_pre), gate_col_block=QKV_BLOCKS + 4)

    out = _ffn(a2, f32(w_ffn_gate_up), f32(w_ffn_down), h1, row(norm_ffn_post))
    return out.reshape(batch, seq, d)
```

```python
import math

import numpy as np
import jax
import jax.numpy as jnp
from jax import lax
from jax.experimental import pallas as pl
from jax.experimental.pallas import tpu as pltpu

F32 = jnp.float32
BF16 = jnp.bfloat16

HEAD_DIM = 128
ATTN_GROUPS = ((128, 1), (512, 4), (2048, 16))
N_GROUPS = len(ATTN_GROUPS)
HEADS = 8
GROUP_WIDTH = HEADS * HEAD_DIM
ROTARY_BLOCKS = 2 * N_GROUPS
QKV_BLOCKS = 3 * N_GROUPS
FORGET_BLOCK = QKV_BLOCKS + 1
ATTN_BACK = 128
ATTN_SUB_BLOCKS = 2
RES = 16
ROPE_THETA = 10000.0
NORM_EPS = 1e-6
HGRN_CHUNK = 128
HGRN_LEVELS = (64, 32, 16, 8, 4, 2, 1)
LSE_LANES = 16
BF16_SUBLANES = 16
NEG_BIG = -1e30
PERMUTE_GROUP = 256
LOG2E = math.log2(math.e)

VMEM_LIMIT_BYTES = 56 * 1024 * 1024


def _params(semantics):
    return pltpu.CompilerParams(dimension_semantics=semantics, vmem_limit_bytes=VMEM_LIMIT_BYTES)


def _rms_scale(v):
    return lax.rsqrt(jnp.mean(v * v, axis=-1, keepdims=True) + NORM_EPS)


def _sigmoid(v):
    return 0.5 * jnp.tanh(0.5 * v) + 0.5


def _silu(v):
    return v * _sigmoid(v)


def _side_cast_specs(weights, steps, step_index):
    specs = []
    for w in weights:
        rows, cols = w.shape
        block_rows = BF16_SUBLANES * pl.cdiv(rows, BF16_SUBLANES * steps)
        assert rows % block_rows == 0
        last = rows // block_rows - 1
        specs.append(pl.BlockSpec((block_rows, cols),
                                  lambda *idx, last=last: (jnp.minimum(step_index(*idx), last), 0)))
    return specs


def _block_position(i, rows_per_residue, block_rows):
    shift = rows_per_residue.bit_length() - 1
    n_res_shift = (block_rows // rows_per_residue).bit_length() - 1
    return ((i & (rows_per_residue - 1)) << n_res_shift) | lax.shift_right_logical(i, shift)


def _in_proj_kernel(x_lo_ref, x_hi_ref, gain_ref, w_ref, rot_ref, hlb_ref, o_ref, keep_ref, a_ref):
    half_rows = x_lo_ref.shape[0]
    tm = 2 * half_rows
    per_res = tm // RES
    grp = PERMUTE_GROUP
    per_grp = grp // RES
    j = pl.program_id(1)

    @pl.when(j == 0)
    def _():
        row_token = _block_position(lax.broadcasted_iota(jnp.int32, (grp, grp), 0), per_grp, grp)
        permute = jnp.where(row_token == lax.broadcasted_iota(jnp.int32, (grp, grp), 1), 1.0, 0.0).astype(BF16)
        gain = gain_ref[...]
        for gi in range(tm // grp):
            x_ref, first = (x_lo_ref, gi * grp) if gi * grp < half_rows else (x_hi_ref, gi * grp - half_rows)
            xg = x_ref[first:first + grp, :]
            ag = (xg * _rms_scale(xg) * gain).astype(BF16)
            pg = jnp.dot(permute, ag, preferred_element_type=F32).astype(BF16)
            for r in range(RES):
                dst = r * per_res + gi * per_grp
                a_ref[dst:dst + per_grp, :] = pg[r * per_grp:(r + 1) * per_grp, :]

    def project():
        return jnp.dot(a_ref[...], w_ref[...].astype(BF16), preferred_element_type=F32)

    def store(ref, val):
        ref[...] = val.reshape(RES, per_res, val.shape[-1]).astype(ref.dtype)

    @pl.when(j < ROTARY_BLOCKS)
    def _():
        res = project()
        scale = jnp.where(j < N_GROUPS, HEAD_DIM ** -0.5 * LOG2E, 1.0)
        table = rot_ref[...].reshape(tm, 2 * HEAD_DIM) * scale
        cos = table[:, :HEAD_DIM]
        sin = table[:, HEAD_DIM:]
        heads = []
        for h in range(HEADS):
            t = res[:, h * HEAD_DIM:(h + 1) * HEAD_DIM]
            heads.append(t * cos + pltpu.roll(t, HEAD_DIM // 2, 1) * sin)
        store(o_ref, jnp.concatenate(heads, axis=1))

    is_swish = (j == FORGET_BLOCK - 1) | (j == FORGET_BLOCK + 2)

    @pl.when(is_swish)
    def _():
        store(o_ref, _silu(project()))

    @pl.when((j >= ROTARY_BLOCKS) & (j != FORGET_BLOCK) & jnp.logical_not(is_swish))
    def _():
        store(o_ref, project())

    @pl.when(j == FORGET_BLOCK)
    def _():
        res = project()
        h0 = hlb_ref[0:1, :]
        h1 = hlb_ref[1:2, :]
        hm = jnp.maximum(h0, h1)
        e0 = jnp.exp(h0 - hm)
        e1 = jnp.exp(h1 - hm)
        lb = e0 / (e0 + e1)
        f = lb + (1.0 - lb) * jax.nn.sigmoid(res)
        store(o_ref, jnp.log(f))
        store(keep_ref, 1.0 - f)


def _in_proj(x2, gain, w, rotary, lower_bounds, batch, seq, *, tm=1024, tn=GROUP_WIDTH):
    assert lower_bounds.shape[0] == 2, "forget-gate bound rows: DEPTH + 1 with DEPTH == 1"
    t, d = x2.shape
    n = w.shape[1]
    tiles_per_seq = seq // tm
    table = pl.BlockSpec((RES, tm // RES, 2 * HEAD_DIM), lambda i, j: (0, i % tiles_per_seq, 0))
    lay = lambda col: pl.BlockSpec((None, RES, tm // RES, tn),
                                   lambda i, j: (i // tiles_per_seq, 0, i % tiles_per_seq, col(j)))
    n_row, n_col = t // tm, n // tn

    def x_half(which, lead):
        def index(i, j):
            tile = jnp.minimum(i + (j >= n_col - lead).astype(jnp.int32), n_row - 1)
            return (2 * tile + which, 0)
        return pl.BlockSpec((tm // 2, d), index)

    return pl.pallas_call(
        _in_proj_kernel,
        grid=(n_row, n_col),
        in_specs=[
            x_half(0, 2), x_half(1, 1),
            pl.BlockSpec((1, d), lambda i, j: (0, 0)),
            pl.BlockSpec((d, tn), lambda i, j: (0, j)),
            table,
            pl.BlockSpec(lower_bounds.shape, lambda i, j: (0, 0)),
        ],
        out_specs=[lay(lambda j: j), lay(lambda j: 0)],
        out_shape=[jax.ShapeDtypeStruct((batch, RES, seq // RES, n), BF16),
                   jax.ShapeDtypeStruct((batch, RES, seq // RES, tn), BF16)],
        scratch_shapes=[pltpu.VMEM((tm, d), BF16)],
        compiler_params=_params(("arbitrary", "arbitrary")),
        name="in_proj",
    )(x2, x2, gain, w, rotary, lower_bounds)


def _sub_blocks(ref, sl, nsub):
    v = ref[:, :, sl]
    a, u, w = v.shape
    step = u // nsub
    if step % BF16_SUBLANES:
        v = v.astype(F32)
    return [v[:, s * step:(s + 1) * step, :].reshape(a * step, w).astype(ref.dtype) for s in range(nsub)]


def _store_sub_blocks(ref, sl, blocks):
    a = ref.shape[0]
    parts = [b.reshape(a, b.shape[0] // a, b.shape[1]) for b in blocks]
    ref[:, :, sl] = jnp.concatenate(parts, axis=1).astype(ref.dtype)


def _attn_carry(n, kk_ref, vv_ref):
    blk = ATTN_BACK
    nsub = ATTN_SUB_BLOCKS

    @pl.when(n == 0)
    def _():
        kk_ref[0:blk, :] = jnp.zeros((blk, GROUP_WIDTH), BF16)
        vv_ref[0:blk, :] = jnp.zeros((blk, GROUP_WIDTH), BF16)

    @pl.when(n > 0)
    def _():
        kk_ref[0:blk, :] = kk_ref[nsub * blk:(nsub + 1) * blk, :]
        vv_ref[0:blk, :] = vv_ref[nsub * blk:(nsub + 1) * blk, :]


def _attn_body(n, q_ref, k_ref, v_ref, o_ref, lse_ref, kk_ref, vv_ref, *, rows_per_residue):
    blk = ATTN_BACK
    nsub = ATTN_SUB_BLOCKS

    qi = lax.broadcasted_iota(jnp.int32, (blk, 2 * blk), 0)
    kj = lax.broadcasted_iota(jnp.int32, (blk, 2 * blk), 1)
    q_pos = _block_position(qi, rows_per_residue, blk)
    k_pos = _block_position(kj & (blk - 1), rows_per_residue, blk) - jnp.where(kj < blk, blk, 0)
    dist = q_pos - k_pos
    band = (dist >= 0) & (dist <= ATTN_BACK)
    band_bias = jnp.where(band, 0.0, NEG_BIG)
    first_bias = jnp.where(band & (kj >= jnp.where(n > 0, 0, blk)), 0.0, NEG_BIG)
    layout_res, rows_blk, _ = lse_ref.shape
    step = rows_blk // nsub

    for h in range(HEADS):
        sl = slice(h * HEAD_DIM, (h + 1) * HEAD_DIM)
        for s, (kb, vb) in enumerate(zip(_sub_blocks(k_ref, sl, nsub), _sub_blocks(v_ref, sl, nsub))):
            rows = slice((s + 1) * blk, (s + 2) * blk)
            kk_ref[rows, sl] = kb
            vv_ref[rows, sl] = vb
        outs = []
        for s, qb in enumerate(_sub_blocks(q_ref, sl, nsub)):
            keys = slice(s * blk, (s + 2) * blk)
            sc = lax.dot_general(qb, kk_ref[keys, sl], (((1,), (1,)), ((), ())), preferred_element_type=F32)
            sc = sc + (first_bias if s == 0 else band_bias)
            m = jnp.max(sc, axis=-1, keepdims=True)
            p = jnp.exp2(sc - m)
            den = jnp.sum(p, axis=-1, keepdims=True)
            o = jnp.dot(p.astype(BF16), vv_ref[keys, sl], preferred_element_type=F32)
            outs.append(o / den)
            lse = (m * math.log(2.0) + jnp.log(den)).reshape(layout_res, step, 1)
            lse_ref[:, s * step:(s + 1) * step, h * LSE_LANES:(h + 1) * LSE_LANES] = jnp.broadcast_to(
                lse, (layout_res, step, LSE_LANES))
        _store_sub_blocks(o_ref, sl, outs)


def _attn_operands(proj, g, dilation):
    batch, _, per, _ = proj.shape
    a = RES // dilation
    rows = ATTN_BACK // a
    rows_blk = rows * ATTN_SUB_BLOCKS
    nblk = per // rows_blk
    view = lambda arr: arr.reshape(arr.shape[:-3] + (a, dilation) + arr.shape[-2:])

    def spec(width, col_block):
        return pl.BlockSpec((None, a, None, rows_blk, width),
                            lambda i: (i // (dilation * nblk), 0, (i // nblk) % dilation, i % nblk, col_block))

    lse_w = HEADS * LSE_LANES
    proj_v = view(proj)
    key_rows = (ATTN_SUB_BLOCKS + 1) * ATTN_BACK
    return dict(
        inputs=[proj_v, proj_v, proj_v],
        in_specs=[spec(GROUP_WIDTH, which * N_GROUPS + g) for which in range(3)],
        out_specs=[spec(GROUP_WIDTH, 0), spec(lse_w, 0)],
        out_shape=[jax.ShapeDtypeStruct((batch, a, dilation, per, GROUP_WIDTH), BF16),
                   jax.ShapeDtypeStruct((batch, a, dilation, per, lse_w), F32)],
        scratch=[pltpu.VMEM((key_rows, GROUP_WIDTH), BF16), pltpu.VMEM((key_rows, GROUP_WIDTH), BF16)],
        rows_per_residue=rows, nblk=nblk)


def _hgrn_cumsum_matrix():
    c = HGRN_CHUNK
    i = np.arange(c)
    per = c // RES
    pos = (i % per) * RES + i // per
    return (pos[None, :] <= pos[:, None]).astype(np.float32)


def _level_reference(b3, half):
    res, per, dk = b3.shape
    if half < res:
        two = 2 * half
        parts = [jnp.broadcast_to(b3[blk * two + half - 1][None], (two, per, dk)) for blk in range(res // two)]
        return parts[0] if len(parts) == 1 else jnp.concatenate(parts, axis=0)
    last = b3[res - 1]
    step = half // res
    u = lax.broadcasted_iota(jnp.int32, (per, dk), 0)
    ref = None
    for blk in reversed(range(per // (2 * step))):
        row = jnp.broadcast_to(last[blk * 2 * step + step - 1][None], (per, dk))
        ref = row if ref is None else jnp.where(u < (blk + 1) * 2 * step, row, ref)
    return jnp.broadcast_to(ref[None], (res, per, dk))


def _level_decay_exponent(b3, half):
    ref = _level_reference(b3, half)
    res, per, dk = b3.shape
    if half < res:
        parts = []
        for lo in range(0, res, 2 * half):
            up, dn = slice(lo, lo + half), slice(lo + half, lo + 2 * half)
            parts += [ref[up] - b3[up], b3[dn] - ref[dn]]
        return jnp.concatenate(parts, axis=0)
    step = half // res
    u = lax.broadcasted_iota(jnp.int32, (per, dk), 0)
    return (b3 - ref) * jnp.where((u & step) != 0, 1.0, -1.0)


def _hgrn_body(q_ref, logf_ref, v_ref, gate_ref, keep_ref, gain_ref, tri_ref, o_ref, st_ref):
    c = HGRN_CHUNK
    dk = HEAD_DIM
    per = c // RES

    g_all = logf_ref[...].astype(F32)
    k_all = keep_ref[...].astype(F32)
    q_all = q_ref[...].astype(F32)
    v_all = v_ref[...].astype(F32)
    gate_all = gate_ref[...].astype(F32)

    ti = _block_position(lax.broadcasted_iota(jnp.int32, (c, c), 0), per, c)
    si = _block_position(lax.broadcasted_iota(jnp.int32, (c, c), 1), per, c)
    top_bit = 31 - lax.clz(ti ^ si)
    n_levels = len(HGRN_LEVELS)
    level_id = jnp.where(ti > si, top_bit, jnp.where(ti == si, n_levels, n_levels + 1))
    level_b = level_id.astype(F32).astype(BF16)
    tri = tri_ref[...]

    n_chunks = o_ref.shape[1] // per
    outs = [[None] * n_chunks for _ in range(HEADS)]
    for ci in range(n_chunks):
        rows = slice(ci * per, (ci + 1) * per)
        g = g_all[:, rows, :].reshape(c, GROUP_WIDTH)
        g_hi = g.astype(BF16)
        r1 = g - g_hi.astype(F32)
        g_mid = r1.astype(BF16)
        g_lo = (r1 - g_mid.astype(F32)).astype(BF16)
        cum = (jnp.dot(tri, g_hi, preferred_element_type=F32)
               + jnp.dot(tri, g_mid, preferred_element_type=F32)
               + jnp.dot(tri, g_lo, preferred_element_type=F32))

        for h in range(HEADS):
            sl = slice(h * dk, (h + 1) * dk)
            q3 = q_all[:, rows, sl]
            k3 = k_all[:, rows, sl]
            q = q3.reshape(c, dk)
            k = k3.reshape(c, dk)
            v = v_all[:, rows, sl].reshape(c, dk).astype(BF16)
            b = cum[:, sl] * LOG2E
            b3 = b.reshape(RES, per, dk)
            b_last = b[c - 1:c, :]
            st = st_ref[h]

            q_dec = (q * jnp.exp2(b)).astype(BF16)
            o = lax.dot_general(q_dec, st.astype(BF16), (((1,), (1,)), ((), ())),
                                preferred_element_type=F32)

            diag = jnp.broadcast_to(jnp.sum(q * k, axis=-1, keepdims=True), (c, c)).astype(BF16)
            a = jnp.where(level_b == n_levels, diag, jnp.zeros((c, c), BF16))
            q_b = q.astype(BF16)
            k_b = k.astype(BF16)
            for half in HGRN_LEVELS:
                e = jnp.exp2(_level_decay_exponent(b3, half)).reshape(c, dk).astype(BF16)
                p_l = lax.dot_general(q_b * e, k_b * e, (((1,), (1,)), ((), ())), preferred_element_type=F32)
                a = jnp.where(level_b == half.bit_length() - 1, p_l.astype(BF16), a)
            o = o + jnp.dot(a, v, preferred_element_type=F32)

            k_dec = (k * jnp.exp2(b_last - b)).astype(BF16)
            st_ref[h] = st * jnp.exp2(b_last) + lax.dot_general(
                v, k_dec, (((0,), (0,)), ((), ())), preferred_element_type=F32)

            r = o * _rms_scale(o) * gain_ref[:, sl] * gate_all[:, rows, sl].reshape(c, dk)
            outs[h][ci] = r.reshape(RES, per, dk)

    for h in range(HEADS):
        o_ref[:, :, h * dk:(h + 1) * dk] = jnp.concatenate(outs[h], axis=1).astype(o_ref.dtype)


def _hgrn_operands(proj, keep, gain, *, first_col_block, rows):
    batch, _, per, _ = proj.shape
    tiles = per // rows
    tri = jnp.asarray(_hgrn_cumsum_matrix(), BF16)

    def col_spec(col_block):
        return pl.BlockSpec((None, RES, rows, GROUP_WIDTH), lambda i: (i // tiles, 0, i % tiles, col_block))

    const = lambda i: (0, 0)
    return dict(
        inputs=[proj, proj, proj, proj, keep, gain, tri],
        in_specs=[col_spec(first_col_block + k) for k in range(4)]
        + [col_spec(0), pl.BlockSpec(gain.shape, const), pl.BlockSpec(tri.shape, const)],
        out_specs=[col_spec(0)],
        out_shape=[jax.ShapeDtypeStruct((batch, RES, per, GROUP_WIDTH), BF16)],
        scratch=[pltpu.VMEM((HEADS, HEAD_DIM, HEAD_DIM), F32)],
        tiles=tiles)


def _mixers(proj, keep, gain, *, hgrn_col_block):
    batch, _, per, _ = proj.shape
    rows = ATTN_SUB_BLOCKS * ATTN_BACK // RES
    steps = batch * per // rows
    parts = [_attn_operands(proj, g, dilation) for g, (_, dilation) in enumerate(ATTN_GROUPS)]
    parts.append(_hgrn_operands(proj, keep, gain, first_col_block=hgrn_col_block, rows=rows))
    n_in = [len(p["inputs"]) for p in parts]
    n_out = [len(p["out_shape"]) for p in parts]
    n_scr = [len(p["scratch"]) for p in parts]

    def kernel_body(*refs):
        ins, rest = refs[:sum(n_in)], refs[sum(n_in):]
        outs, scr = rest[:sum(n_out)], rest[sum(n_out):]
        i = pl.program_id(0)
        take = lambda seq, counts, k: seq[sum(counts[:k]):sum(counts[:k + 1])]
        for k, part in enumerate(parts[:-1]):
            _attn_carry(i % part["nblk"], *take(scr, n_scr, k))
        k = len(parts) - 1

        @pl.when(i % parts[k]["tiles"] == 0)
        def _():
            st_ref = take(scr, n_scr, k)[0]
            st_ref[...] = jnp.zeros(st_ref.shape, F32)

        for k, part in enumerate(parts[:-1]):
            _attn_body(i % part["nblk"], *take(ins, n_in, k), *take(outs, n_out, k), *take(scr, n_scr, k),
                       rows_per_residue=part["rows_per_residue"])
        k = len(parts) - 1
        _hgrn_body(*take(ins, n_in, k), *take(outs, n_out, k), *take(scr, n_scr, k))

    flat = lambda key: [v for p in parts for v in p[key]]
    res = pl.pallas_call(
        kernel_body,
        grid=(steps,),
        in_specs=flat("in_specs"),
        out_specs=flat("out_specs"),
        out_shape=flat("out_shape"),
        scratch_shapes=flat("scratch"),
        compiler_params=_params(("arbitrary",)),
        name="token_mixers",
    )(*flat("inputs"))
    lse_w = HEADS * LSE_LANES
    outs = [res[2 * g].reshape(batch, RES, per, GROUP_WIDTH) for g in range(N_GROUPS)]
    lses = [res[2 * g + 1].reshape(batch, RES, per, lse_w) for g in range(N_GROUPS)]
    return outs, lses, res[2 * N_GROUPS]


def _merge_kernel(o1_ref, o2_ref, o3_ref, l1_ref, l2_ref, l3_ref, r_ref,
                  ga0_ref, ga1_ref, gr0_ref, gr1_ref, x_ref,
                  wpa_ref, wpr_ref, wo_ref, gpost_ref, gffn_ref,
                  h_ref, a2_ref, attn_s, y_s):
    i = pl.program_id(0)
    tm = x_ref.shape[0]
    per = tm // RES
    flat = lambda ref: ref[...].reshape(tm, ref.shape[-1])

    @pl.when(i == 0)
    def _():
        y_s[...] = jnp.zeros(y_s.shape, BF16)

    z = jnp.dot(y_s[(i + 1) % 2], wo_ref[...], preferred_element_type=F32)
    h1 = x_ref[...] + z * _rms_scale(z) * gpost_ref[...]
    h_ref[...] = h1
    a2_ref[...] = (h1 * _rms_scale(h1) * gffn_ref[...]).astype(BF16)

    la, lb, lc = flat(l1_ref), flat(l2_ref), flat(l3_ref)
    lm = jnp.maximum(la, jnp.maximum(lb, lc))
    ea, eb, ec = jnp.exp(la - lm), jnp.exp(lb - lm), jnp.exp(lc - lm)
    den = ea + eb + ec
    wa, wb, wc = ea / den, eb / den, ec / den
    o1, o2, o3 = flat(o1_ref), flat(o2_ref), flat(o3_ref)
    for h in range(HEADS):
        sl = slice(h * HEAD_DIM, (h + 1) * HEAD_DIM)
        c0 = h * LSE_LANES
        mixed = (wa[:, c0:c0 + 1] * o1[:, sl].astype(F32)
                 + wb[:, c0:c0 + 1] * o2[:, sl].astype(F32)
                 + wc[:, c0:c0 + 1] * o3[:, sl].astype(F32))
        attn_s[:, sl] = mixed.astype(BF16)

    tok = lax.broadcasted_iota(jnp.int32, (tm, tm), 0)
    src = _block_position(lax.broadcasted_iota(jnp.int32, (tm, tm), 1), per, tm)
    unpermute = jnp.where(src == tok, 1.0, 0.0).astype(BF16)

    attn = attn_s[...]
    r = flat(r_ref)
    half_w = GROUP_WIDTH
    y_out = y_s.at[i % 2]
    for half, (ga_ref, gr_ref) in enumerate(((ga0_ref, gr0_ref), (ga1_ref, gr1_ref))):
        cs = slice(half * half_w, (half + 1) * half_w)
        ya = jnp.dot(attn, wpa_ref[:, cs], preferred_element_type=F32)
        yr = jnp.dot(r, wpr_ref[:, cs], preferred_element_type=F32)
        y = _sigmoid(flat(ga_ref).astype(F32)) * ya + _sigmoid(flat(gr_ref).astype(F32)) * yr
        y_out[:, cs] = jnp.dot(unpermute, y.astype(BF16), preferred_element_type=F32).astype(BF16)


def _merge(outs, lses, r_out, proj, x2, wpa, wpr, wo, gpost, gffn, *, gate_col_block, tm=256):
    t, d = x2.shape
    batch, _, per, _ = proj.shape
    rows = tm // RES
    tiles_per_seq = per // rows
    n_tiles = t // tm
    const = lambda i: (0, 0)
    nat = pl.BlockSpec((tm, d), lambda i: (jnp.maximum(i - 1, 0), 0))

    def lay(width, col_block=0):
        def index(i):
            tile = jnp.minimum(i, n_tiles - 1)
            return (tile // tiles_per_seq, 0, tile % tiles_per_seq, col_block)
        return pl.BlockSpec((None, RES, rows, width), index)

    def resident(shape):
        return pl.BlockSpec(shape, const, pipeline_mode=pl.Buffered(1))

    lse_w = HEADS * LSE_LANES
    return pl.pallas_call(
        _merge_kernel,
        grid=(n_tiles + 1,),
        in_specs=[lay(GROUP_WIDTH)] * 3 + [lay(lse_w)] * 3 + [lay(GROUP_WIDTH)]
        + [lay(GROUP_WIDTH, gate_col_block + k) for k in range(4)]
        + [nat, resident(wpa.shape), resident(wpr.shape), resident(wo.shape),
           resident(gpost.shape), resident(gffn.shape)],
        out_specs=[nat, nat],
        out_shape=[jax.ShapeDtypeStruct((t, d), F32), jax.ShapeDtypeStruct((t, d), BF16)],
        scratch_shapes=[pltpu.VMEM((tm, GROUP_WIDTH), BF16), pltpu.VMEM((2, tm, d), BF16)],
        compiler_params=_params(("arbitrary",)),
        name="merge_out_proj",
    )(*outs, *lses, r_out, proj, proj, proj, proj, x2, wpa, wpr, wo, gpost, gffn)


def _ffn_hidden_kernel(a_ref, wg_ref, wu_ref, side_ref, o_ref, cast_ref, wg_b, wu_b):
    @pl.when(pl.program_id(1) == 0)
    def _():
        wg_b[...] = wg_ref[...].astype(BF16)
        wu_b[...] = wu_ref[...].astype(BF16)

    cast_ref[...] = side_ref[...].astype(BF16)
    a = a_ref[...]
    gt = jnp.dot(a, wg_b[...], preferred_element_type=F32)
    up = jnp.dot(a, wu_b[...], preferred_element_type=F32)
    o_ref[...] = (_silu(gt) * up).astype(o_ref.dtype)


def _ffn_out_kernel(hid_ref, wd_ref, h_ref, gain_ref, o_ref):
    ff = jnp.dot(hid_ref[...], wd_ref[...], preferred_element_type=F32)
    o_ref[...] = h_ref[...] + ff * _rms_scale(ff) * gain_ref[...]


def _ffn(a2, w_gate_up, w_down, h1, gain, *, tm_hidden=1024, tf=512, tm_out=512):
    t, d = a2.shape
    d_ff = w_down.shape[0]
    nf = d_ff // tf
    n_row = t // tm_hidden
    side = _side_cast_specs([w_down], nf * n_row, lambda j, i: j * n_row + i)

    def weight_block(first, lead):
        return pl.BlockSpec((d, tf), lambda j, i: (
            0, first + jnp.minimum(j + (i >= n_row - lead).astype(jnp.int32), nf - 1)))

    hidden, w_down = pl.pallas_call(
        _ffn_hidden_kernel,
        grid=(nf, n_row),
        in_specs=[
            pl.BlockSpec((tm_hidden, d), lambda j, i: (i, 0)),
            weight_block(0, 2),
            weight_block(nf, 1),
        ] + side,
        out_specs=[pl.BlockSpec((tm_hidden, tf), lambda j, i: (i, j))] + side,
        out_shape=[jax.ShapeDtypeStruct((t, d_ff), BF16), jax.ShapeDtypeStruct(w_down.shape, BF16)],
        scratch_shapes=[pltpu.VMEM((d, tf), BF16), pltpu.VMEM((d, tf), BF16)],
        compiler_params=_params(("arbitrary", "arbitrary")),
        name="swiglu_hidden",
    )(a2, w_gate_up, w_gate_up, w_down)
    rows = pl.BlockSpec((tm_out, d), lambda i: (i, 0))
    return pl.pallas_call(
        _ffn_out_kernel,
        grid=(t // tm_out,),
        in_specs=[
            pl.BlockSpec((tm_out, d_ff), lambda i: (i, 0)),
            pl.BlockSpec((d_ff, d), lambda i: (0, 0), pipeline_mode=pl.Buffered(1)),
            rows,
            pl.BlockSpec((1, d), lambda i: (0, 0)),
        ],
        out_specs=rows,
        out_shape=jax.ShapeDtypeStruct((t, d), F32),
        compiler_params=_params(("parallel",)),
        name="swiglu_out",
    )(hidden, w_down, h1, gain)


def _rotary_table(seq):
    inv_freq = ROPE_THETA ** (-np.arange(0, HEAD_DIM, 2, dtype=np.float64) / HEAD_DIM)
    ang = np.arange(seq, dtype=np.float64)[:, None] * inv_freq[None, :]
    table = np.concatenate([np.cos(ang), np.cos(ang), -np.sin(ang), np.sin(ang)], axis=-1)
    return jnp.asarray(table.reshape(seq // RES, RES, 2 * HEAD_DIM).transpose(1, 0, 2), F32)


def kernel(x, w_in, w_attn_branch, w_hgrn_branch, w_mix_out, hgrn_lower_bounds, hgrn_norm_gain,
           norm_mix_pre, norm_mix_post, w_ffn_gate_up, w_ffn_down, norm_ffn_pre, norm_ffn_post):
    batch, seq, d = x.shape
    assert w_in.shape[0] == 1, "single-layer block"
    x2 = x.reshape(batch * seq, d)
    row = lambda v: v[0].reshape(1, -1).astype(F32)

    for window, dilation in ATTN_GROUPS:
        assert window // dilation == ATTN_BACK and seq % window == 0 and RES % dilation == 0
    assert w_in.shape[2] == (QKV_BLOCKS + 8) * GROUP_WIDTH
    rotary = _rotary_table(seq)
    f32 = lambda wt: wt[0].astype(F32)
    bf16 = lambda wt: wt[0].astype(BF16)
    proj, keep = _in_proj(x2, row(norm_mix_pre), f32(w_in), rotary, hgrn_lower_bounds.astype(F32),
                          batch, seq)

    outs, lses, r_out = _mixers(proj, keep, row(hgrn_norm_gain), hgrn_col_block=QKV_BLOCKS)

    h1, a2 = _merge(outs, lses, r_out, proj, x2, bf16(w_attn_branch), bf16(w_hgrn_branch), bf16(w_mix_out),
                    row(norm_mix_post), row(norm_ffn_pre), gate_col_block=QKV_BLOCKS + 4)

    out = _ffn(a2, f32(w_ffn_gate_up), f32(w_ffn_down), h1, row(norm_ffn_post))
    return out.reshape(batch, seq, d)
```

```python
import math

import numpy as np
import jax
import jax.numpy as jnp
from jax import lax
from jax.experimental import pallas as pl
from jax.experimental.pallas import tpu as pltpu

F32 = jnp.float32
BF16 = jnp.bfloat16

HEAD_DIM = 128
ATTN_GROUPS = ((128, 1), (512, 4), (2048, 16))
N_GROUPS = len(ATTN_GROUPS)
HEADS = 8
GROUP_WIDTH = HEADS * HEAD_DIM
ROTARY_BLOCKS = 2 * N_GROUPS
QKV_BLOCKS = 3 * N_GROUPS
FORGET_BLOCK = QKV_BLOCKS + 1
ATTN_BACK = 128
ATTN_SUB_BLOCKS = 2
RES = 16
ROPE_THETA = 10000.0
NORM_EPS = 1e-6
HGRN_CHUNK = 128
HGRN_LEVELS = (64, 32, 16, 8, 4, 2, 1)
LSE_LANES = 16
BF16_SUBLANES = 16
NEG_BIG = -1e30
PERMUTE_GROUP = 256
LOG2E = math.log2(math.e)

VMEM_LIMIT_BYTES = 56 * 1024 * 1024


def _params(semantics):
    return pltpu.CompilerParams(dimension_semantics=semantics, vmem_limit_bytes=VMEM_LIMIT_BYTES)


def _rms_scale(v):
    return lax.rsqrt(jnp.mean(v * v, axis=-1, keepdims=True) + NORM_EPS)


def _sigmoid(v):
    return 0.5 * jnp.tanh(0.5 * v) + 0.5


def _silu(v):
    return v * _sigmoid(v)


def _side_cast_specs(weights, steps, step_index):
    specs = []
    for w in weights:
        rows, cols = w.shape
        block_rows = BF16_SUBLANES * pl.cdiv(rows, BF16_SUBLANES * steps)
        assert rows % block_rows == 0
        last = rows // block_rows - 1
        specs.append(pl.BlockSpec((block_rows, cols),
                                  lambda *idx, last=last: (jnp.minimum(step_index(*idx), last), 0)))
    return specs


def _block_position(i, rows_per_residue, block_rows):
    shift = rows_per_residue.bit_length() - 1
    n_res_shift = (block_rows // rows_per_residue).bit_length() - 1
    return ((i & (rows_per_residue - 1)) << n_res_shift) | lax.shift_right_logical(i, shift)


def _in_proj_kernel(x_lo_ref, x_hi_ref, gain_ref, w_ref, rot_ref, hlb_ref, o_ref, keep_ref, a_ref, raw_ref):
    half_rows = x_lo_ref.shape[0]
    tm = 2 * half_rows
    per_res = tm // RES
    grp = PERMUTE_GROUP
    per_grp = grp // RES
    j = pl.program_id(1)
    n_col = pl.num_programs(1) - 1
    e = j - 1

    @pl.when(j == 0)
    def _():
        row_token = _block_position(lax.broadcasted_iota(jnp.int32, (grp, grp), 0), per_grp, grp)
        permute = jnp.where(row_token == lax.broadcasted_iota(jnp.int32, (grp, grp), 1), 1.0, 0.0).astype(BF16)
        gain = gain_ref[...]
        for gi in range(tm // grp):
            x_ref, first = (x_lo_ref, gi * grp) if gi * grp < half_rows else (x_hi_ref, gi * grp - half_rows)
            xg = x_ref[first:first + grp, :]
            ag = (xg * _rms_scale(xg) * gain).astype(BF16)
            pg = jnp.dot(permute, ag, preferred_element_type=F32).astype(BF16)
            for r in range(RES):
                dst = r * per_res + gi * per_grp
                a_ref[dst:dst + per_grp, :] = pg[r * per_grp:(r + 1) * per_grp, :]

    def project():
        raw_ref[...] = jnp.dot(a_ref[...], w_ref[...].astype(BF16), preferred_element_type=F32)

    def store(ref, val):
        ref[...] = val.reshape(RES, per_res, val.shape[-1]).astype(ref.dtype)

    @pl.when(j == 0)
    def _():
        project()

    @pl.when((e >= 0) & (e < ROTARY_BLOCKS))
    def _():
        res = raw_ref[...]
        scale = jnp.where(e < N_GROUPS, HEAD_DIM ** -0.5 * LOG2E, 1.0)
        table = rot_ref[...].reshape(tm, 2 * HEAD_DIM) * scale
        cos = table[:, :HEAD_DIM]
        sin = table[:, HEAD_DIM:]
        heads = []
        for h in range(HEADS):
            t = res[:, h * HEAD_DIM:(h + 1) * HEAD_DIM]
            heads.append(t * cos + pltpu.roll(t, HEAD_DIM // 2, 1) * sin)
        store(o_ref, jnp.concatenate(heads, axis=1))
        project()

    is_swish = (e == FORGET_BLOCK - 1) | (e == FORGET_BLOCK + 2)

    @pl.when(is_swish)
    def _():
        store(o_ref, _silu(raw_ref[...]))
        project()

    is_plain = (e >= ROTARY_BLOCKS) & (e != FORGET_BLOCK) & jnp.logical_not(is_swish)

    @pl.when(is_plain & (j < n_col))
    def _():
        store(o_ref, raw_ref[...])
        project()

    @pl.when(j == n_col)
    def _():
        store(o_ref, raw_ref[...])

    @pl.when(e == FORGET_BLOCK)
    def _():
        res = raw_ref[...]
        h0 = hlb_ref[0:1, :]
        h1 = hlb_ref[1:2, :]
        hm = jnp.maximum(h0, h1)
        e0 = jnp.exp(h0 - hm)
        e1 = jnp.exp(h1 - hm)
        lb = e0 / (e0 + e1)
        f = lb + (1.0 - lb) * jax.nn.sigmoid(res)
        store(o_ref, jnp.log(f))
        store(keep_ref, 1.0 - f)
        project()


def _in_proj(x2, gain, w, rotary, lower_bounds, batch, seq, *, tm=1024, tn=GROUP_WIDTH):
    assert lower_bounds.shape[0] == 2, "forget-gate bound rows: DEPTH + 1 with DEPTH == 1"
    t, d = x2.shape
    n = w.shape[1]
    tiles_per_seq = seq // tm
    n_row, n_col = t // tm, n // tn
    assert (n_col - 1) >= FORGET_BLOCK + 3, "the last column block must be a plain one"
    n_step = n_col + 1
    table = pl.BlockSpec((RES, tm // RES, 2 * HEAD_DIM), lambda i, j: (0, i % tiles_per_seq, 0))
    lay = lambda col: pl.BlockSpec((None, RES, tm // RES, tn),
                                   lambda i, j: (i // tiles_per_seq, 0, i % tiles_per_seq, col(j)))

    def x_half(which, lead):
        def index(i, j):
            tile = jnp.minimum(i + (j >= n_step - lead).astype(jnp.int32), n_row - 1)
            return (2 * tile + which, 0)
        return pl.BlockSpec((tm // 2, d), index)

    return pl.pallas_call(
        _in_proj_kernel,
        grid=(n_row, n_step),
        in_specs=[
            x_half(0, 3), x_half(1, 2),
            pl.BlockSpec((1, d), lambda i, j: (0, 0)),
            pl.BlockSpec((d, tn), lambda i, j: (0, jnp.minimum(j, n_col - 1))),
            table,
            pl.BlockSpec(lower_bounds.shape, lambda i, j: (0, 0)),
        ],
        out_specs=[lay(lambda j: jnp.maximum(j - 1, 0)), lay(lambda j: 0)],
        out_shape=[jax.ShapeDtypeStruct((batch, RES, seq // RES, n), BF16),
                   jax.ShapeDtypeStruct((batch, RES, seq // RES, tn), BF16)],
        scratch_shapes=[pltpu.VMEM((tm, d), BF16), pltpu.VMEM((tm, tn), F32)],
        compiler_params=_params(("arbitrary", "arbitrary")),
        name="in_proj",
    )(x2, x2, gain, w, rotary, lower_bounds)


def _sub_blocks(ref, sl, nsub):
    v = ref[:, :, sl]
    a, u, w = v.shape
    step = u // nsub
    if step % BF16_SUBLANES:
        v = v.astype(F32)
    return [v[:, s * step:(s + 1) * step, :].reshape(a * step, w).astype(ref.dtype) for s in range(nsub)]


def _store_sub_blocks(ref, sl, blocks):
    a = ref.shape[0]
    parts = [b.reshape(a, b.shape[0] // a, b.shape[1]) for b in blocks]
    ref[:, :, sl] = jnp.concatenate(parts, axis=1).astype(ref.dtype)


def _attn_carry(n, kk_ref, vv_ref):
    blk = ATTN_BACK
    nsub = ATTN_SUB_BLOCKS

    @pl.when(n == 0)
    def _():
        kk_ref[0:blk, :] = jnp.zeros((blk, GROUP_WIDTH), BF16)
        vv_ref[0:blk, :] = jnp.zeros((blk, GROUP_WIDTH), BF16)

    @pl.when(n > 0)
    def _():
        kk_ref[0:blk, :] = kk_ref[nsub * blk:(nsub + 1) * blk, :]
        vv_ref[0:blk, :] = vv_ref[nsub * blk:(nsub + 1) * blk, :]


def _attn_body(n, q_ref, k_ref, v_ref, o_ref, lse_ref, kk_ref, vv_ref, *, rows_per_residue):
    blk = ATTN_BACK
    nsub = ATTN_SUB_BLOCKS

    qi = lax.broadcasted_iota(jnp.int32, (blk, 2 * blk), 0)
    kj = lax.broadcasted_iota(jnp.int32, (blk, 2 * blk), 1)
    q_pos = _block_position(qi, rows_per_residue, blk)
    k_pos = _block_position(kj & (blk - 1), rows_per_residue, blk) - jnp.where(kj < blk, blk, 0)
    dist = q_pos - k_pos
    band = (dist >= 0) & (dist <= ATTN_BACK)
    band_bias = jnp.where(band, 0.0, NEG_BIG)
    first_bias = jnp.where(band & (kj >= jnp.where(n > 0, 0, blk)), 0.0, NEG_BIG)
    layout_res, rows_blk, _ = lse_ref.shape
    step = rows_blk // nsub

    for h in range(HEADS):
        sl = slice(h * HEAD_DIM, (h + 1) * HEAD_DIM)
        for s, (kb, vb) in enumerate(zip(_sub_blocks(k_ref, sl, nsub), _sub_blocks(v_ref, sl, nsub))):
            rows = slice((s + 1) * blk, (s + 2) * blk)
            kk_ref[rows, sl] = kb
            vv_ref[rows, sl] = vb
        outs = []
        for s, qb in enumerate(_sub_blocks(q_ref, sl, nsub)):
            keys = slice(s * blk, (s + 2) * blk)
            sc = lax.dot_general(qb, kk_ref[keys, sl], (((1,), (1,)), ((), ())), preferred_element_type=F32)
            sc = sc + (first_bias if s == 0 else band_bias)
            m = jnp.max(sc, axis=-1, keepdims=True)
            p = jnp.exp2(sc - m)
            den = jnp.sum(p, axis=-1, keepdims=True)
            o = jnp.dot(p.astype(BF16), vv_ref[keys, sl], preferred_element_type=F32)
            outs.append(o / den)
            lse = (m * math.log(2.0) + jnp.log(den)).reshape(layout_res, step, 1)
            lse_ref[:, s * step:(s + 1) * step, h * LSE_LANES:(h + 1) * LSE_LANES] = jnp.broadcast_to(
                lse, (layout_res, step, LSE_LANES))
        _store_sub_blocks(o_ref, sl, outs)


def _attn_operands(proj, g, dilation):
    batch, _, per, _ = proj.shape
    a = RES // dilation
    rows = ATTN_BACK // a
    rows_blk = rows * ATTN_SUB_BLOCKS
    nblk = per // rows_blk
    view = lambda arr: arr.reshape(arr.shape[:-3] + (a, dilation) + arr.shape[-2:])

    def spec(width, col_block):
        return pl.BlockSpec((None, a, None, rows_blk, width),
                            lambda i: (i // (dilation * nblk), 0, (i // nblk) % dilation, i % nblk, col_block))

    lse_w = HEADS * LSE_LANES
    proj_v = view(proj)
    key_rows = (ATTN_SUB_BLOCKS + 1) * ATTN_BACK
    return dict(
        inputs=[proj_v, proj_v, proj_v],
        in_specs=[spec(GROUP_WIDTH, which * N_GROUPS + g) for which in range(3)],
        out_specs=[spec(GROUP_WIDTH, 0), spec(lse_w, 0)],
        out_shape=[jax.ShapeDtypeStruct((batch, a, dilation, per, GROUP_WIDTH), BF16),
                   jax.ShapeDtypeStruct((batch, a, dilation, per, lse_w), F32)],
        scratch=[pltpu.VMEM((key_rows, GROUP_WIDTH), BF16), pltpu.VMEM((key_rows, GROUP_WIDTH), BF16)],
        rows_per_residue=rows, nblk=nblk)


def _hgrn_cumsum_matrix():
    c = HGRN_CHUNK
    i = np.arange(c)
    per = c // RES
    pos = (i % per) * RES + i // per
    return (pos[None, :] <= pos[:, None]).astype(np.float32)


def _level_reference(b3, half):
    res, per, dk = b3.shape
    if half < res:
        two = 2 * half
        parts = [jnp.broadcast_to(b3[blk * two + half - 1][None], (two, per, dk)) for blk in range(res // two)]
        return parts[0] if len(parts) == 1 else jnp.concatenate(parts, axis=0)
    last = b3[res - 1]
    step = half // res
    u = lax.broadcasted_iota(jnp.int32, (per, dk), 0)
    ref = None
    for blk in reversed(range(per // (2 * step))):
        row = jnp.broadcast_to(last[blk * 2 * step + step - 1][None], (per, dk))
        ref = row if ref is None else jnp.where(u < (blk + 1) * 2 * step, row, ref)
    return jnp.broadcast_to(ref[None], (res, per, dk))


def _level_decay_exponent(b3, half):
    ref = _level_reference(b3, half)
    res, per, dk = b3.shape
    if half < res:
        parts = []
        for lo in range(0, res, 2 * half):
            up, dn = slice(lo, lo + half), slice(lo + half, lo + 2 * half)
            parts += [ref[up] - b3[up], b3[dn] - ref[dn]]
        return jnp.concatenate(parts, axis=0)
    step = half // res
    u = lax.broadcasted_iota(jnp.int32, (per, dk), 0)
    return (b3 - ref) * jnp.where((u & step) != 0, 1.0, -1.0)


def _hgrn_body(q_ref, logf_ref, v_ref, gate_ref, keep_ref, gain_ref, tri_ref, o_ref, st_ref):
    c = HGRN_CHUNK
    dk = HEAD_DIM
    per = c // RES

    g_all = logf_ref[...].astype(F32)
    k_all = keep_ref[...].astype(F32)
    q_all = q_ref[...].astype(F32)
    v_all = v_ref[...].astype(F32)
    gate_all = gate_ref[...].astype(F32)

    ti = _block_position(lax.broadcasted_iota(jnp.int32, (c, c), 0), per, c)
    si = _block_position(lax.broadcasted_iota(jnp.int32, (c, c), 1), per, c)
    top_bit = 31 - lax.clz(ti ^ si)
    n_levels = len(HGRN_LEVELS)
    level_id = jnp.where(ti > si, top_bit, jnp.where(ti == si, n_levels, n_levels + 1))
    level_b = level_id.astype(F32).astype(BF16)
    tri = tri_ref[...]

    n_chunks = o_ref.shape[1] // per
    outs = [[None] * n_chunks for _ in range(HEADS)]
    for ci in range(n_chunks):
        rows = slice(ci * per, (ci + 1) * per)
        g = g_all[:, rows, :].reshape(c, GROUP_WIDTH)
        g_hi = g.astype(BF16)
        r1 = g - g_hi.astype(F32)
        g_mid = r1.astype(BF16)
        g_lo = (r1 - g_mid.astype(F32)).astype(BF16)
        cum = (jnp.dot(tri, g_hi, preferred_element_type=F32)
               + jnp.dot(tri, g_mid, preferred_element_type=F32)
               + jnp.dot(tri, g_lo, preferred_element_type=F32))

        for h in range(HEADS):
            sl = slice(h * dk, (h + 1) * dk)
            q3 = q_all[:, rows, sl]
            k3 = k_all[:, rows, sl]
            q = q3.reshape(c, dk)
            k = k3.reshape(c, dk)
            v = v_all[:, rows, sl].reshape(c, dk).astype(BF16)
            b = cum[:, sl] * LOG2E
            b3 = b.reshape(RES, per, dk)
            b_last = b[c - 1:c, :]
            st = st_ref[h]

            q_dec = (q * jnp.exp2(b)).astype(BF16)
            o = lax.dot_general(q_dec, st.astype(BF16), (((1,), (1,)), ((), ())),
                                preferred_element_type=F32)

            diag = jnp.broadcast_to(jnp.sum(q * k, axis=-1, keepdims=True), (c, c)).astype(BF16)
            a = jnp.where(level_b == n_levels, diag, jnp.zeros((c, c), BF16))
            q_b = q.astype(BF16)
            k_b = k.astype(BF16)
            for half in HGRN_LEVELS:
                e = jnp.exp2(_level_decay_exponent(b3, half)).reshape(c, dk).astype(BF16)
                p_l = lax.dot_general(q_b * e, k_b * e, (((1,), (1,)), ((), ())), preferred_element_type=F32)
                a = jnp.where(level_b == half.bit_length() - 1, p_l.astype(BF16), a)
            o = o + jnp.dot(a, v, preferred_element_type=F32)

            k_dec = (k * jnp.exp2(b_last - b)).astype(BF16)
            st_ref[h] = st * jnp.exp2(b_last) + lax.dot_general(
                v, k_dec, (((0,), (0,)), ((), ())), preferred_element_type=F32)

            r = o * _rms_scale(o) * gain_ref[:, sl] * gate_all[:, rows, sl].reshape(c, dk)
            outs[h][ci] = r.reshape(RES, per, dk)

    for h in range(HEADS):
        o_ref[:, :, h * dk:(h + 1) * dk] = jnp.concatenate(outs[h], axis=1).astype(o_ref.dtype)


def _hgrn_operands(proj, keep, gain, *, first_col_block, rows):
    batch, _, per, _ = proj.shape
    tiles = per // rows
    tri = jnp.asarray(_hgrn_cumsum_matrix(), BF16)

    def col_spec(col_block):
        return pl.BlockSpec((None, RES, rows, GROUP_WIDTH), lambda i: (i // tiles, 0, i % tiles, col_block))

    const = lambda i: (0, 0)
    return dict(
        inputs=[proj, proj, proj, proj, keep, gain, tri],
        in_specs=[col_spec(first_col_block + k) for k in range(4)]
        + [col_spec(0), pl.BlockSpec(gain.shape, const), pl.BlockSpec(tri.shape, const)],
        out_specs=[col_spec(0)],
        out_shape=[jax.ShapeDtypeStruct((batch, RES, per, GROUP_WIDTH), BF16)],
        scratch=[pltpu.VMEM((HEADS, HEAD_DIM, HEAD_DIM), F32)],
        tiles=tiles)


def _mixers(proj, keep, gain, *, hgrn_col_block):
    batch, _, per, _ = proj.shape
    rows = ATTN_SUB_BLOCKS * ATTN_BACK // RES
    steps = batch * per // rows
    parts = [_attn_operands(proj, g, dilation) for g, (_, dilation) in enumerate(ATTN_GROUPS)]
    parts.append(_hgrn_operands(proj, keep, gain, first_col_block=hgrn_col_block, rows=rows))
    n_in = [len(p["inputs"]) for p in parts]
    n_out = [len(p["out_shape"]) for p in parts]
    n_scr = [len(p["scratch"]) for p in parts]

    def kernel_body(*refs):
        ins, rest = refs[:sum(n_in)], refs[sum(n_in):]
        outs, scr = rest[:sum(n_out)], rest[sum(n_out):]
        i = pl.program_id(0)
        take = lambda seq, counts, k: seq[sum(counts[:k]):sum(counts[:k + 1])]
        for k, part in enumerate(parts[:-1]):
            _attn_carry(i % part["nblk"], *take(scr, n_scr, k))
        k = len(parts) - 1

        @pl.when(i % parts[k]["tiles"] == 0)
        def _():
            st_ref = take(scr, n_scr, k)[0]
            st_ref[...] = jnp.zeros(st_ref.shape, F32)

        for k, part in enumerate(parts[:-1]):
            _attn_body(i % part["nblk"], *take(ins, n_in, k), *take(outs, n_out, k), *take(scr, n_scr, k),
                       rows_per_residue=part["rows_per_residue"])
        k = len(parts) - 1
        _hgrn_body(*take(ins, n_in, k), *take(outs, n_out, k), *take(scr, n_scr, k))

    flat = lambda key: [v for p in parts for v in p[key]]
    res = pl.pallas_call(
        kernel_body,
        grid=(steps,),
        in_specs=flat("in_specs"),
        out_specs=flat("out_specs"),
        out_shape=flat("out_shape"),
        scratch_shapes=flat("scratch"),
        compiler_params=_params(("arbitrary",)),
        name="token_mixers",
    )(*flat("inputs"))
    lse_w = HEADS * LSE_LANES
    outs = [res[2 * g].reshape(batch, RES, per, GROUP_WIDTH) for g in range(N_GROUPS)]
    lses = [res[2 * g + 1].reshape(batch, RES, per, lse_w) for g in range(N_GROUPS)]
    return outs, lses, res[2 * N_GROUPS]


def _merge_kernel(o1_ref, o2_ref, o3_ref, l1_ref, l2_ref, l3_ref, r_ref,
                  ga0_ref, ga1_ref, gr0_ref, gr1_ref, x_ref,
                  wpa_ref, wpr_ref, wo_ref, gpost_ref, gffn_ref,
                  h_ref, a2_ref, attn_s, y_s):
    i = pl.program_id(0)
    tm = x_ref.shape[0]
    per = tm // RES
    flat = lambda ref: ref[...].reshape(tm, ref.shape[-1])

    @pl.when(i == 0)
    def _():
        y_s[...] = jnp.zeros(y_s.shape, BF16)

    z = jnp.dot(y_s[(i + 1) % 2], wo_ref[...], preferred_element_type=F32)
    h1 = x_ref[...] + z * _rms_scale(z) * gpost_ref[...]
    h_ref[...] = h1
    a2_ref[...] = (h1 * _rms_scale(h1) * gffn_ref[...]).astype(BF16)

    la, lb, lc = flat(l1_ref), flat(l2_ref), flat(l3_ref)
    lm = jnp.maximum(la, jnp.maximum(lb, lc))
    ea, eb, ec = jnp.exp(la - lm), jnp.exp(lb - lm), jnp.exp(lc - lm)
    den = ea + eb + ec
    wa, wb, wc = ea / den, eb / den, ec / den
    o1, o2, o3 = flat(o1_ref), flat(o2_ref), flat(o3_ref)
    for h in range(HEADS):
        sl = slice(h * HEAD_DIM, (h + 1) * HEAD_DIM)
        c0 = h * LSE_LANES
        mixed = (wa[:, c0:c0 + 1] * o1[:, sl].astype(F32)
                 + wb[:, c0:c0 + 1] * o2[:, sl].astype(F32)
                 + wc[:, c0:c0 + 1] * o3[:, sl].astype(F32))
        attn_s[:, sl] = mixed.astype(BF16)

    tok = lax.broadcasted_iota(jnp.int32, (tm, tm), 0)
    src = _block_position(lax.broadcasted_iota(jnp.int32, (tm, tm), 1), per, tm)
    unpermute = jnp.where(src == tok, 1.0, 0.0).astype(BF16)

    attn = attn_s[...]
    r = flat(r_ref)
    half_w = GROUP_WIDTH
    y_out = y_s.at[i % 2]
    for half, (ga_ref, gr_ref) in enumerate(((ga0_ref, gr0_ref), (ga1_ref, gr1_ref))):
        cs = slice(half * half_w, (half + 1) * half_w)
        ya = jnp.dot(attn, wpa_ref[:, cs], preferred_element_type=F32)
        yr = jnp.dot(r, wpr_ref[:, cs], preferred_element_type=F32)
        y = _sigmoid(flat(ga_ref).astype(F32)) * ya + _sigmoid(flat(gr_ref).astype(F32)) * yr
        y_out[:, cs] = jnp.dot(unpermute, y.astype(BF16), preferred_element_type=F32).astype(BF16)


def _merge(outs, lses, r_out, proj, x2, wpa, wpr, wo, gpost, gffn, *, gate_col_block, tm=256):
    t, d = x2.shape
    batch, _, per, _ = proj.shape
    rows = tm // RES
    tiles_per_seq = per // rows
    n_tiles = t // tm
    const = lambda i: (0, 0)
    nat = pl.BlockSpec((tm, d), lambda i: (jnp.maximum(i - 1, 0), 0))

    def lay(width, col_block=0):
        def index(i):
            tile = jnp.minimum(i, n_tiles - 1)
            return (tile // tiles_per_seq, 0, tile % tiles_per_seq, col_block)
        return pl.BlockSpec((None, RES, rows, width), index)

    def resident(shape):
        return pl.BlockSpec(shape, const, pipeline_mode=pl.Buffered(1))

    lse_w = HEADS * LSE_LANES
    return pl.pallas_call(
        _merge_kernel,
        grid=(n_tiles + 1,),
        in_specs=[lay(GROUP_WIDTH)] * 3 + [lay(lse_w)] * 3 + [lay(GROUP_WIDTH)]
        + [lay(GROUP_WIDTH, gate_col_block + k) for k in range(4)]
        + [nat, resident(wpa.shape), resident(wpr.shape), resident(wo.shape),
           resident(gpost.shape), resident(gffn.shape)],
        out_specs=[nat, nat],
        out_shape=[jax.ShapeDtypeStruct((t, d), F32), jax.ShapeDtypeStruct((t, d), BF16)],
        scratch_shapes=[pltpu.VMEM((tm, GROUP_WIDTH), BF16), pltpu.VMEM((2, tm, d), BF16)],
        compiler_params=_params(("arbitrary",)),
        name="merge_out_proj",
    )(*outs, *lses, r_out, proj, proj, proj, proj, x2, wpa, wpr, wo, gpost, gffn)


def _ffn_hidden_kernel(a_ref, wg_ref, wu_ref, side_ref, o_ref, cast_ref, wg_b, wu_b):
    @pl.when(pl.program_id(1) == 0)
    def _():
        wg_b[...] = wg_ref[...].astype(BF16)
        wu_b[...] = wu_ref[...].astype(BF16)

    cast_ref[...] = side_ref[...].astype(BF16)
    a = a_ref[...]
    gt = jnp.dot(a, wg_b[...], preferred_element_type=F32)
    up = jnp.dot(a, wu_b[...], preferred_element_type=F32)
    o_ref[...] = (_silu(gt) * up).astype(o_ref.dtype)


def _ffn_out_kernel(hid_ref, wd_ref, h_ref, gain_ref, o_ref):
    ff = jnp.dot(hid_ref[...], wd_ref[...], preferred_element_type=F32)
    o_ref[...] = h_ref[...] + ff * _rms_scale(ff) * gain_ref[...]


def _ffn(a2, w_gate_up, w_down, h1, gain, *, tm_hidden=1024, tf=512, tm_out=512):
    t, d = a2.shape
    d_ff = w_down.shape[0]
    nf = d_ff // tf
    n_row = t // tm_hidden
    side = _side_cast_specs([w_down], nf * n_row, lambda j, i: j * n_row + i)

    def weight_block(first, lead):
        return pl.BlockSpec((d, tf), lambda j, i: (
            0, first + jnp.minimum(j + (i >= n_row - lead).astype(jnp.int32), nf - 1)))

    hidden, w_down = pl.pallas_call(
        _ffn_hidden_kernel,
        grid=(nf, n_row),
        in_specs=[
            pl.BlockSpec((tm_hidden, d), lambda j, i: (i, 0)),
            weight_block(0, 2),
            weight_block(nf, 1),
        ] + side,
        out_specs=[pl.BlockSpec((tm_hidden, tf), lambda j, i: (i, j))] + side,
        out_shape=[jax.ShapeDtypeStruct((t, d_ff), BF16), jax.ShapeDtypeStruct(w_down.shape, BF16)],
        scratch_shapes=[pltpu.VMEM((d, tf), BF16), pltpu.VMEM((d, tf), BF16)],
        compiler_params=_params(("arbitrary", "arbitrary")),
        name="swiglu_hidden",
    )(a2, w_gate_up, w_gate_up, w_down)
    rows = pl.BlockSpec((tm_out, d), lambda i: (i, 0))
    return pl.pallas_call(
        _ffn_out_kernel,
        grid=(t // tm_out,),
        in_specs=[
            pl.BlockSpec((tm_out, d_ff), lambda i: (i, 0)),
            pl.BlockSpec((d_ff, d), lambda i: (0, 0), pipeline_mode=pl.Buffered(1)),
            rows,
            pl.BlockSpec((1, d), lambda i: (0, 0)),
        ],
        out_specs=rows,
        out_shape=jax.ShapeDtypeStruct((t, d), F32),
        compiler_params=_params(("parallel",)),
        name="swiglu_out",
    )(hidden, w_down, h1, gain)


def _rotary_table(seq):
    inv_freq = ROPE_THETA ** (-np.arange(0, HEAD_DIM, 2, dtype=np.float64) / HEAD_DIM)
    ang = np.arange(seq, dtype=np.float64)[:, None] * inv_freq[None, :]
    table = np.concatenate([np.cos(ang), np.cos(ang), -np.sin(ang), np.sin(ang)], axis=-1)
    return jnp.asarray(table.reshape(seq // RES, RES, 2 * HEAD_DIM).transpose(1, 0, 2), F32)


def kernel(x, w_in, w_attn_branch, w_hgrn_branch, w_mix_out, hgrn_lower_bounds, hgrn_norm_gain,
           norm_mix_pre, norm_mix_post, w_ffn_gate_up, w_ffn_down, norm_ffn_pre, norm_ffn_post):
    batch, seq, d = x.shape
    assert w_in.shape[0] == 1, "single-layer block"
    x2 = x.reshape(batch * seq, d)
    row = lambda v: v[0].reshape(1, -1).astype(F32)

    for window, dilation in ATTN_GROUPS:
        assert window // dilation == ATTN_BACK and seq % window == 0 and RES % dilation == 0
    assert w_in.shape[2] == (QKV_BLOCKS + 8) * GROUP_WIDTH
    rotary = _rotary_table(seq)
    f32 = lambda wt: wt[0].astype(F32)
    bf16 = lambda wt: wt[0].astype(BF16)
    proj, keep = _in_proj(x2, row(norm_mix_pre), f32(w_in), rotary, hgrn_lower_bounds.astype(F32),
                          batch, seq)

    outs, lses, r_out = _mixers(proj, keep, row(hgrn_norm_gain), hgrn_col_block=QKV_BLOCKS)

    h1, a2 = _merge(outs, lses, r_out, proj, x2, bf16(w_attn_branch), bf16(w_hgrn_branch), bf16(w_mix_out),
                    row(norm_mix_post), row(norm_ffn_pre), gate_col_block=QKV_BLOCKS + 4)

    out = _ffn(a2, f32(w_ffn_gate_up), f32(w_ffn_down), h1, row(norm_ffn_post))
    return out.reshape(batch, seq, d)
```

```python
import math

import numpy as np
import jax
import jax.numpy as jnp
from jax import lax
from jax.experimental import pallas as pl
from jax.experimental.pallas import tpu as pltpu

F32 = jnp.float32
BF16 = jnp.bfloat16

HEAD_DIM = 128
ATTN_GROUPS = ((128, 1), (512, 4), (2048, 16))
N_GROUPS = len(ATTN_GROUPS)
HEADS = 8
GROUP_WIDTH = HEADS * HEAD_DIM
ROTARY_BLOCKS = 2 * N_GROUPS
QKV_BLOCKS = 3 * N_GROUPS
FORGET_BLOCK = QKV_BLOCKS + 1
ATTN_BACK = 128
ATTN_SUB_BLOCKS = 2
RES = 16
ROPE_THETA = 10000.0
NORM_EPS = 1e-6
HGRN_CHUNK = 128
HGRN_LEVELS = (64, 32, 16, 8, 4, 2, 1)
LSE_LANES = 16
BF16_SUBLANES = 16
NEG_BIG = -1e30
PERMUTE_GROUP = 256
LOG2E = math.log2(math.e)

VMEM_LIMIT_BYTES = 56 * 1024 * 1024


def _params(semantics):
    return pltpu.CompilerParams(dimension_semantics=semantics, vmem_limit_bytes=VMEM_LIMIT_BYTES)


def _rms_scale(v):
    return lax.rsqrt(jnp.mean(v * v, axis=-1, keepdims=True) + NORM_EPS)


def _sigmoid(v):
    return 0.5 * jnp.tanh(0.5 * v) + 0.5


def _silu(v):
    return v * _sigmoid(v)


def _side_cast_specs(weights, steps, step_index):
    specs = []
    for w in weights:
        rows, cols = w.shape
        block_rows = BF16_SUBLANES * pl.cdiv(rows, BF16_SUBLANES * steps)
        assert rows % block_rows == 0
        last = rows // block_rows - 1
        specs.append(pl.BlockSpec((block_rows, cols),
                                  lambda *idx, last=last: (jnp.minimum(step_index(*idx), last), 0)))
    return specs


def _block_position(i, rows_per_residue, block_rows):
    shift = rows_per_residue.bit_length() - 1
    n_res_shift = (block_rows // rows_per_residue).bit_length() - 1
    return ((i & (rows_per_residue - 1)) << n_res_shift) | lax.shift_right_logical(i, shift)


def _in_proj_kernel(x_lo_ref, x_hi_ref, gain_ref, w_ref, rot_ref, hlb_ref, o_ref, keep_ref, a_ref):
    half_rows = x_lo_ref.shape[0]
    tm = 2 * half_rows
    per_res = tm // RES
    grp = PERMUTE_GROUP
    per_grp = grp // RES
    j = pl.program_id(1)

    @pl.when(j == 0)
    def _():
        row_token = _block_position(lax.broadcasted_iota(jnp.int32, (grp, grp), 0), per_grp, grp)
        permute = jnp.where(row_token == lax.broadcasted_iota(jnp.int32, (grp, grp), 1), 1.0, 0.0).astype(BF16)
        gain = gain_ref[...]
        for gi in range(tm // grp):
            x_ref, first = (x_lo_ref, gi * grp) if gi * grp < half_rows else (x_hi_ref, gi * grp - half_rows)
            xg = x_ref[first:first + grp, :]
            ag = (xg * _rms_scale(xg) * gain).astype(BF16)
            pg = jnp.dot(permute, ag, preferred_element_type=F32).astype(BF16)
            for r in range(RES):
                dst = r * per_res + gi * per_grp
                a_ref[dst:dst + per_grp, :] = pg[r * per_grp:(r + 1) * per_grp, :]

    def project():
        return jnp.dot(a_ref[...], w_ref[...].astype(BF16), preferred_element_type=F32)

    def store(ref, val):
        ref[...] = val.reshape(RES, per_res, val.shape[-1]).astype(ref.dtype)

    @pl.when(j < ROTARY_BLOCKS)
    def _():
        res = project()
        scale = jnp.where(j < N_GROUPS, HEAD_DIM ** -0.5 * LOG2E, 1.0)
        table = rot_ref[...].reshape(tm, 2 * HEAD_DIM) * scale
        cos = table[:, :HEAD_DIM]
        sin = table[:, HEAD_DIM:]
        heads = []
        for h in range(HEADS):
            t = res[:, h * HEAD_DIM:(h + 1) * HEAD_DIM]
            heads.append(t * cos + pltpu.roll(t, HEAD_DIM // 2, 1) * sin)
        store(o_ref, jnp.concatenate(heads, axis=1))

    is_swish = (j == FORGET_BLOCK - 1) | (j == FORGET_BLOCK + 2)

    @pl.when(is_swish)
    def _():
        store(o_ref, _silu(project()))

    @pl.when((j >= ROTARY_BLOCKS) & (j != FORGET_BLOCK) & jnp.logical_not(is_swish))
    def _():
        store(o_ref, project())

    @pl.when(j == FORGET_BLOCK)
    def _():
        res = project()
        h0 = hlb_ref[0:1, :]
        h1 = hlb_ref[1:2, :]
        hm = jnp.maximum(h0, h1)
        e0 = jnp.exp(h0 - hm)
        e1 = jnp.exp(h1 - hm)
        lb = e0 / (e0 + e1)
        f = lb + (1.0 - lb) * jax.nn.sigmoid(res)
        store(o_ref, jnp.log(f))
        store(keep_ref, 1.0 - f)


def _in_proj(x2, gain, w, rotary, lower_bounds, batch, seq, *, tm=1024, tn=GROUP_WIDTH):
    assert lower_bounds.shape[0] == 2, "forget-gate bound rows: DEPTH + 1 with DEPTH == 1"
    t, d = x2.shape
    n = w.shape[1]
    tiles_per_seq = seq // tm
    table = pl.BlockSpec((RES, tm // RES, 2 * HEAD_DIM), lambda i, j: (0, i % tiles_per_seq, 0))
    lay = lambda col: pl.BlockSpec((None, RES, tm // RES, tn),
                                   lambda i, j: (i // tiles_per_seq, 0, i % tiles_per_seq, col(j)))
    n_row, n_col = t // tm, n // tn

    def x_half(which, lead):
        def index(i, j):
            tile = jnp.minimum(i + (j >= n_col - lead).astype(jnp.int32), n_row - 1)
            return (2 * tile + which, 0)
        return pl.BlockSpec((tm // 2, d), index)

    return pl.pallas_call(
        _in_proj_kernel,
        grid=(n_row, n_col),
        in_specs=[
            x_half(0, 2), x_half(1, 1),
            pl.BlockSpec((1, d), lambda i, j: (0, 0)),
            pl.BlockSpec((d, tn), lambda i, j: (0, j)),
            table,
            pl.BlockSpec(lower_bounds.shape, lambda i, j: (0, 0)),
        ],
        out_specs=[lay(lambda j: j), lay(lambda j: 0)],
        out_shape=[jax.ShapeDtypeStruct((batch, RES, seq // RES, n), BF16),
                   jax.ShapeDtypeStruct((batch, RES, seq // RES, tn), BF16)],
        scratch_shapes=[pltpu.VMEM((tm, d), BF16)],
        compiler_params=_params(("arbitrary", "arbitrary")),
        name="in_proj",
    )(x2, x2, gain, w, rotary, lower_bounds)


def _sub_blocks(ref, sl, nsub):
    v = ref[:, :, sl]
    a, u, w = v.shape
    step = u // nsub
    if step % BF16_SUBLANES:
        v = v.astype(F32)
    return [v[:, s * step:(s + 1) * step, :].reshape(a * step, w).astype(ref.dtype) for s in range(nsub)]


def _store_sub_blocks(ref, sl, blocks):
    a = ref.shape[0]
    parts = [b.reshape(a, b.shape[0] // a, b.shape[1]) for b in blocks]
    ref[:, :, sl] = jnp.concatenate(parts, axis=1).astype(ref.dtype)


def _attn_carry(n, kk_ref, vv_ref):
    blk = ATTN_BACK
    nsub = ATTN_SUB_BLOCKS

    @pl.when(n == 0)
    def _():
        kk_ref[0:blk, :] = jnp.zeros((blk, GROUP_WIDTH), BF16)
        vv_ref[0:blk, :] = jnp.zeros((blk, GROUP_WIDTH), BF16)

    @pl.when(n > 0)
    def _():
        kk_ref[0:blk, :] = kk_ref[nsub * blk:(nsub + 1) * blk, :]
        vv_ref[0:blk, :] = vv_ref[nsub * blk:(nsub + 1) * blk, :]


def _attn_body(n, q_ref, k_ref, v_ref, o_ref, lse_ref, kk_ref, vv_ref, *, rows_per_residue):
    blk = ATTN_BACK
    nsub = ATTN_SUB_BLOCKS

    qi = lax.broadcasted_iota(jnp.int32, (blk, 2 * blk), 0)
    kj = lax.broadcasted_iota(jnp.int32, (blk, 2 * blk), 1)
    q_pos = _block_position(qi, rows_per_residue, blk)
    k_pos = _block_position(kj & (blk - 1), rows_per_residue, blk) - jnp.where(kj < blk, blk, 0)
    dist = q_pos - k_pos
    band = (dist >= 0) & (dist <= ATTN_BACK)
    band_bias = jnp.where(band, 0.0, NEG_BIG)
    first_bias = jnp.where(band & (kj >= jnp.where(n > 0, 0, blk)), 0.0, NEG_BIG)
    layout_res, rows_blk, _ = lse_ref.shape
    step = rows_blk // nsub

    for h in range(HEADS):
        sl = slice(h * HEAD_DIM, (h + 1) * HEAD_DIM)
        for s, (kb, vb) in enumerate(zip(_sub_blocks(k_ref, sl, nsub), _sub_blocks(v_ref, sl, nsub))):
            rows = slice((s + 1) * blk, (s + 2) * blk)
            kk_ref[rows, sl] = kb
            vv_ref[rows, sl] = vb
        outs = []
        for s, qb in enumerate(_sub_blocks(q_ref, sl, nsub)):
            keys = slice(s * blk, (s + 2) * blk)
            sc = lax.dot_general(qb, kk_ref[keys, sl], (((1,), (1,)), ((), ())), preferred_element_type=F32)
            sc = sc + (first_bias if s == 0 else band_bias)
            m = jnp.max(sc, axis=-1, keepdims=True)
            p = jnp.exp2(sc - m)
            den = jnp.sum(p, axis=-1, keepdims=True)
            o = jnp.dot(p.astype(BF16), vv_ref[keys, sl], preferred_element_type=F32)
            outs.append(o / den)
            lse = (m * math.log(2.0) + jnp.log(den)).reshape(layout_res, step, 1)
            lse_ref[:, s * step:(s + 1) * step, h * LSE_LANES:(h + 1) * LSE_LANES] = jnp.broadcast_to(
                lse, (layout_res, step, LSE_LANES))
        _store_sub_blocks(o_ref, sl, outs)


def _attn_operands(proj, g, dilation):
    batch, _, per, _ = proj.shape
    a = RES // dilation
    rows = ATTN_BACK // a
    rows_blk = rows * ATTN_SUB_BLOCKS
    nblk = per // rows_blk
    view = lambda arr: arr.reshape(arr.shape[:-3] + (a, dilation) + arr.shape[-2:])

    def spec(width, col_block):
        return pl.BlockSpec((None, a, None, rows_blk, width),
                            lambda i: (i // (dilation * nblk), 0, (i // nblk) % dilation, i % nblk, col_block))

    lse_w = HEADS * LSE_LANES
    proj_v = view(proj)
    key_rows = (ATTN_SUB_BLOCKS + 1) * ATTN_BACK
    return dict(
        inputs=[proj_v, proj_v, proj_v],
        in_specs=[spec(GROUP_WIDTH, which * N_GROUPS + g) for which in range(3)],
        out_specs=[spec(GROUP_WIDTH, 0), spec(lse_w, 0)],
        out_shape=[jax.ShapeDtypeStruct((batch, a, dilation, per, GROUP_WIDTH), BF16),
                   jax.ShapeDtypeStruct((batch, a, dilation, per, lse_w), F32)],
        scratch=[pltpu.VMEM((key_rows, GROUP_WIDTH), BF16), pltpu.VMEM((key_rows, GROUP_WIDTH), BF16)],
        rows_per_residue=rows, nblk=nblk)


def _hgrn_cumsum_matrix():
    c = HGRN_CHUNK
    i = np.arange(c)
    per = c // RES
    pos = (i % per) * RES + i // per
    return (pos[None, :] <= pos[:, None]).astype(np.float32)


def _level_reference(b3, half):
    res, per, dk = b3.shape
    if half < res:
        two = 2 * half
        parts = [jnp.broadcast_to(b3[blk * two + half - 1][None], (two, per, dk)) for blk in range(res // two)]
        return parts[0] if len(parts) == 1 else jnp.concatenate(parts, axis=0)
    last = b3[res - 1]
    step = half // res
    u = lax.broadcasted_iota(jnp.int32, (per, dk), 0)
    ref = None
    for blk in reversed(range(per // (2 * step))):
        row = jnp.broadcast_to(last[blk * 2 * step + step - 1][None], (per, dk))
        ref = row if ref is None else jnp.where(u < (blk + 1) * 2 * step, row, ref)
    return jnp.broadcast_to(ref[None], (res, per, dk))


def _level_decay_exponent(b3, half):
    ref = _level_reference(b3, half)
    res, per, dk = b3.shape
    if half < res:
        parts = []
        for lo in range(0, res, 2 * half):
            up, dn = slice(lo, lo + half), slice(lo + half, lo + 2 * half)
            parts += [ref[up] - b3[up], b3[dn] - ref[dn]]
        return jnp.concatenate(parts, axis=0)
    step = half // res
    u = lax.broadcasted_iota(jnp.int32, (per, dk), 0)
    return (b3 - ref) * jnp.where((u & step) != 0, 1.0, -1.0)


def _hgrn_body(q_ref, logf_ref, v_ref, gate_ref, keep_ref, gain_ref, tri_ref, o_ref, st_ref):
    c = HGRN_CHUNK
    dk = HEAD_DIM
    per = c // RES

    g_all = logf_ref[...].astype(F32)
    k_all = keep_ref[...].astype(F32)
    q_all = q_ref[...].astype(F32)
    v_all = v_ref[...].astype(F32)
    gate_all = gate_ref[...].astype(F32)

    ti = _block_position(lax.broadcasted_iota(jnp.int32, (c, c), 0), per, c)
    si = _block_position(lax.broadcasted_iota(jnp.int32, (c, c), 1), per, c)
    top_bit = 31 - lax.clz(ti ^ si)
    n_levels = len(HGRN_LEVELS)
    level_id = jnp.where(ti > si, top_bit, jnp.where(ti == si, n_levels, n_levels + 1))
    level_b = level_id.astype(F32).astype(BF16)
    tri = tri_ref[...]

    n_chunks = o_ref.shape[1] // per
    outs = [[None] * n_chunks for _ in range(HEADS)]
    for ci in range(n_chunks):
        rows = slice(ci * per, (ci + 1) * per)
        g = g_all[:, rows, :].reshape(c, GROUP_WIDTH)
        g_hi = g.astype(BF16)
        r1 = g - g_hi.astype(F32)
        g_mid = r1.astype(BF16)
        g_lo = (r1 - g_mid.astype(F32)).astype(BF16)
        cum = (jnp.dot(tri, g_hi, preferred_element_type=F32)
               + jnp.dot(tri, g_mid, preferred_element_type=F32)
               + jnp.dot(tri, g_lo, preferred_element_type=F32))

        for h in range(HEADS):
            sl = slice(h * dk, (h + 1) * dk)
            q3 = q_all[:, rows, sl]
            k3 = k_all[:, rows, sl]
            q = q3.reshape(c, dk)
            k = k3.reshape(c, dk)
            v = v_all[:, rows, sl].reshape(c, dk).astype(BF16)
            b = cum[:, sl] * LOG2E
            b3 = b.reshape(RES, per, dk)
            b_last = b[c - 1:c, :]
            st = st_ref[h]

            q_dec = (q * jnp.exp2(b)).astype(BF16)
            o = lax.dot_general(q_dec, st.astype(BF16), (((1,), (1,)), ((), ())),
                                preferred_element_type=F32)

            diag = jnp.broadcast_to(jnp.sum(q * k, axis=-1, keepdims=True), (c, c)).astype(BF16)
            a = jnp.where(level_b == n_levels, diag, jnp.zeros((c, c), BF16))
            q_b = q.astype(BF16)
            k_b = k.astype(BF16)
            for half in HGRN_LEVELS:
                e = jnp.exp2(_level_decay_exponent(b3, half)).reshape(c, dk).astype(BF16)
                p_l = lax.dot_general(q_b * e, k_b * e, (((1,), (1,)), ((), ())), preferred_element_type=F32)
                a = jnp.where(level_b == half.bit_length() - 1, p_l.astype(BF16), a)
            o = o + jnp.dot(a, v, preferred_element_type=F32)

            k_dec = (k * jnp.exp2(b_last - b)).astype(BF16)
            st_ref[h] = st * jnp.exp2(b_last) + lax.dot_general(
                v, k_dec, (((0,), (0,)), ((), ())), preferred_element_type=F32)

            r = o * _rms_scale(o) * gain_ref[:, sl] * gate_all[:, rows, sl].reshape(c, dk)
            outs[h][ci] = r.reshape(RES, per, dk)

    for h in range(HEADS):
        o_ref[:, :, h * dk:(h + 1) * dk] = jnp.concatenate(outs[h], axis=1).astype(o_ref.dtype)


def _hgrn_operands(proj, keep, gain, *, first_col_block, rows):
    batch, _, per, _ = proj.shape
    tiles = per // rows
    tri = jnp.asarray(_hgrn_cumsum_matrix(), BF16)

    def col_spec(col_block):
        return pl.BlockSpec((None, RES, rows, GROUP_WIDTH), lambda i: (i // tiles, 0, i % tiles, col_block))

    const = lambda i: (0, 0)
    return dict(
        inputs=[proj, proj, proj, proj, keep, gain, tri],
        in_specs=[col_spec(first_col_block + k) for k in range(4)]
        + [col_spec(0), pl.BlockSpec(gain.shape, const), pl.BlockSpec(tri.shape, const)],
        out_specs=[col_spec(0)],
        out_shape=[jax.ShapeDtypeStruct((batch, RES, per, GROUP_WIDTH), BF16)],
        scratch=[pltpu.VMEM((HEADS, HEAD_DIM, HEAD_DIM), F32)],
        tiles=tiles)


def _mixers(proj, keep, gain, *, hgrn_col_block):
    batch, _, per, _ = proj.shape
    rows = ATTN_SUB_BLOCKS * ATTN_BACK // RES
    steps = batch * per // rows
    parts = [_attn_operands(proj, g, dilation) for g, (_, dilation) in enumerate(ATTN_GROUPS)]
    parts.append(_hgrn_operands(proj, keep, gain, first_col_block=hgrn_col_block, rows=rows))
    n_in = [len(p["inputs"]) for p in parts]
    n_out = [len(p["out_shape"]) for p in parts]
    n_scr = [len(p["scratch"]) for p in parts]

    def kernel_body(*refs):
        ins, rest = refs[:sum(n_in)], refs[sum(n_in):]
        outs, scr = rest[:sum(n_out)], rest[sum(n_out):]
        i = pl.program_id(0)
        take = lambda seq, counts, k: seq[sum(counts[:k]):sum(counts[:k + 1])]
        for k, part in enumerate(parts[:-1]):
            _attn_carry(i % part["nblk"], *take(scr, n_scr, k))
        k = len(parts) - 1

        @pl.when(i % parts[k]["tiles"] == 0)
        def _():
            st_ref = take(scr, n_scr, k)[0]
            st_ref[...] = jnp.zeros(st_ref.shape, F32)

        for k, part in enumerate(parts[:-1]):
            _attn_body(i % part["nblk"], *take(ins, n_in, k), *take(outs, n_out, k), *take(scr, n_scr, k),
                       rows_per_residue=part["rows_per_residue"])
        k = len(parts) - 1
        _hgrn_body(*take(ins, n_in, k), *take(outs, n_out, k), *take(scr, n_scr, k))

    flat = lambda key: [v for p in parts for v in p[key]]
    res = pl.pallas_call(
        kernel_body,
        grid=(steps,),
        in_specs=flat("in_specs"),
        out_specs=flat("out_specs"),
        out_shape=flat("out_shape"),
        scratch_shapes=flat("scratch"),
        compiler_params=_params(("arbitrary",)),
        name="token_mixers",
    )(*flat("inputs"))
    lse_w = HEADS * LSE_LANES
    outs = [res[2 * g].reshape(batch, RES, per, GROUP_WIDTH) for g in range(N_GROUPS)]
    lses = [res[2 * g + 1].reshape(batch, RES, per, lse_w) for g in range(N_GROUPS)]
    return outs, lses, res[2 * N_GROUPS]


def _merge_kernel(o1_ref, o2_ref, o3_ref, l1_ref, l2_ref, l3_ref, r_ref,
                  ga0_ref, ga1_ref, gr0_ref, gr1_ref, x_ref,
                  wpa_ref, wpr_ref, wo_ref, gpost_ref, gffn_ref,
                  h_ref, a2_ref, attn_s, y_s):
    i = pl.program_id(0)
    tm = x_ref.shape[0]
    per = tm // RES
    flat = lambda ref: ref[...].reshape(tm, ref.shape[-1])

    @pl.when(i == 0)
    def _():
        y_s[...] = jnp.zeros(y_s.shape, BF16)

    z = jnp.dot(y_s[...], wo_ref[...], preferred_element_type=F32)
    h1 = x_ref[...] + z * _rms_scale(z) * gpost_ref[...]
    h_ref[...] = h1
    a2_ref[...] = (h1 * _rms_scale(h1) * gffn_ref[...]).astype(BF16)

    la, lb, lc = flat(l1_ref), flat(l2_ref), flat(l3_ref)
    lm = jnp.maximum(la, jnp.maximum(lb, lc))
    ea, eb, ec = jnp.exp(la - lm), jnp.exp(lb - lm), jnp.exp(lc - lm)
    den = ea + eb + ec
    wa, wb, wc = ea / den, eb / den, ec / den
    o1, o2, o3 = flat(o1_ref), flat(o2_ref), flat(o3_ref)
    for h in range(HEADS):
        sl = slice(h * HEAD_DIM, (h + 1) * HEAD_DIM)
        c0 = h * LSE_LANES
        mixed = (wa[:, c0:c0 + 1] * o1[:, sl].astype(F32)
                 + wb[:, c0:c0 + 1] * o2[:, sl].astype(F32)
                 + wc[:, c0:c0 + 1] * o3[:, sl].astype(F32))
        attn_s[:, sl] = mixed.astype(BF16)

    tok = lax.broadcasted_iota(jnp.int32, (tm, tm), 0)
    src = _block_position(lax.broadcasted_iota(jnp.int32, (tm, tm), 1), per, tm)
    unpermute = jnp.where(src == tok, 1.0, 0.0).astype(BF16)

    attn = attn_s[...]
    r = flat(r_ref)
    half_w = GROUP_WIDTH
    y_out = y_s
    for half, (ga_ref, gr_ref) in enumerate(((ga0_ref, gr0_ref), (ga1_ref, gr1_ref))):
        cs = slice(half * half_w, (half + 1) * half_w)
        ya = jnp.dot(attn, wpa_ref[:, cs], preferred_element_type=F32)
        yr = jnp.dot(r, wpr_ref[:, cs], preferred_element_type=F32)
        y = _sigmoid(flat(ga_ref).astype(F32)) * ya + _sigmoid(flat(gr_ref).astype(F32)) * yr
        y_out[:, cs] = jnp.dot(unpermute, y.astype(BF16), preferred_element_type=F32).astype(BF16)


def _merge(outs, lses, r_out, proj, x2, wpa, wpr, wo, gpost, gffn, *, gate_col_block, tm=256):
    t, d = x2.shape
    batch, _, per, _ = proj.shape
    rows = tm // RES
    tiles_per_seq = per // rows
    n_tiles = t // tm
    const = lambda i: (0, 0)
    nat = pl.BlockSpec((tm, d), lambda i: (jnp.maximum(i - 1, 0), 0))

    def lay(width, col_block=0):
        def index(i):
            tile = jnp.minimum(i, n_tiles - 1)
            return (tile // tiles_per_seq, 0, tile % tiles_per_seq, col_block)
        return pl.BlockSpec((None, RES, rows, width), index)

    def resident(shape):
        return pl.BlockSpec(shape, const, pipeline_mode=pl.Buffered(1))

    lse_w = HEADS * LSE_LANES
    return pl.pallas_call(
        _merge_kernel,
        grid=(n_tiles + 1,),
        in_specs=[lay(GROUP_WIDTH)] * 3 + [lay(lse_w)] * 3 + [lay(GROUP_WIDTH)]
        + [lay(GROUP_WIDTH, gate_col_block + k) for k in range(4)]
        + [nat, resident(wpa.shape), resident(wpr.shape), resident(wo.shape),
           resident(gpost.shape), resident(gffn.shape)],
        out_specs=[nat, nat],
        out_shape=[jax.ShapeDtypeStruct((t, d), F32), jax.ShapeDtypeStruct((t, d), BF16)],
        scratch_shapes=[pltpu.VMEM((tm, GROUP_WIDTH), BF16), pltpu.VMEM((tm, d), BF16)],
        compiler_params=_params(("arbitrary",)),
        name="merge_out_proj",
    )(*outs, *lses, r_out, proj, proj, proj, proj, x2, wpa, wpr, wo, gpost, gffn)


def _ffn_hidden_kernel(a_ref, wg_ref, wu_ref, side_ref, o_ref, cast_ref, wg_b, wu_b):
    @pl.when(pl.program_id(1) == 0)
    def _():
        wg_b[...] = wg_ref[...].astype(BF16)
        wu_b[...] = wu_ref[...].astype(BF16)

    cast_ref[...] = side_ref[...].astype(BF16)
    a = a_ref[...]
    gt = jnp.dot(a, wg_b[...], preferred_element_type=F32)
    up = jnp.dot(a, wu_b[...], preferred_element_type=F32)
    o_ref[...] = (_silu(gt) * up).astype(o_ref.dtype)


def _ffn_out_kernel(hid_ref, wd_ref, h_ref, gain_ref, o_ref):
    ff = jnp.dot(hid_ref[...], wd_ref[...], preferred_element_type=F32)
    o_ref[...] = h_ref[...] + ff * _rms_scale(ff) * gain_ref[...]


def _ffn(a2, w_gate_up, w_down, h1, gain, *, tm_hidden=1024, tf=512, tm_out=512):
    t, d = a2.shape
    d_ff = w_down.shape[0]
    nf = d_ff // tf
    n_row = t // tm_hidden
    side = _side_cast_specs([w_down], nf * n_row, lambda j, i: j * n_row + i)

    def weight_block(first, lead):
        return pl.BlockSpec((d, tf), lambda j, i: (
            0, first + jnp.minimum(j + (i >= n_row - lead).astype(jnp.int32), nf - 1)))

    hidden, w_down = pl.pallas_call(
        _ffn_hidden_kernel,
        grid=(nf, n_row),
        in_specs=[
            pl.BlockSpec((tm_hidden, d), lambda j, i: (i, 0)),
            weight_block(0, 2),
            weight_block(nf, 1),
        ] + side,
        out_specs=[pl.BlockSpec((tm_hidden, tf), lambda j, i: (i, j))] + side,
        out_shape=[jax.ShapeDtypeStruct((t, d_ff), BF16), jax.ShapeDtypeStruct(w_down.shape, BF16)],
        scratch_shapes=[pltpu.VMEM((d, tf), BF16), pltpu.VMEM((d, tf), BF16)],
        compiler_params=_params(("arbitrary", "arbitrary")),
        name="swiglu_hidden",
    )(a2, w_gate_up, w_gate_up, w_down)
    rows = pl.BlockSpec((tm_out, d), lambda i: (i, 0))
    return pl.pallas_call(
        _ffn_out_kernel,
        grid=(t // tm_out,),
        in_specs=[
            pl.BlockSpec((tm_out, d_ff), lambda i: (i, 0)),
            pl.BlockSpec((d_ff, d), lambda i: (0, 0), pipeline_mode=pl.Buffered(1)),
            rows,
            pl.BlockSpec((1, d), lambda i: (0, 0)),
        ],
        out_specs=rows,
        out_shape=jax.ShapeDtypeStruct((t, d), F32),
        compiler_params=_params(("parallel",)),
        name="swiglu_out",
    )(hidden, w_down, h1, gain)


def _rotary_table(seq):
    inv_freq = ROPE_THETA ** (-np.arange(0, HEAD_DIM, 2, dtype=np.float64) / HEAD_DIM)
    ang = np.arange(seq, dtype=np.float64)[:, None] * inv_freq[None, :]
    table = np.concatenate([np.cos(ang), np.cos(ang), -np.sin(ang), np.sin(ang)], axis=-1)
    return jnp.asarray(table.reshape(seq // RES, RES, 2 * HEAD_DIM).transpose(1, 0, 2), F32)


def kernel(x, w_in, w_attn_branch, w_hgrn_branch, w_mix_out, hgrn_lower_bounds, hgrn_norm_gain,
           norm_mix_pre, norm_mix_post, w_ffn_gate_up, w_ffn_down, norm_ffn_pre, norm_ffn_post):
    batch, seq, d = x.shape
    assert w_in.shape[0] == 1, "single-layer block"
    x2 = x.reshape(batch * seq, d)
    row = lambda v: v[0].reshape(1, -1).astype(F32)

    for window, dilation in ATTN_GROUPS:
        assert window // dilation == ATTN_BACK and seq % window == 0 and RES % dilation == 0
    assert w_in.shape[2] == (QKV_BLOCKS + 8) * GROUP_WIDTH
    rotary = _rotary_table(seq)
    f32 = lambda wt: wt[0].astype(F32)
    bf16 = lambda wt: wt[0].astype(BF16)
    proj, keep = _in_proj(x2, row(norm_mix_pre), f32(w_in), rotary, hgrn_lower_bounds.astype(F32),
                          batch, seq)

    outs, lses, r_out = _mixers(proj, keep, row(hgrn_norm_gain), hgrn_col_block=QKV_BLOCKS)

    h1, a2 = _merge(outs, lses, r_out, proj, x2, bf16(w_attn_branch), bf16(w_hgrn_branch), bf16(w_mix_out),
                    row(norm_mix_post), row(norm_ffn_pre), gate_col_block=QKV_BLOCKS + 4)

    out = _ffn(a2, f32(w_ffn_gate_up), f32(w_ffn_down), h1, row(norm_ffn_post))
    return out.reshape(batch, seq, d)
```
